```python
import functools
import jax, jax.numpy as jnp
from jax import lax
import numpy as np

D_MODEL = 1024
BATCH = 2
SEQ = 8192
DEPTH = 2
DEC_BATCH = 32
DEC_SEQ = 16
PAST_LEN = 2048

CHUNK = 64
Q_BLOCK = 128
SB_HEADS = 8
SB_HEAD_DIM = 64
SB_WIDTH = SB_HEADS * SB_HEAD_DIM
HG_HEADS = 4
HG_HEAD_DIM = 128
HG_WIDTH = HG_HEADS * HG_HEAD_DIM
MIX_WIDTH = SB_WIDTH + HG_WIDTH
IN_COLS = 3 * SB_WIDTH + 4 * HG_WIDTH
SPLITS = [SB_WIDTH, 2 * SB_WIDTH, 3 * SB_WIDTH, 3 * SB_WIDTH + HG_WIDTH,
          3 * SB_WIDTH + 2 * HG_WIDTH, 3 * SB_WIDTH + 3 * HG_WIDTH]
N_MEM = 256
MEM_HEADS = 4
MEM_HEAD_DIM = 128
MEM_WIDTH = MEM_HEADS * MEM_HEAD_DIM
D_FF = 4 * D_MODEL
EPS = 1e-6

kernel_name = 'hybrid_sb_hgrn2_stream_encoder'


def rmsnorm(x, gain):
    x32 = x.astype(jnp.float32)
    y = x32 * lax.rsqrt(jnp.mean(x32 * x32, axis=-1, keepdims=True) + EPS)
    return (y * gain.astype(jnp.float32)).astype(x.dtype)


def stick_breaking(q, k, v, q_pos, k_pos):
    z = jnp.einsum('bqhd,bkhd->bhqk', q.astype(jnp.float32), k.astype(jnp.float32)) * (SB_HEAD_DIM ** -0.5)
    mask = (k_pos[None, :] < q_pos[:, None])[None, None]
    log_keep = jnp.where(mask, -jax.nn.softplus(z), 0.0)
    rev = lax.cumsum(log_keep, axis=3, reverse=True)
    excl = jnp.concatenate([rev[..., 1:], jnp.zeros_like(rev[..., :1])], axis=-1)
    a = jnp.where(mask, jnp.exp(jax.nn.log_sigmoid(z) + excl), 0.0)
    return jnp.einsum('bhqk,bkhd->bqhd', a, v.astype(jnp.float32))


def sb_prompt(q, k, v):
    b, t, h, d = q.shape
    nb = t // Q_BLOCK
    pos = jnp.arange(t, dtype=jnp.int32)
    qb = q.reshape(b, nb, Q_BLOCK, h, d).transpose(1, 0, 2, 3, 4)
    pb = pos.reshape(nb, Q_BLOCK)
    out = lax.map(lambda a: stick_breaking(a[0], k, v, a[1], pos), (qb, pb))
    return out.transpose(1, 0, 2, 3, 4).reshape(b, t, h, d)


def sb_sample(past_k, past_v, q, k, v):
    p = past_k.shape[1]
    t = q.shape[1]
    k_all = jnp.concatenate([past_k.astype(k.dtype), k], axis=1)
    v_all = jnp.concatenate([past_v.astype(v.dtype), v], axis=1)
    q_pos = p + jnp.arange(t, dtype=jnp.int32)
    k_pos = jnp.arange(p + t, dtype=jnp.int32)
    return stick_breaking(q, k_all, v_all, q_pos, k_pos)


def hgrn2_chunked(q, k, v, log_f, s0):
    b, t, h, dk = q.shape
    dv = v.shape[-1]
    ln = min(CHUNK, t)
    nc = t // ln

    def to_chunks(a):
        return a.astype(jnp.float32).reshape(b, nc, ln, h, a.shape[-1]).transpose(1, 0, 2, 3, 4)

    tri = jnp.tril(jnp.ones((ln, ln), dtype=bool))[None, :, :, None, None]

    def step(s, inp):
        qc, kc, vc, gc = inp
        cum = jnp.cumsum(gc, axis=1)
        o_inter = jnp.einsum('blhk,bhkv->blhv', qc * jnp.exp(cum), s)
        diff = cum[:, :, None] - cum[:, None, :]
        decay = jnp.exp(jnp.where(tri, diff, -jnp.inf))
        scores = jnp.einsum('bthk,bshk,btshk->bths', qc, kc, decay)
        o_intra = jnp.einsum('bths,bshv->bthv', scores, vc)
        last = cum[:, -1]
        s_new = jnp.exp(last)[..., None] * s + jnp.einsum(
            'bshk,bshv->bhkv', kc * jnp.exp(last[:, None] - cum), vc)
        return s_new, o_inter + o_intra

    s_fin, o = lax.scan(step, s0.astype(jnp.float32),
                        (to_chunks(q), to_chunks(k), to_chunks(v), to_chunks(log_f)))
    o = o.transpose(1, 0, 2, 3, 4).reshape(b, t, h, dv)
    return o, s_fin


def memory_kv(mem, gain, w_mk, w_mv):
    b = mem.shape[0]
    h = rmsnorm(mem, gain)
    mk = (h @ w_mk).reshape(b, N_MEM, MEM_HEADS, MEM_HEAD_DIM)
    mv = (h @ w_mv).reshape(b, N_MEM, MEM_HEADS, MEM_HEAD_DIM)
    return mk, mv


def layer_forward(x, sb_fn, s0, mem_k, mem_v, lb, norm_mix, w_in, sb_gain, hg_gain, w_out,
                  norm_mem_q, w_mq, w_mo, norm_ffn, w_ffn1, w_ffn2):
    bx, t, _ = x.shape
    h = rmsnorm(x, norm_mix)
    proj = h @ w_in
    q_sb, k_sb, v_sb, q_hg, f_hg, i_hg, g_hg = jnp.split(proj, SPLITS, axis=-1)
    q_sb = q_sb.reshape(bx, t, SB_HEADS, SB_HEAD_DIM)
    k_sb = k_sb.reshape(bx, t, SB_HEADS, SB_HEAD_DIM)
    v_sb = v_sb.reshape(bx, t, SB_HEADS, SB_HEAD_DIM)
    o_sb = rmsnorm(sb_fn(q_sb, k_sb, v_sb).reshape(bx, t, SB_WIDTH), sb_gain)
    z = f_hg.astype(jnp.float32).reshape(bx, t, HG_HEADS, HG_HEAD_DIM)
    lb_h = lb.reshape(HG_HEADS, HG_HEAD_DIM)
    log_f = jnp.logaddexp(jnp.log(lb_h), jnp.log1p(-lb_h) + jax.nn.log_sigmoid(z))
    k_hg = (1.0 - lb_h) * jax.nn.sigmoid(-z)
    q_hg = q_hg.astype(jnp.float32).reshape(bx, t, HG_HEADS, HG_HEAD_DIM) * (HG_HEAD_DIM ** -0.5)
    i_hg = i_hg.reshape(bx, t, HG_HEADS, HG_HEAD_DIM)
    o_hg, s_new = hgrn2_chunked(q_hg, k_hg, i_hg, log_f, s0)
    gate = jax.nn.silu(g_hg.astype(jnp.float32).reshape(bx, t, HG_HEADS, HG_HEAD_DIM))
    o_hg = (rmsnorm(o_hg, hg_gain) * gate).reshape(bx, t, HG_WIDTH)
    mix = jnp.concatenate([o_sb.astype(x.dtype), o_hg.astype(x.dtype)], axis=-1)
    x = x + mix @ w_out
    h = rmsnorm(x, norm_mem_q)
    qm = (h @ w_mq).reshape(bx, t, MEM_HEADS, MEM_HEAD_DIM)
    sc = jnp.einsum('bqhd,bmhd->bhqm', qm.astype(jnp.float32), mem_k.astype(jnp.float32)) * (MEM_HEAD_DIM ** -0.5)
    p = jax.nn.softmax(sc, axis=-1)
    om = jnp.einsum('bhqm,bmhd->bqhd', p, mem_v.astype(jnp.float32)).reshape(bx, t, MEM_WIDTH)
    x = x + om.astype(x.dtype) @ w_mo
    h = rmsnorm(x, norm_ffn)
    x = x + jnp.square(jax.nn.relu(h @ w_ffn1)) @ w_ffn2
    return x, k_sb, v_sb, s_new


def setup_inputs(seed: int = 0) -> dict:
    key = jax.random.key(seed)
    ks = jax.random.split(key, 24)

    def nrm(k, shape, scale):
        return jax.random.normal(k, shape, jnp.float32) * scale

    def gain(k, shape):
        return 1.0 + 0.02 * jax.random.normal(k, shape, jnp.float32)

    return {
        'x_prompt': nrm(ks[0], (BATCH, SEQ, D_MODEL), 1.0),
        'x_sample': nrm(ks[1], (DEC_BATCH, DEC_SEQ, D_MODEL), 1.0),
        'mem_prompt': nrm(ks[2], (BATCH, N_MEM, D_MODEL), 1.0),
        'cache_sb_k': nrm(ks[3], (DEPTH, DEC_BATCH, PAST_LEN, SB_HEADS, SB_HEAD_DIM), 1.0),
        'cache_sb_v': nrm(ks[4], (DEPTH, DEC_BATCH, PAST_LEN, SB_HEADS, SB_HEAD_DIM), 1.0),
        'state_hgrn': nrm(ks[5], (DEPTH, DEC_BATCH, HG_HEADS, HG_HEAD_DIM, HG_HEAD_DIM), 0.3),
        'cache_mem_k': nrm(ks[6], (DEPTH, DEC_BATCH, N_MEM, MEM_HEADS, MEM_HEAD_DIM), 1.0),
        'cache_mem_v': nrm(ks[7], (DEPTH, DEC_BATCH, N_MEM, MEM_HEADS, MEM_HEAD_DIM), 1.0),
        'lb_logits': nrm(ks[8], (DEPTH, HG_WIDTH), 0.5),
        'norm_mix': gain(ks[9], (DEPTH, D_MODEL)),
        'w_in': nrm(ks[10], (DEPTH, D_MODEL, IN_COLS), D_MODEL ** -0.5),
        'sb_gain': gain(ks[11], (DEPTH, SB_WIDTH)),
        'hg_gain': gain(ks[12], (DEPTH, HG_HEADS, HG_HEAD_DIM)),
        'w_out': nrm(ks[13], (DEPTH, MIX_WIDTH, D_MODEL), MIX_WIDTH ** -0.5),
        'norm_mem_q': gain(ks[14], (DEPTH, D_MODEL)),
        'norm_mem_kv': gain(ks[15], (DEPTH, D_MODEL)),
        'w_mq': nrm(ks[16], (DEPTH, D_MODEL, MEM_WIDTH), D_MODEL ** -0.5),
        'w_mk': nrm(ks[17], (DEPTH, D_MODEL, MEM_WIDTH), D_MODEL ** -0.5),
        'w_mv': nrm(ks[18], (DEPTH, D_MODEL, MEM_WIDTH), D_MODEL ** -0.5),
        'w_mo': nrm(ks[19], (DEPTH, MEM_WIDTH, D_MODEL), MEM_WIDTH ** -0.5),
        'norm_ffn': gain(ks[20], (DEPTH, D_MODEL)),
        'w_ffn1': nrm(ks[21], (DEPTH, D_MODEL, D_FF), D_MODEL ** -0.5),
        'w_ffn2': nrm(ks[22], (DEPTH, D_FF, D_MODEL), D_FF ** -0.5),
        'norm_final': gain(ks[23], (D_MODEL,)),
    }


def reference(x_prompt, x_sample, mem_prompt, cache_sb_k, cache_sb_v, state_hgrn, cache_mem_k, cache_mem_v,
              lb_logits, norm_mix, w_in, sb_gain, hg_gain, w_out, norm_mem_q, norm_mem_kv, w_mq, w_mk, w_mv,
              w_mo, norm_ffn, w_ffn1, w_ffn2, norm_final):
    lb_all = jnp.cumsum(jax.nn.softmax(lb_logits.astype(jnp.float32), axis=0), axis=0)
    lb_all = lb_all - lb_all[:1]
    xp = x_prompt
    xs = x_sample
    kp_l, vp_l, sp_l, mkp_l, mvp_l, ks_l, vs_l, ss_l = [], [], [], [], [], [], [], []
    for l in range(DEPTH):
        shared = (lb_all[l], norm_mix[l], w_in[l], sb_gain[l], hg_gain[l], w_out[l],
                  norm_mem_q[l], w_mq[l], w_mo[l], norm_ffn[l], w_ffn1[l], w_ffn2[l])
        mk_p, mv_p = memory_kv(mem_prompt, norm_mem_kv[l], w_mk[l], w_mv[l])
        s0_p = jnp.zeros((xp.shape[0], HG_HEADS, HG_HEAD_DIM, HG_HEAD_DIM), jnp.float32)
        xp, kp, vp, sp = layer_forward(xp, sb_prompt, s0_p, mk_p, mv_p, *shared)
        xs, kn, vn, sn = layer_forward(xs, functools.partial(sb_sample, cache_sb_k[l], cache_sb_v[l]),
                                       state_hgrn[l], cache_mem_k[l], cache_mem_v[l], *shared)
        kp_l.append(kp); vp_l.append(vp); sp_l.append(sp); mkp_l.append(mk_p); mvp_l.append(mv_p)
        ks_l.append(kn); vs_l.append(vn); ss_l.append(sn)
    y_prompt = rmsnorm(xp, norm_final)
    y_sample = rmsnorm(xs, norm_final)
    return (y_prompt, y_sample, jnp.stack(kp_l), jnp.stack(vp_l), jnp.stack(sp_l), jnp.stack(mkp_l),
            jnp.stack(mvp_l), jnp.stack(ks_l), jnp.stack(vs_l), jnp.stack(ss_l))
```

```python
import functools

import jax
import jax.numpy as jnp
import numpy as np
from jax import lax
from jax.experimental import pallas as pl
from jax.experimental.pallas import tpu as pltpu

D_MODEL = 1024
DEPTH = 2
SB_HEADS = 8
SB_HEAD_DIM = 64
SB_WIDTH = SB_HEADS * SB_HEAD_DIM
HG_HEADS = 4
HG_HEAD_DIM = 128
HG_WIDTH = HG_HEADS * HG_HEAD_DIM
IN_COLS = 3 * SB_WIDTH + 4 * HG_WIDTH
N_MEM = 256
MEM_HEADS = 4
MEM_HEAD_DIM = 128
MEM_WIDTH = MEM_HEADS * MEM_HEAD_DIM
D_FF = 4 * D_MODEL
EPS = 1e-6

LANES = 128
VMEM_LIMIT = 56 * 1024 * 1024
F32 = jnp.float32
BF16 = jnp.bfloat16

_NT = (((1,), (1,)), ((), ()))
_TN = (((0,), (0,)), ((), ()))


def _params(*sem):
    return pltpu.CompilerParams(dimension_semantics=sem, vmem_limit_bytes=VMEM_LIMIT)


def _const_spec(shape):
    nd = len(shape)
    return pl.BlockSpec(shape, lambda *_: (0,) * nd, pipeline_mode=pl.Buffered(1))


def _log2(n):
    assert n > 0 and n & (n - 1) == 0, n
    return n.bit_length() - 1


def _rms(x, gain):
    ms = jnp.mean(x * x, axis=-1, keepdims=True)
    return x * lax.rsqrt(ms + EPS) * gain


def _log_sigmoid_parts(z):
    l = jnp.log(1.0 + jnp.exp(-jnp.abs(z)))
    ls = jnp.minimum(z, 0.0) - l
    return ls, ls - z


def _split_bf16(x):
    hi = x.astype(BF16)
    lo = (x - hi.astype(F32)).astype(BF16)
    return hi, lo


def _memkv_kernel(mem_ref, g_ref, wk_ref, wv_ref, mk_ref, mv_ref):
    h = _rms(mem_ref[...], g_ref[0]).astype(BF16)
    mk_ref[0] = jnp.dot(h, wk_ref[0], preferred_element_type=F32)
    mv_ref[0] = jnp.dot(h, wv_ref[0], preferred_element_type=F32)


def _memkv(mem2d, gains, wk, wv):
    n = mem2d.shape[0]
    out = jax.ShapeDtypeStruct((DEPTH, n, MEM_WIDTH), F32)
    return pl.pallas_call(
        _memkv_kernel,
        grid=(DEPTH,),
        in_specs=[
            pl.BlockSpec((n, D_MODEL), lambda l: (0, 0)),
            pl.BlockSpec((1, 1, D_MODEL), lambda l: (l, 0, 0)),
            pl.BlockSpec((1, D_MODEL, MEM_WIDTH), lambda l: (l, 0, 0)),
            pl.BlockSpec((1, D_MODEL, MEM_WIDTH), lambda l: (l, 0, 0)),
        ],
        out_specs=[pl.BlockSpec((1, n, MEM_WIDTH), lambda l: (l, 0, 0))] * 2,
        out_shape=[out, out],
        compiler_params=_params("arbitrary"),
        name="memkv",
    )(mem2d, gains.reshape(DEPTH, 1, D_MODEL), wk, wv)


def _proj_kernel(x_ref, g_ref, w_ref, qsb_ref, ksb_ref, vsb_ref, kbf_ref, vbf_ref, hg_ref):
    h = _rms(x_ref[...], g_ref[...]).astype(BF16)

    def col(c):
        return jnp.dot(h, w_ref[:, c * SB_WIDTH:(c + 1) * SB_WIDTH], preferred_element_type=F32)

    qsb_ref[...] = (col(0) * (SB_HEAD_DIM ** -0.5)).astype(BF16)
    k = col(1)
    ksb_ref[...] = k
    kbf_ref[...] = k.astype(BF16)
    v = col(2)
    vsb_ref[...] = v
    vbf_ref[...] = v.astype(BF16)
    for c in range(4):
        hg_ref[:, c * HG_WIDTH:(c + 1) * HG_WIDTH] = col(3 + c)


def _proj(x2d, gain, w_in, tm):
    n = x2d.shape[0]
    row = lambda w: pl.BlockSpec((tm, w), lambda i: (i, 0))
    return pl.pallas_call(
        _proj_kernel,
        grid=(n // tm,),
        in_specs=[row(D_MODEL), _const_spec((1, D_MODEL)), _const_spec((D_MODEL, IN_COLS))],
        out_specs=[row(SB_WIDTH)] * 5 + [row(4 * HG_WIDTH)],
        out_shape=[
            jax.ShapeDtypeStruct((n, SB_WIDTH), BF16),
            jax.ShapeDtypeStruct((n, SB_WIDTH), F32),
            jax.ShapeDtypeStruct((n, SB_WIDTH), F32),
            jax.ShapeDtypeStruct((n, SB_WIDTH), BF16),
            jax.ShapeDtypeStruct((n, SB_WIDTH), BF16),
            jax.ShapeDtypeStruct((n, 4 * HG_WIDTH), F32),
        ],
        compiler_params=_params("arbitrary"),
        name="proj",
    )(x2d, gain.reshape(1, D_MODEL), w_in)


SB_BK = LANES


def _suffix_matrix():
    j = np.arange(SB_BK)[:, None]
    s = np.arange(SB_BK)[None, :]
    u = np.concatenate([(j > s).astype(np.float32), np.ones((SB_BK, SB_BK), np.float32)], axis=1)
    return jnp.asarray(np.concatenate([u, u], axis=0), dtype=BF16)


def _sbp_kernel(q_ref, k_ref, v_ref, u_ref, o_ref, acc_ref, ca_ref, cb_ref, *, bq):
    bk = SB_BK
    qi = pl.program_id(2)
    q = q_ref[...]
    lane = lax.broadcasted_iota(jnp.int32, (1, LANES), 1)
    m_a = (lane < SB_HEAD_DIM).astype(BF16)
    m_b = (lane >= SB_HEAD_DIM).astype(BF16)
    acc_ref[...] = jnp.zeros_like(acc_ref)
    ca_ref[...] = jnp.zeros_like(ca_ref)
    cb_ref[...] = jnp.zeros_like(cb_ref)
    rpos = qi * bq + lax.broadcasted_iota(jnp.int32, (bq, 1), 0)
    u = u_ref[...]
    nkb = (qi + 1) * (bq // bk)

    def body(t, carry):
        j = nkb - 1 - t
        r0 = pl.multiple_of(j * bk, bk)
        ks = k_ref[pl.ds(r0, bk), :]
        vs = v_ref[pl.ds(r0, bk), :]
        kcat = jnp.concatenate([ks * m_a, ks * m_b], axis=0)
        vcat = jnp.concatenate([vs * m_a, vs * m_b], axis=0)
        z = lax.dot_general(q, kcat, _NT, preferred_element_type=F32)
        cpos = r0 + lax.broadcasted_iota(jnp.int32, (1, bk), 1)
        mask = cpos < rpos
        a_parts = []
        for hh, c_ref in ((0, ca_ref), (1, cb_ref)):
            zh = z[:, hh * bk:(hh + 1) * bk]
            ls, lk = _log_sigmoid_parts(zh)
            lk = jnp.where(mask, lk, 0.0)
            hi, lo = _split_bf16(lk)
            cs = jnp.dot(jnp.concatenate([hi, lo], axis=1), u, preferred_element_type=F32)
            run = c_ref[...]
            a = jnp.where(mask, jnp.exp(ls + cs[:, :bk] + run), 0.0)
            c_ref[...] = run + cs[:, bk:]
            a_parts.append(a.astype(BF16))
        acc_ref[...] += jnp.dot(jnp.concatenate(a_parts, axis=1), vcat, preferred_element_type=F32)
        return carry

    lax.fori_loop(0, nkb, body, 0)
    o_ref[...] = acc_ref[...]


def _sb_prompt(qb, kb, vb, nb, t, bq):
    nq = t // bq
    hp = SB_WIDTH // LANES
    return pl.pallas_call(
        functools.partial(_sbp_kernel, bq=bq),
        grid=(nb, hp, nq),
        in_specs=[
            pl.BlockSpec((bq, LANES), lambda b, h, i: (b * nq + i, h)),
            pl.BlockSpec((t, LANES), lambda b, h, i: (b, h)),
            pl.BlockSpec((t, LANES), lambda b, h, i: (b, h)),
            _const_spec((2 * SB_BK, 2 * SB_BK)),
        ],
        out_specs=pl.BlockSpec((bq, LANES), lambda b, h, i: (b * nq + i, h)),
        out_shape=jax.ShapeDtypeStruct((nb * t, SB_WIDTH), F32),
        scratch_shapes=[pltpu.VMEM((bq, LANES), F32)] * 3,
        compiler_params=_params("arbitrary", "arbitrary", "arbitrary"),
        name="sb_prompt",
    )(qb, kb, vb, _suffix_matrix())


SBS_PAD = 16


def _prefix_matrix():
    s = np.arange(SB_BK)[:, None]
    j = np.arange(SB_BK)[None, :]
    u = np.concatenate([(j > s).astype(np.float32), np.ones((SBS_PAD, SB_BK), np.float32)], axis=0)
    return jnp.asarray(np.concatenate([u, u], axis=1), dtype=BF16)


def _sbs_kernel(q_ref, kn_ref, vn_ref, kc_ref, vc_ref, u_ref, o_ref, acc_ref, run_ref, *, tq, past):
    bk = SB_BK
    nl = SB_HEADS * tq
    q = q_ref[...]
    row = lax.broadcasted_iota(jnp.int32, (nl, SB_WIDTH), 0)
    colw = lax.broadcasted_iota(jnp.int32, (nl, SB_WIDTH), 1)
    same_head = (row >> _log2(tq)) == (colw >> _log2(SB_HEAD_DIM))
    qx = jnp.where(same_head, jnp.concatenate([q] * SB_HEADS, axis=0), jnp.zeros((), BF16))
    u = u_ref[...]
    acc_ref[...] = jnp.zeros_like(acc_ref)
    run_ref[...] = jnp.zeros_like(run_ref)

    def block(kb, vb, mask):
        z = lax.dot_general(kb, qx, _NT, preferred_element_type=F32)
        ls, lk = _log_sigmoid_parts(z)
        if mask is not None:
            lk = jnp.where(mask, lk, 0.0)
        hi, lo = _split_bf16(lk)
        cs = jnp.dot(u, jnp.concatenate([hi, lo], axis=0), preferred_element_type=F32)
        run = run_ref[...]
        a = jnp.exp(ls + cs[:bk] + run)
        if mask is not None:
            a = jnp.where(mask, a, 0.0)
        run_ref[...] = run + cs[bk:bk + 1]
        acc_ref[...] += lax.dot_general(a.astype(BF16), vb, _TN, preferred_element_type=F32)

    pad = jnp.zeros((bk - tq, SB_WIDTH), BF16)
    srow = lax.broadcasted_iota(jnp.int32, (bk, nl), 0)
    tlane = lax.broadcasted_iota(jnp.int32, (bk, nl), 1) & (tq - 1)
    block(jnp.concatenate([kn_ref[...], pad], axis=0), jnp.concatenate([vn_ref[...], pad], axis=0),
          srow < tlane)

    def body(t, carry):
        r0 = pl.multiple_of((past // bk - 1 - t) * bk, bk)
        block(kc_ref[0, pl.ds(r0, bk), :].astype(BF16), vc_ref[0, pl.ds(r0, bk), :].astype(BF16), None)
        return carry

    lax.fori_loop(0, past // bk, body, 0)
    accm = jnp.where(same_head, acc_ref[...], 0.0)
    out = accm[0:tq]
    for h in range(1, SB_HEADS):
        out = out + accm[h * tq:(h + 1) * tq]
    o_ref[...] = out


def _sb_sample(qb, kb, vb, kcache, vcache, layer, nb, tq):
    past = kcache.shape[1]
    rowspec = pl.BlockSpec((tq, SB_WIDTH), lambda s: (s, 0))
    cspec = pl.BlockSpec((1, past, SB_WIDTH), lambda s: (layer * nb + s, 0, 0))
    return pl.pallas_call(
        functools.partial(_sbs_kernel, tq=tq, past=past),
        grid=(nb,),
        in_specs=[rowspec, rowspec, rowspec, cspec, cspec, _const_spec((SB_BK + SBS_PAD, 2 * SB_BK))],
        out_specs=rowspec,
        out_shape=jax.ShapeDtypeStruct((nb * tq, SB_WIDTH), F32),
        scratch_shapes=[pltpu.VMEM((SB_HEADS * tq, SB_WIDTH), F32), pltpu.VMEM((1, SB_HEADS * tq), F32)],
        compiler_params=_params("arbitrary"),
        name="sb_sample",
    )(qb, kb, vb, kcache, vcache, _prefix_matrix())


def _hgrn_consts(c):
    levels = []
    m = 1
    while m < c:
        levels.append(m)
        m *= 2
    t = np.arange(c)[:, None]
    u = np.arange(c)[None, :]
    mats = [(u <= t).astype(np.float32)]
    masks = [np.eye(c, dtype=np.float32)]
    for m in levels:
        end_a = (t // (2 * m)) * 2 * m + m - 1
        in_b = (t % (2 * m)) >= m
        w = np.where(in_b, (u > end_a) & (u <= t), False).astype(np.float32)
        w = w - np.where(~in_b, (u > t) & (u <= end_a), False).astype(np.float32)
        mats.append(w)
        masks.append(((t // (2 * m)) == (u // (2 * m))).astype(np.float32))
    w = np.concatenate(mats, axis=0)
    return tuple(levels), jnp.asarray(np.concatenate([w, w], axis=1), dtype=BF16), jnp.asarray(np.stack(masks))


def _hgrn_kernel(hg_ref, lbl_ref, gain_ref, s0_ref, w_ref, mk_ref, o_ref, sfin_ref, st_ref,
                 *, c, levels, layer, nchunks):
    step = pl.program_id(1)
    hd = HG_HEAD_DIM

    @pl.when(step == 0)
    def _():
        for h in range(HG_HEADS):
            st_ref[h] = s0_ref[0, h].T

    lg = lbl_ref[...]
    ex = jnp.exp(lg - jnp.max(lg, axis=0, keepdims=True))
    sm = ex / jnp.sum(ex, axis=0, keepdims=True)
    csum = sm[0:1]
    for l in range(1, layer + 1):
        csum = csum + sm[l:l + 1]
    lb = csum - sm[0:1]

    zf = hg_ref[:, HG_WIDTH:2 * HG_WIDTH]
    ez = jnp.exp(-jnp.abs(zf))
    lsig = jnp.minimum(zf, 0.0) - jnp.log(1.0 + ez)
    bterm = jnp.log(1.0 - lb) + lsig
    la = jnp.log(jnp.maximum(lb, 1e-30))
    logf_mix = jnp.maximum(la, bterm) + jnp.log(1.0 + jnp.exp(-jnp.abs(la - bterm)))
    logf = jnp.where(lb > 0.0, logf_mix, bterm)
    khg = (1.0 - lb) * (jnp.where(zf >= 0.0, ez, 1.0) / (1.0 + ez))

    hi, lo = _split_bf16(logf)
    dd = jnp.dot(w_ref[...], jnp.concatenate([hi, lo], axis=0), preferred_element_type=F32)

    rowi = lax.broadcasted_iota(jnp.int32, (c, 1), 0)
    for h in range(HG_HEADS):
        sl = slice(h * hd, (h + 1) * hd)
        q = hg_ref[:, h * hd:(h + 1) * hd] * (hd ** -0.5)
        k = khg[:, sl]
        v = hg_ref[:, 2 * HG_WIDTH + h * hd:2 * HG_WIDTH + (h + 1) * hd]
        g = hg_ref[:, 3 * HG_WIDTH + h * hd:3 * HG_WIDTH + (h + 1) * hd]
        vb = v.astype(BF16)
        cum = dd[0:c, sl]
        st = st_ref[h]
        o = lax.dot_general((q * jnp.exp(cum)).astype(BF16), st.astype(BF16), _NT,
                            preferred_element_type=F32)
        sc = lax.dot_general(q.astype(BF16), k.astype(BF16), _NT, preferred_element_type=F32) * mk_ref[0]
        for li, m in enumerate(levels):
            e = jnp.exp(-jnp.abs(dd[(li + 1) * c:(li + 2) * c, sl]))
            second = (rowi & m) != 0
            qs = jnp.where(second, q * e, 0.0).astype(BF16)
            ks = jnp.where(second, 0.0, k * e).astype(BF16)
            sc = sc + lax.dot_general(qs, ks, _NT, preferred_element_type=F32) * mk_ref[li + 1]
        o = o + jnp.dot(sc.astype(BF16), vb, preferred_element_type=F32)
        last = cum[c - 1:c, :]
        kdec = (k * jnp.exp(last - cum)).astype(BF16)
        st_new = st * jnp.exp(last) + lax.dot_general(vb, kdec, _TN, preferred_element_type=F32)
        st_ref[h] = st_new

        eg = jnp.exp(-jnp.abs(g))
        gate = g * (jnp.where(g >= 0.0, 1.0, eg) / (1.0 + eg))
        o_ref[:, sl] = _rms(o, gain_ref[:, sl]) * gate

        @pl.when(step == nchunks - 1)
        def _():
            sfin_ref[0, h] = st_new.T


def _hgrn(hg, lb_logits, hg_gain, s0, s0_base, layer, nb, t):
    c = min(LANES, t)
    nchunks = t // c
    levels, wcat, masks = _hgrn_consts(c)
    nl = len(levels) + 1
    return pl.pallas_call(
        functools.partial(_hgrn_kernel, c=c, levels=levels, layer=layer, nchunks=nchunks),
        grid=(nb, nchunks),
        in_specs=[
            pl.BlockSpec((c, 4 * HG_WIDTH), lambda b, s: (b * nchunks + s, 0)),
            _const_spec((DEPTH, HG_WIDTH)),
            _const_spec((1, HG_WIDTH)),
            pl.BlockSpec((1, HG_HEADS, HG_HEAD_DIM, HG_HEAD_DIM), lambda b, s: (s0_base + b, 0, 0, 0)),
            _const_spec((nl * c, 2 * c)),
            _const_spec((nl, c, c)),
        ],
        out_specs=[
            pl.BlockSpec((c, HG_WIDTH), lambda b, s: (b * nchunks + s, 0)),
            pl.BlockSpec((1, HG_HEADS, HG_HEAD_DIM, HG_HEAD_DIM), lambda b, s: (b, 0, 0, 0)),
        ],
        out_shape=[
            jax.ShapeDtypeStruct((nb * t, HG_WIDTH), F32),
            jax.ShapeDtypeStruct((nb, HG_HEADS, HG_HEAD_DIM, HG_HEAD_DIM), F32),
        ],
        scratch_shapes=[pltpu.VMEM((HG_HEADS, HG_HEAD_DIM, HG_HEAD_DIM), F32)],
        compiler_params=_params("arbitrary", "arbitrary"),
        name="hgrn",
    )(hg, lb_logits, hg_gain.reshape(1, HG_WIDTH), s0, wcat, masks)


def _mix_kernel(x_ref, osb_ref, ohg_ref, sbg_ref, wout_ref, nmq_ref, wmq_ref, x1_ref, qm_ref):
    a = _rms(osb_ref[...], sbg_ref[...]).astype(BF16)
    b = ohg_ref[...].astype(BF16)
    x1 = (x_ref[...]
          + jnp.dot(a, wout_ref[0:SB_WIDTH, :], preferred_element_type=F32)
          + jnp.dot(b, wout_ref[SB_WIDTH:, :], preferred_element_type=F32))
    x1_ref[...] = x1
    h = _rms(x1, nmq_ref[...]).astype(BF16)
    qm_ref[...] = jnp.dot(h, wmq_ref[...], preferred_element_type=F32).astype(BF16)


def _mix(x2d, osb, ohg, sb_gain, w_out, norm_mem_q, w_mq, tm):
    n = x2d.shape[0]
    row = lambda w: pl.BlockSpec((tm, w), lambda i: (i, 0))
    return pl.pallas_call(
        _mix_kernel,
        grid=(n // tm,),
        in_specs=[row(D_MODEL), row(SB_WIDTH), row(HG_WIDTH), _const_spec((1, SB_WIDTH)),
                  _const_spec((D_MODEL, D_MODEL)), _const_spec((1, D_MODEL)),
                  _const_spec((D_MODEL, MEM_WIDTH))],
        out_specs=[row(D_MODEL), row(MEM_WIDTH)],
        out_shape=[jax.ShapeDtypeStruct((n, D_MODEL), F32), jax.ShapeDtypeStruct((n, MEM_WIDTH), BF16)],
        compiler_params=_params("arbitrary"),
        name="mix",
    )(x2d, osb, ohg, sb_gain.reshape(1, SB_WIDTH), w_out, norm_mem_q.reshape(1, D_MODEL), w_mq)


def _memattn_kernel(q_ref, mk_ref, mv_ref, o_ref):
    hd = MEM_HEAD_DIM
    for h in range(MEM_HEADS):
        sl = slice(h * hd, (h + 1) * hd)
        s = lax.dot_general(q_ref[:, sl], mk_ref[0, :, sl].astype(BF16), _NT,
                            preferred_element_type=F32) * (hd ** -0.5)
        p = jnp.exp(s - jnp.max(s, axis=-1, keepdims=True))
        den = jnp.sum(p, axis=-1, keepdims=True)
        o = jnp.dot(p.astype(BF16), mv_ref[0, :, sl].astype(BF16), preferred_element_type=F32)
        o_ref[:, sl] = (o / den).astype(BF16)


def _memattn(qm, mk3, mv3, base, nb, t, tm):
    nt = t // tm
    mspec = pl.BlockSpec((1, N_MEM, MEM_WIDTH), lambda b, i: (base + b, 0, 0))
    return pl.pallas_call(
        _memattn_kernel,
        grid=(nb, nt),
        in_specs=[pl.BlockSpec((tm, MEM_WIDTH), lambda b, i: (b * nt + i, 0)), mspec, mspec],
        out_specs=pl.BlockSpec((tm, MEM_WIDTH), lambda b, i: (b * nt + i, 0)),
        out_shape=jax.ShapeDtypeStruct((nb * t, MEM_WIDTH), BF16),
        compiler_params=_params("arbitrary", "arbitrary"),
        name="memattn",
    )(qm, mk3, mv3)


FF_CHUNK = 1024


def _ffn_kernel(x1_ref, om_ref, wmo_ref, nf_ref, w1_ref, w2_ref, nfin_ref, o_ref, *, final):
    x2 = x1_ref[...] + jnp.dot(om_ref[...], wmo_ref[...], preferred_element_type=F32)
    h = _rms(x2, nf_ref[...]).astype(BF16)
    x3 = x2
    for c in range(D_FF // FF_CHUNK):
        cs = slice(c * FF_CHUNK, (c + 1) * FF_CHUNK)
        r = jnp.maximum(jnp.dot(h, w1_ref[:, cs], preferred_element_type=F32), 0.0)
        x3 = x3 + jnp.dot((r * r).astype(BF16), w2_ref[cs, :], preferred_element_type=F32)
    o_ref[...] = _rms(x3, nfin_ref[...]) if final else x3


def _ffn(x1, om, w_mo, norm_ffn, w1, w2, norm_final, final, tm):
    n = x1.shape[0]
    row = lambda w: pl.BlockSpec((tm, w), lambda i: (i, 0))
    return pl.pallas_call(
        functools.partial(_ffn_kernel, final=final),
        grid=(n // tm,),
        in_specs=[row(D_MODEL), row(MEM_WIDTH), _const_spec((MEM_WIDTH, D_MODEL)),
                  _const_spec((1, D_MODEL)), _const_spec((D_MODEL, D_FF)), _const_spec((D_FF, D_MODEL)),
                  _const_spec((1, D_MODEL))],
        out_specs=row(D_MODEL),
        out_shape=jax.ShapeDtypeStruct((n, D_MODEL), F32),
        compiler_params=_params("arbitrary"),
        name="ffn",
    )(x1, om, w_mo, norm_ffn.reshape(1, D_MODEL), w1, w2, norm_final.reshape(1, D_MODEL))


TM = 256
SB_BQ = 256


def kernel(x_prompt, x_sample, mem_prompt, cache_sb_k, cache_sb_v, state_hgrn, cache_mem_k, cache_mem_v,
           lb_logits, norm_mix, w_in, sb_gain, hg_gain, w_out, norm_mem_q, norm_mem_kv, w_mq, w_mk, w_mv,
           w_mo, norm_ffn, w_ffn1, w_ffn2, norm_final):
    nbp, tp, _ = x_prompt.shape
    nbs, ts, _ = x_sample.shape
    past = cache_sb_k.shape[2]
    w_in_b, w_out_b, w_mq_b, w_mk_b, w_mv_b, w_mo_b, w1_b, w2_b = (
        w.astype(BF16) for w in (w_in, w_out, w_mq, w_mk, w_mv, w_mo, w_ffn1, w_ffn2))
    lbl = lb_logits.astype(F32)

    mk_p, mv_p = _memkv(mem_prompt.reshape(nbp * N_MEM, D_MODEL), norm_mem_kv, w_mk_b, w_mv_b)
    mk_p3 = mk_p.reshape(DEPTH * nbp, N_MEM, MEM_WIDTH)
    mv_p3 = mv_p.reshape(DEPTH * nbp, N_MEM, MEM_WIDTH)
    mk_s3 = cache_mem_k.reshape(DEPTH * nbs, N_MEM, MEM_WIDTH)
    mv_s3 = cache_mem_v.reshape(DEPTH * nbs, N_MEM, MEM_WIDTH)
    kc3 = cache_sb_k.reshape(DEPTH * nbs, past, SB_WIDTH)
    vc3 = cache_sb_v.reshape(DEPTH * nbs, past, SB_WIDTH)
    s0_s = state_hgrn.reshape(DEPTH * nbs, HG_HEADS, HG_HEAD_DIM, HG_HEAD_DIM)
    s0_p = jnp.zeros((nbp, HG_HEADS, HG_HEAD_DIM, HG_HEAD_DIM), F32)

    def layer(l, x2d, nb, t, prompt):
        qb, k32, v32, kb, vb, hg = _proj(x2d, norm_mix[l], w_in_b[l], TM)
        if prompt:
            osb = _sb_prompt(qb, kb, vb, nb, t, SB_BQ)
            ohg, sfin = _hgrn(hg, lbl, hg_gain[l], s0_p, 0, l, nb, t)
            mk3, mv3, base = mk_p3, mv_p3, l * nb
        else:
            osb = _sb_sample(qb, kb, vb, kc3, vc3, l, nb, t)
            ohg, sfin = _hgrn(hg, lbl, hg_gain[l], s0_s, l * nb, l, nb, t)
            mk3, mv3, base = mk_s3, mv_s3, l * nb
        x1, qm = _mix(x2d, osb, ohg, sb_gain[l], w_out_b[l], norm_mem_q[l], w_mq_b[l], TM)
        om = _memattn(qm, mk3, mv3, base, nb, t, min(TM, t))
        xo = _ffn(x1, om, w_mo_b[l], norm_ffn[l], w1_b[l], w2_b[l], norm_final, l == DEPTH - 1, TM)
        return xo, k32, v32, sfin

    xp = x_prompt.reshape(nbp * tp, D_MODEL)
    xs = x_sample.reshape(nbs * ts, D_MODEL)
    kp_l, vp_l, sp_l, ks_l, vs_l, ss_l = [], [], [], [], [], []
    for l in range(DEPTH):
        xp, kp, vp, sp = layer(l, xp, nbp, tp, True)
        xs, kn, vn, sn = layer(l, xs, nbs, ts, False)
        kp_l.append(kp); vp_l.append(vp); sp_l.append(sp)
        ks_l.append(kn); vs_l.append(vn); ss_l.append(sn)

    sbp = (DEPTH, nbp, tp, SB_HEADS, SB_HEAD_DIM)
    sbs = (DEPTH, nbs, ts, SB_HEADS, SB_HEAD_DIM)
    memp = (DEPTH, nbp, N_MEM, MEM_HEADS, MEM_HEAD_DIM)
    return (xp.reshape(nbp, tp, D_MODEL), xs.reshape(nbs, ts, D_MODEL),
            jnp.stack(kp_l).reshape(sbp), jnp.stack(vp_l).reshape(sbp), jnp.stack(sp_l),
            mk_p.reshape(memp), mv_p.reshape(memp),
            jnp.stack(ks_l).reshape(sbs), jnp.stack(vs_l).reshape(sbs), jnp.stack(ss_l))
```

```python
import functools

import jax
import jax.numpy as jnp
import numpy as np
from jax import lax
from jax.experimental import pallas as pl
from jax.experimental.pallas import tpu as pltpu

D_MODEL = 1024
DEPTH = 2
SB_HEADS = 8
SB_HEAD_DIM = 64
SB_WIDTH = SB_HEADS * SB_HEAD_DIM
HG_HEADS = 4
HG_HEAD_DIM = 128
HG_WIDTH = HG_HEADS * HG_HEAD_DIM
IN_COLS = 3 * SB_WIDTH + 4 * HG_WIDTH
N_MEM = 256
MEM_HEADS = 4
MEM_HEAD_DIM = 128
MEM_WIDTH = MEM_HEADS * MEM_HEAD_DIM
D_FF = 4 * D_MODEL
EPS = 1e-6

LANES = 128
VMEM_LIMIT = 56 * 1024 * 1024
F32 = jnp.float32
BF16 = jnp.bfloat16

_NT = (((1,), (1,)), ((), ()))
_TN = (((0,), (0,)), ((), ()))


def _params(*sem):
    return pltpu.CompilerParams(dimension_semantics=sem, vmem_limit_bytes=VMEM_LIMIT)


def _const_spec(shape):
    nd = len(shape)
    return pl.BlockSpec(shape, lambda *_: (0,) * nd, pipeline_mode=pl.Buffered(1))


def _log2(n):
    assert n > 0 and n & (n - 1) == 0, n
    return n.bit_length() - 1


def _rms(x, gain):
    ms = jnp.mean(x * x, axis=-1, keepdims=True)
    return x * lax.rsqrt(ms + EPS) * gain


def _log_sigmoid_parts(z):
    l = jnp.log(1.0 + jnp.exp(-jnp.abs(z)))
    ls = jnp.minimum(z, 0.0) - l
    return ls, ls - z


def _split_bf16(x):
    hi = x.astype(BF16)
    lo = (x - hi.astype(F32)).astype(BF16)
    return hi, lo


def _memkv_kernel(mem_ref, g_ref, wk_ref, wv_ref, mk_ref, mv_ref):
    h = _rms(mem_ref[...], g_ref[0]).astype(BF16)
    mk_ref[0] = jnp.dot(h, wk_ref[0], preferred_element_type=F32)
    mv_ref[0] = jnp.dot(h, wv_ref[0], preferred_element_type=F32)


def _memkv(mem2d, gains, wk, wv):
    n = mem2d.shape[0]
    out = jax.ShapeDtypeStruct((DEPTH, n, MEM_WIDTH), F32)
    return pl.pallas_call(
        _memkv_kernel,
        grid=(DEPTH,),
        in_specs=[
            pl.BlockSpec((n, D_MODEL), lambda l: (0, 0)),
            pl.BlockSpec((1, 1, D_MODEL), lambda l: (l, 0, 0)),
            pl.BlockSpec((1, D_MODEL, MEM_WIDTH), lambda l: (l, 0, 0)),
            pl.BlockSpec((1, D_MODEL, MEM_WIDTH), lambda l: (l, 0, 0)),
        ],
        out_specs=[pl.BlockSpec((1, n, MEM_WIDTH), lambda l: (l, 0, 0))] * 2,
        out_shape=[out, out],
        compiler_params=_params("arbitrary"),
        name="memkv",
    )(mem2d, gains.reshape(DEPTH, 1, D_MODEL), wk, wv)


def _proj_kernel(x_ref, g_ref, w_ref, *refs, slab):
    qsb_ref, ksb_ref, vsb_ref, kbf_ref, vbf_ref, hg_ref = refs[-6:]
    h = _rms(x_ref[...], g_ref[...]).astype(BF16)

    def col(c):
        return jnp.dot(h, w_ref[:, c * SB_WIDTH:(c + 1) * SB_WIDTH], preferred_element_type=F32)

    def put(ref, val):
        for s in range(ref.shape[0]):
            ref[s] = val if s == slab else jnp.zeros_like(val)

    qsb_ref[...] = (col(0) * (SB_HEAD_DIM ** -0.5)).astype(BF16)
    k = col(1)
    put(ksb_ref, k)
    kbf_ref[...] = k.astype(BF16)
    v = col(2)
    put(vsb_ref, v)
    vbf_ref[...] = v.astype(BF16)
    for c in range(4):
        hg_ref[:, c * HG_WIDTH:(c + 1) * HG_WIDTH] = col(3 + c)


def _proj(x2d, gain, w_in, tm, layer, kv_all):
    n = x2d.shape[0]
    row = lambda w: pl.BlockSpec((tm, w), lambda i: (i, 0))
    in_specs = [row(D_MODEL), _const_spec((1, D_MODEL)), _const_spec((D_MODEL, IN_COLS))]
    args = (x2d, gain.reshape(1, D_MODEL), w_in)
    if kv_all is None:
        kv_spec, slab, aliases = pl.BlockSpec((DEPTH, tm, SB_WIDTH), lambda i: (0, i, 0)), layer, {}
    else:
        kv_spec, slab, aliases = pl.BlockSpec((1, tm, SB_WIDTH), lambda i: (layer, i, 0)), 0, {3: 1, 4: 2}
        in_specs += [pl.BlockSpec(memory_space=pl.ANY)] * 2
        args += tuple(kv_all)
    return pl.pallas_call(
        functools.partial(_proj_kernel, slab=slab),
        grid=(n // tm,),
        in_specs=in_specs,
        out_specs=[row(SB_WIDTH), kv_spec, kv_spec, row(SB_WIDTH), row(SB_WIDTH), row(4 * HG_WIDTH)],
        out_shape=[
            jax.ShapeDtypeStruct((n, SB_WIDTH), BF16),
            jax.ShapeDtypeStruct((DEPTH, n, SB_WIDTH), F32),
            jax.ShapeDtypeStruct((DEPTH, n, SB_WIDTH), F32),
            jax.ShapeDtypeStruct((n, SB_WIDTH), BF16),
            jax.ShapeDtypeStruct((n, SB_WIDTH), BF16),
            jax.ShapeDtypeStruct((n, 4 * HG_WIDTH), F32),
        ],
        input_output_aliases=aliases,
        compiler_params=_params("arbitrary"),
        name="proj",
    )(*args)


SB_BK = LANES
SB_DEAD = -104.0


def _suffix_matrix():
    j = np.arange(SB_BK)[:, None]
    s = np.arange(SB_BK)[None, :]
    u = np.concatenate([(j > s).astype(np.float32), np.ones((SB_BK, SB_BK), np.float32)], axis=1)
    return jnp.asarray(np.concatenate([u, u], axis=0), dtype=BF16)


def _sbp_kernel(q_ref, k_ref, v_ref, u_ref, o_ref, acc_ref, ca_ref, cb_ref, *, bq):
    bk = SB_BK
    qi = pl.program_id(2)
    q = q_ref[...]
    lane = lax.broadcasted_iota(jnp.int32, (1, LANES), 1)
    m_a = (lane < SB_HEAD_DIM).astype(BF16)
    m_b = (lane >= SB_HEAD_DIM).astype(BF16)
    acc_ref[...] = jnp.zeros_like(acc_ref)
    ca_ref[...] = jnp.zeros_like(ca_ref)
    cb_ref[...] = jnp.zeros_like(cb_ref)
    u = u_ref[...]
    nd = bq // bk
    assert nd % 2 == 0, "earlier key blocks are taken two per trip"

    def block(j, rel):
        r0 = pl.multiple_of(j * bk, bk)
        ks = k_ref[pl.ds(r0, bk), :]
        vs = v_ref[pl.ds(r0, bk), :]
        kcat = jnp.concatenate([ks * m_a, ks * m_b], axis=0)
        vcat = jnp.concatenate([vs * m_a, vs * m_b], axis=0)
        z = lax.dot_general(q, kcat, _NT, preferred_element_type=F32)
        if rel is not None:
            mask = (lax.broadcasted_iota(jnp.int32, (bq, bk), 1) + rel
                    < lax.broadcasted_iota(jnp.int32, (bq, bk), 0))
        a_parts = []
        for hh, c_ref in ((0, ca_ref), (1, cb_ref)):
            zh = z[:, hh * bk:(hh + 1) * bk]
            ls, lk = _log_sigmoid_parts(zh)
            if rel is not None:
                lk = jnp.where(mask, lk, 0.0)
            hi, lo = _split_bf16(lk)
            cs = jnp.dot(jnp.concatenate([hi, lo], axis=1), u, preferred_element_type=F32)
            run = c_ref[...]
            a = jnp.exp(ls + cs[:, :bk] + run)
            if rel is not None:
                a = jnp.where(mask, a, 0.0)
            c_ref[...] = run + cs[:, bk:]
            a_parts.append(a.astype(BF16))
        acc_ref[...] += jnp.dot(jnp.concatenate(a_parts, axis=1), vcat, preferred_element_type=F32)

    for d in range(nd - 1, -1, -1):
        block(qi * nd + d, d * bk)

    def live():
        return jnp.maximum(jnp.max(ca_ref[...]), jnp.max(cb_ref[...])) > SB_DEAD

    def cond(state):
        j, alive = state
        return jnp.logical_and(j >= 0, alive)

    def body(state):
        j, _ = state
        block(j, None)
        block(j - 1, None)
        return j - 2, live()

    lax.while_loop(cond, body, (qi * nd - 1, live()))
    o_ref[...] = acc_ref[...]


def _sb_prompt(qb, kb, vb, nb, t, bq):
    nq = t // bq
    hp = SB_WIDTH // LANES
    return pl.pallas_call(
        functools.partial(_sbp_kernel, bq=bq),
        grid=(nb, hp, nq),
        in_specs=[
            pl.BlockSpec((bq, LANES), lambda b, h, i: (b * nq + i, h)),
            pl.BlockSpec((t, LANES), lambda b, h, i: (b, h)),
            pl.BlockSpec((t, LANES), lambda b, h, i: (b, h)),
            _const_spec((2 * SB_BK, 2 * SB_BK)),
        ],
        out_specs=pl.BlockSpec((bq, LANES), lambda b, h, i: (b * nq + i, h)),
        out_shape=jax.ShapeDtypeStruct((nb * t, SB_WIDTH), F32),
        scratch_shapes=[pltpu.VMEM((bq, LANES), F32)] * 3,
        compiler_params=_params("arbitrary", "arbitrary", "arbitrary"),
        name="sb_prompt",
    )(qb, kb, vb, _suffix_matrix())


SBS_PAD = 16


def _prefix_matrix():
    s = np.arange(SB_BK)[:, None]
    j = np.arange(SB_BK)[None, :]
    u = np.concatenate([(j > s).astype(np.float32), np.ones((SBS_PAD, SB_BK), np.float32)], axis=0)
    return jnp.asarray(np.concatenate([u, u], axis=1), dtype=BF16)


def _sbs_kernel(q_ref, kn_ref, vn_ref, kc_ref, vc_ref, u_ref, o_ref, alive_ref, acc_ref, run_ref,
                *, tq, past):
    bk = SB_BK
    nl = SB_HEADS * tq
    q = q_ref[...]
    row = lax.broadcasted_iota(jnp.int32, (nl, SB_WIDTH), 0)
    colw = lax.broadcasted_iota(jnp.int32, (nl, SB_WIDTH), 1)
    same_head = (row >> _log2(tq)) == (colw >> _log2(SB_HEAD_DIM))
    qx = jnp.where(same_head, jnp.concatenate([q] * SB_HEADS, axis=0), jnp.zeros((), BF16))
    u = u_ref[...]
    acc_ref[...] = jnp.zeros_like(acc_ref)
    run_ref[...] = jnp.zeros_like(run_ref)

    def block(kb, vb, mask):
        z = lax.dot_general(kb, qx, _NT, preferred_element_type=F32)
        ls, lk = _log_sigmoid_parts(z)
        if mask is not None:
            lk = jnp.where(mask, lk, 0.0)
        hi, lo = _split_bf16(lk)
        cs = jnp.dot(u, jnp.concatenate([hi, lo], axis=0), preferred_element_type=F32)
        run = run_ref[...]
        a = jnp.exp(ls + cs[:bk] + run)
        if mask is not None:
            a = jnp.where(mask, a, 0.0)
        run_ref[...] = run + cs[bk:bk + 1]
        acc_ref[...] += lax.dot_general(a.astype(BF16), vb, _TN, preferred_element_type=F32)

    pad = jnp.zeros((bk - tq, SB_WIDTH), BF16)
    srow = lax.broadcasted_iota(jnp.int32, (bk, nl), 0)
    tlane = lax.broadcasted_iota(jnp.int32, (bk, nl), 1) & (tq - 1)
    block(jnp.concatenate([kn_ref[...], pad], axis=0), jnp.concatenate([vn_ref[...], pad], axis=0),
          srow < tlane)

    def live():
        return jnp.max(run_ref[...]) > SB_DEAD

    def cond(state):
        j, alive = state
        return jnp.logical_and(j >= 0, alive)

    def body(state):
        j, _ = state
        r0 = pl.multiple_of(j * bk, bk)
        block(kc_ref[0, pl.ds(r0, bk), :].astype(BF16), vc_ref[0, pl.ds(r0, bk), :].astype(BF16), None)
        return j - 1, live()

    _, alive = lax.while_loop(cond, body, (past // bk - 1, live()))
    alive_ref[...] = jnp.where(alive, jnp.ones(alive_ref.shape, F32), jnp.zeros(alive_ref.shape, F32))
    accm = jnp.where(same_head, acc_ref[...], 0.0)
    out = accm[0:tq]
    for h in range(1, SB_HEADS):
        out = out + accm[h * tq:(h + 1) * tq]
    o_ref[...] = out


def _sb_sample(qb, kb, vb, kcache, vcache, layer, nb, tq):
    past = kcache.shape[1]
    rowspec = pl.BlockSpec((tq, SB_WIDTH), lambda s: (s, 0))
    cspec = pl.BlockSpec((1, past, SB_WIDTH), lambda s: (layer * nb + s, 0, 0))
    return pl.pallas_call(
        functools.partial(_sbs_kernel, tq=tq, past=past),
        grid=(nb,),
        in_specs=[rowspec, rowspec, rowspec, cspec, cspec, _const_spec((SB_BK + SBS_PAD, 2 * SB_BK))],
        out_specs=[rowspec, pl.BlockSpec((1, 1, LANES), lambda s: (s, 0, 0))],
        out_shape=[jax.ShapeDtypeStruct((nb * tq, SB_WIDTH), F32),
                   jax.ShapeDtypeStruct((nb, 1, LANES), F32)],
        scratch_shapes=[pltpu.VMEM((SB_HEADS * tq, SB_WIDTH), F32), pltpu.VMEM((1, SB_HEADS * tq), F32)],
        compiler_params=_params("arbitrary"),
        name="sb_sample",
    )(qb, kb, vb, kcache, vcache, _prefix_matrix())


def _hgrn_consts(c):
    levels = []
    m = 1
    while m < c:
        levels.append(m)
        m *= 2
    t = np.arange(c)[:, None]
    u = np.arange(c)[None, :]
    mats = [(u <= t).astype(np.float32)]
    masks = [np.eye(c, dtype=np.float32)]
    for m in levels:
        end_a = (t // (2 * m)) * 2 * m + m - 1
        in_b = (t % (2 * m)) >= m
        w = np.where(in_b, (u > end_a) & (u <= t), False).astype(np.float32)
        w = w - np.where(~in_b, (u > t) & (u <= end_a), False).astype(np.float32)
        mats.append(w)
        masks.append(((t // (2 * m)) == (u // (2 * m))).astype(np.float32))
    w = np.concatenate(mats, axis=0)
    return tuple(levels), jnp.asarray(np.concatenate([w, w], axis=1), dtype=BF16), jnp.asarray(np.stack(masks))


def _hgrn_kernel(hg_ref, lbl_ref, gain_ref, s0_ref, w_ref, mk_ref, o_ref, sfin_ref, st_ref,
                 *, c, levels, layer, nchunks):
    step = pl.program_id(1)
    hd = HG_HEAD_DIM

    @pl.when(step == 0)
    def _():
        for h in range(HG_HEADS):
            st_ref[h] = s0_ref[0, h].T

    lg = lbl_ref[...]
    ex = jnp.exp(lg - jnp.max(lg, axis=0, keepdims=True))
    sm = ex / jnp.sum(ex, axis=0, keepdims=True)
    csum = sm[0:1]
    for l in range(1, layer + 1):
        csum = csum + sm[l:l + 1]
    lb = csum - sm[0:1]

    zf = hg_ref[:, HG_WIDTH:2 * HG_WIDTH]
    ez = jnp.exp(-jnp.abs(zf))
    lsig = jnp.minimum(zf, 0.0) - jnp.log(1.0 + ez)
    bterm = jnp.log(1.0 - lb) + lsig
    la = jnp.log(jnp.maximum(lb, 1e-30))
    logf_mix = jnp.maximum(la, bterm) + jnp.log(1.0 + jnp.exp(-jnp.abs(la - bterm)))
    logf = jnp.where(lb > 0.0, logf_mix, bterm)
    khg = (1.0 - lb) * (jnp.where(zf >= 0.0, ez, 1.0) / (1.0 + ez))

    hi, lo = _split_bf16(logf)
    dd = jnp.dot(w_ref[...], jnp.concatenate([hi, lo], axis=0), preferred_element_type=F32)

    rowi = lax.broadcasted_iota(jnp.int32, (c, 1), 0)
    for h in range(HG_HEADS):
        sl = slice(h * hd, (h + 1) * hd)
        q = hg_ref[:, h * hd:(h + 1) * hd] * (hd ** -0.5)
        k = khg[:, sl]
        v = hg_ref[:, 2 * HG_WIDTH + h * hd:2 * HG_WIDTH + (h + 1) * hd]
        g = hg_ref[:, 3 * HG_WIDTH + h * hd:3 * HG_WIDTH + (h + 1) * hd]
        vb = v.astype(BF16)
        cum = dd[0:c, sl]
        st = st_ref[h]
        o = lax.dot_general((q * jnp.exp(cum)).astype(BF16), st.astype(BF16), _NT,
                            preferred_element_type=F32)
        sc = lax.dot_general(q.astype(BF16), k.astype(BF16), _NT, preferred_element_type=F32) * mk_ref[0]
        for li, m in enumerate(levels):
            e = jnp.exp(-jnp.abs(dd[(li + 1) * c:(li + 2) * c, sl]))
            second = (rowi & m) != 0
            qs = jnp.where(second, q * e, 0.0).astype(BF16)
            ks = jnp.where(second, 0.0, k * e).astype(BF16)
            sc = sc + lax.dot_general(qs, ks, _NT, preferred_element_type=F32) * mk_ref[li + 1]
        o = o + jnp.dot(sc.astype(BF16), vb, preferred_element_type=F32)
        last = cum[c - 1:c, :]
        kdec = (k * jnp.exp(last - cum)).astype(BF16)
        st_new = st * jnp.exp(last) + lax.dot_general(vb, kdec, _TN, preferred_element_type=F32)
        st_ref[h] = st_new

        eg = jnp.exp(-jnp.abs(g))
        gate = g * (jnp.where(g >= 0.0, 1.0, eg) / (1.0 + eg))
        o_ref[:, sl] = _rms(o, gain_ref[:, sl]) * gate

        @pl.when(step == nchunks - 1)
        def _():
            sfin_ref[0, h] = st_new.T


def _hgrn(hg, lb_logits, hg_gain, s0, s0_base, layer, nb, t):
    c = min(LANES, t)
    nchunks = t // c
    levels, wcat, masks = _hgrn_consts(c)
    nl = len(levels) + 1
    return pl.pallas_call(
        functools.partial(_hgrn_kernel, c=c, levels=levels, layer=layer, nchunks=nchunks),
        grid=(nb, nchunks),
        in_specs=[
            pl.BlockSpec((c, 4 * HG_WIDTH), lambda b, s: (b * nchunks + s, 0)),
            _const_spec((DEPTH, HG_WIDTH)),
            _const_spec((1, HG_WIDTH)),
            pl.BlockSpec((1, HG_HEADS, HG_HEAD_DIM, HG_HEAD_DIM), lambda b, s: (s0_base + b, 0, 0, 0)),
            _const_spec((nl * c, 2 * c)),
            _const_spec((nl, c, c)),
        ],
        out_specs=[
            pl.BlockSpec((c, HG_WIDTH), lambda b, s: (b * nchunks + s, 0)),
            pl.BlockSpec((1, HG_HEADS, HG_HEAD_DIM, HG_HEAD_DIM), lambda b, s: (b, 0, 0, 0)),
        ],
        out_shape=[
            jax.ShapeDtypeStruct((nb * t, HG_WIDTH), F32),
            jax.ShapeDtypeStruct((nb, HG_HEADS, HG_HEAD_DIM, HG_HEAD_DIM), F32),
        ],
        scratch_shapes=[pltpu.VMEM((HG_HEADS, HG_HEAD_DIM, HG_HEAD_DIM), F32)],
        compiler_params=_params("arbitrary", "arbitrary"),
        name="hgrn",
    )(hg, lb_logits, hg_gain.reshape(1, HG_WIDTH), s0, wcat, masks)


def _mix_kernel(x_ref, osb_ref, ohg_ref, sbg_ref, wout_ref, nmq_ref, wmq_ref, x1_ref, qm_ref):
    a = _rms(osb_ref[...], sbg_ref[...]).astype(BF16)
    b = ohg_ref[...].astype(BF16)
    x1 = (x_ref[...]
          + jnp.dot(a, wout_ref[0:SB_WIDTH, :], preferred_element_type=F32)
          + jnp.dot(b, wout_ref[SB_WIDTH:, :], preferred_element_type=F32))
    x1_ref[...] = x1
    h = _rms(x1, nmq_ref[...]).astype(BF16)
    qm_ref[...] = jnp.dot(h, wmq_ref[...], preferred_element_type=F32).astype(BF16)


def _mix(x2d, osb, ohg, sb_gain, w_out, norm_mem_q, w_mq, tm):
    n = x2d.shape[0]
    row = lambda w: pl.BlockSpec((tm, w), lambda i: (i, 0))
    return pl.pallas_call(
        _mix_kernel,
        grid=(n // tm,),
        in_specs=[row(D_MODEL), row(SB_WIDTH), row(HG_WIDTH), _const_spec((1, SB_WIDTH)),
                  _const_spec((D_MODEL, D_MODEL)), _const_spec((1, D_MODEL)),
                  _const_spec((D_MODEL, MEM_WIDTH))],
        out_specs=[row(D_MODEL), row(MEM_WIDTH)],
        out_shape=[jax.ShapeDtypeStruct((n, D_MODEL), F32), jax.ShapeDtypeStruct((n, MEM_WIDTH), BF16)],
        compiler_params=_params("arbitrary"),
        name="mix",
    )(x2d, osb, ohg, sb_gain.reshape(1, SB_WIDTH), w_out, norm_mem_q.reshape(1, D_MODEL), w_mq)


def _memattn_kernel(q_ref, mk_ref, mv_ref, o_ref):
    hd = MEM_HEAD_DIM
    for h in range(MEM_HEADS):
        sl = slice(h * hd, (h + 1) * hd)
        s = lax.dot_general(q_ref[:, sl], mk_ref[0, :, sl].astype(BF16), _NT,
                            preferred_element_type=F32) * (hd ** -0.5)
        p = jnp.exp(s - jnp.max(s, axis=-1, keepdims=True))
        den = jnp.sum(p, axis=-1, keepdims=True)
        o = jnp.dot(p.astype(BF16), mv_ref[0, :, sl].astype(BF16), preferred_element_type=F32)
        o_ref[:, sl] = (o / den).astype(BF16)


def _memattn(qm, mk3, mv3, base, nb, t, tm):
    nt = t // tm
    mspec = pl.BlockSpec((1, N_MEM, MEM_WIDTH), lambda b, i: (base + b, 0, 0))
    return pl.pallas_call(
        _memattn_kernel,
        grid=(nb, nt),
        in_specs=[pl.BlockSpec((tm, MEM_WIDTH), lambda b, i: (b * nt + i, 0)), mspec, mspec],
        out_specs=pl.BlockSpec((tm, MEM_WIDTH), lambda b, i: (b * nt + i, 0)),
        out_shape=jax.ShapeDtypeStruct((nb * t, MEM_WIDTH), BF16),
        compiler_params=_params("arbitrary", "arbitrary"),
        name="memattn",
    )(qm, mk3, mv3)


FF_CHUNK = 1024


def _ffn_kernel(x1_ref, om_ref, wmo_ref, nf_ref, w1_ref, w2_ref, nfin_ref, o_ref, *, final):
    x2 = x1_ref[...] + jnp.dot(om_ref[...], wmo_ref[...], preferred_element_type=F32)
    h = _rms(x2, nf_ref[...]).astype(BF16)
    x3 = x2
    for c in range(D_FF // FF_CHUNK):
        cs = slice(c * FF_CHUNK, (c + 1) * FF_CHUNK)
        r = jnp.maximum(jnp.dot(h, w1_ref[:, cs], preferred_element_type=F32), 0.0)
        x3 = x3 + jnp.dot((r * r).astype(BF16), w2_ref[cs, :], preferred_element_type=F32)
    o_ref[...] = _rms(x3, nfin_ref[...]) if final else x3


def _ffn(x1, om, w_mo, norm_ffn, w1, w2, norm_final, final, tm):
    n = x1.shape[0]
    row = lambda w: pl.BlockSpec((tm, w), lambda i: (i, 0))
    return pl.pallas_call(
        functools.partial(_ffn_kernel, final=final),
        grid=(n // tm,),
        in_specs=[row(D_MODEL), row(MEM_WIDTH), _const_spec((MEM_WIDTH, D_MODEL)),
                  _const_spec((1, D_MODEL)), _const_spec((D_MODEL, D_FF)), _const_spec((D_FF, D_MODEL)),
                  _const_spec((1, D_MODEL))],
        out_specs=row(D_MODEL),
        out_shape=jax.ShapeDtypeStruct((n, D_MODEL), F32),
        compiler_params=_params("arbitrary"),
        name="ffn",
    )(x1, om, w_mo, norm_ffn.reshape(1, D_MODEL), w1, w2, norm_final.reshape(1, D_MODEL))


TM = 256
SB_BQ = 256
SBS_NEAR = 256


def kernel(x_prompt, x_sample, mem_prompt, cache_sb_k, cache_sb_v, state_hgrn, cache_mem_k, cache_mem_v,
           lb_logits, norm_mix, w_in, sb_gain, hg_gain, w_out, norm_mem_q, norm_mem_kv, w_mq, w_mk, w_mv,
           w_mo, norm_ffn, w_ffn1, w_ffn2, norm_final):
    nbp, tp, _ = x_prompt.shape
    nbs, ts, _ = x_sample.shape
    past = cache_sb_k.shape[2]
    w_in_b, w_out_b, w_mq_b, w_mk_b, w_mv_b, w_mo_b, w1_b, w2_b = (
        w.astype(BF16) for w in (w_in, w_out, w_mq, w_mk, w_mv, w_mo, w_ffn1, w_ffn2))
    lbl = lb_logits.astype(F32)

    mk_p, mv_p = _memkv(mem_prompt.reshape(nbp * N_MEM, D_MODEL), norm_mem_kv, w_mk_b, w_mv_b)
    mk_p3 = mk_p.reshape(DEPTH * nbp, N_MEM, MEM_WIDTH)
    mv_p3 = mv_p.reshape(DEPTH * nbp, N_MEM, MEM_WIDTH)
    mk_s3 = cache_mem_k.reshape(DEPTH * nbs, N_MEM, MEM_WIDTH)
    mv_s3 = cache_mem_v.reshape(DEPTH * nbs, N_MEM, MEM_WIDTH)
    s0_s = state_hgrn.reshape(DEPTH * nbs, HG_HEADS, HG_HEAD_DIM, HG_HEAD_DIM)
    s0_p = jnp.zeros((nbp, HG_HEADS, HG_HEAD_DIM, HG_HEAD_DIM), F32)
    near = min(SBS_NEAR, past)
    kc_near = cache_sb_k[:, :, past - near:].reshape(DEPTH * nbs, near, SB_WIDTH)
    vc_near = cache_sb_v[:, :, past - near:].reshape(DEPTH * nbs, near, SB_WIDTH)

    def sb_sample(l, qb, kb, vb, nb, t):
        o_near, alive = _sb_sample(qb, kb, vb, kc_near, vc_near, l, nb, t)
        if near == past:
            return o_near

        def full(ck, cv):
            return _sb_sample(qb, kb, vb, ck.reshape(DEPTH * nbs, past, SB_WIDTH),
                              cv.reshape(DEPTH * nbs, past, SB_WIDTH), l, nb, t)[0]

        return lax.cond(jnp.max(alive) > 0.0, full, lambda ck, cv: o_near, cache_sb_k, cache_sb_v)

    def layer(l, x2d, nb, t, prompt, kv_all):
        qb, k_all, v_all, kb, vb, hg = _proj(x2d, norm_mix[l], w_in_b[l], TM, l, kv_all)
        if prompt:
            osb = _sb_prompt(qb, kb, vb, nb, t, SB_BQ)
            ohg, sfin = _hgrn(hg, lbl, hg_gain[l], s0_p, 0, l, nb, t)
            mk3, mv3, base = mk_p3, mv_p3, l * nb
        else:
            osb = sb_sample(l, qb, kb, vb, nb, t)
            ohg, sfin = _hgrn(hg, lbl, hg_gain[l], s0_s, l * nb, l, nb, t)
            mk3, mv3, base = mk_s3, mv_s3, l * nb
        x1, qm = _mix(x2d, osb, ohg, sb_gain[l], w_out_b[l], norm_mem_q[l], w_mq_b[l], TM)
        om = _memattn(qm, mk3, mv3, base, nb, t, min(TM, t))
        xo = _ffn(x1, om, w_mo_b[l], norm_ffn[l], w1_b[l], w2_b[l], norm_final, l == DEPTH - 1, TM)
        return xo, (k_all, v_all), sfin

    xp = x_prompt.reshape(nbp * tp, D_MODEL)
    xs = x_sample.reshape(nbs * ts, D_MODEL)
    kv_p = kv_s = None
    sp_l, ss_l = [], []
    for l in range(DEPTH):
        xp, kv_p, sp = layer(l, xp, nbp, tp, True, kv_p)
        xs, kv_s, sn = layer(l, xs, nbs, ts, False, kv_s)
        sp_l.append(sp)
        ss_l.append(sn)

    sbp = (DEPTH, nbp, tp, SB_HEADS, SB_HEAD_DIM)
    sbs = (DEPTH, nbs, ts, SB_HEADS, SB_HEAD_DIM)
    memp = (DEPTH, nbp, N_MEM, MEM_HEADS, MEM_HEAD_DIM)
    return (xp.reshape(nbp, tp, D_MODEL), xs.reshape(nbs, ts, D_MODEL),
            kv_p[0].reshape(sbp), kv_p[1].reshape(sbp), jnp.stack(sp_l),
            mk_p.reshape(memp), mv_p.reshape(memp),
            kv_s[0].reshape(sbs), kv_s[1].reshape(sbs), jnp.stack(ss_l))
```

```python
import functools

import jax
import jax.numpy as jnp
import numpy as np
from jax import lax
from jax.experimental import pallas as pl
from jax.experimental.pallas import tpu as pltpu

D_MODEL = 1024
DEPTH = 2
SB_HEADS = 8
SB_HEAD_DIM = 64
SB_WIDTH = SB_HEADS * SB_HEAD_DIM
HG_HEADS = 4
HG_HEAD_DIM = 128
HG_WIDTH = HG_HEADS * HG_HEAD_DIM
IN_COLS = 3 * SB_WIDTH + 4 * HG_WIDTH
N_MEM = 256
MEM_HEADS = 4
MEM_HEAD_DIM = 128
MEM_WIDTH = MEM_HEADS * MEM_HEAD_DIM
D_FF = 4 * D_MODEL
EPS = 1e-6

LANES = 128
VMEM_LIMIT = 56 * 1024 * 1024
F32 = jnp.float32
BF16 = jnp.bfloat16

_NT = (((1,), (1,)), ((), ()))
_TN = (((0,), (0,)), ((), ()))


def _params(*sem):
    return pltpu.CompilerParams(dimension_semantics=sem, vmem_limit_bytes=VMEM_LIMIT)


def _const_spec(shape):
    nd = len(shape)
    return pl.BlockSpec(shape, lambda *_: (0,) * nd, pipeline_mode=pl.Buffered(1))


def _log2(n):
    assert n > 0 and n & (n - 1) == 0, n
    return n.bit_length() - 1


def _rms(x, gain):
    ms = jnp.mean(x * x, axis=-1, keepdims=True)
    return x * lax.rsqrt(ms + EPS) * gain


def _log_sigmoid_parts(z):
    l = jnp.log(1.0 + jnp.exp(-jnp.abs(z)))
    ls = jnp.minimum(z, 0.0) - l
    return ls, ls - z


LOG2E = 1.4426950408889634


def _exp2_neg_abs(x):
    return jnp.exp2(-jnp.abs(x))


def _split_bf16(x):
    hi = x.astype(BF16)
    lo = (x - hi.astype(F32)).astype(BF16)
    return hi, lo


def _memkv_kernel(mem_ref, g_ref, wk_ref, wv_ref, mk_ref, mv_ref):
    h = _rms(mem_ref[...], g_ref[0]).astype(BF16)
    mk_ref[0] = jnp.dot(h, wk_ref[0], preferred_element_type=F32)
    mv_ref[0] = jnp.dot(h, wv_ref[0], preferred_element_type=F32)


def _memkv(mem2d, gains, wk, wv):
    n = mem2d.shape[0]
    out = jax.ShapeDtypeStruct((DEPTH, n, MEM_WIDTH), F32)
    return pl.pallas_call(
        _memkv_kernel,
        grid=(DEPTH,),
        in_specs=[
            pl.BlockSpec((n, D_MODEL), lambda l: (0, 0)),
            pl.BlockSpec((1, 1, D_MODEL), lambda l: (l, 0, 0)),
            pl.BlockSpec((1, D_MODEL, MEM_WIDTH), lambda l: (l, 0, 0)),
            pl.BlockSpec((1, D_MODEL, MEM_WIDTH), lambda l: (l, 0, 0)),
        ],
        out_specs=[pl.BlockSpec((1, n, MEM_WIDTH), lambda l: (l, 0, 0))] * 2,
        out_shape=[out, out],
        compiler_params=_params("arbitrary"),
        name="memkv",
    )(mem2d, gains.reshape(DEPTH, 1, D_MODEL), wk, wv)


def _proj_kernel(x_ref, g_ref, w_ref, *refs, slab):
    qsb_ref, ksb_ref, vsb_ref, kbf_ref, vbf_ref, hg_ref = refs[-6:]
    h = _rms(x_ref[...], g_ref[...]).astype(BF16)

    def col(c):
        return jnp.dot(h, w_ref[:, c * SB_WIDTH:(c + 1) * SB_WIDTH], preferred_element_type=F32)

    def put(ref, val):
        for s in range(ref.shape[0]):
            ref[s] = val if s == slab else jnp.zeros_like(val)

    qsb_ref[...] = (col(0) * (SB_HEAD_DIM ** -0.5)).astype(BF16)
    k = col(1)
    put(ksb_ref, k)
    kbf_ref[...] = k.astype(BF16)
    v = col(2)
    put(vsb_ref, v)
    vbf_ref[...] = v.astype(BF16)
    for c in range(4):
        hg_ref[:, c * HG_WIDTH:(c + 1) * HG_WIDTH] = col(3 + c)


def _proj(x2d, gain, w_in, tm, layer, kv_all):
    n = x2d.shape[0]
    row = lambda w: pl.BlockSpec((tm, w), lambda i: (i, 0))
    in_specs = [row(D_MODEL), _const_spec((1, D_MODEL)), _const_spec((D_MODEL, IN_COLS))]
    args = (x2d, gain.reshape(1, D_MODEL), w_in)
    if kv_all is None:
        kv_spec, slab, aliases = pl.BlockSpec((DEPTH, tm, SB_WIDTH), lambda i: (0, i, 0)), layer, {}
    else:
        kv_spec, slab, aliases = pl.BlockSpec((1, tm, SB_WIDTH), lambda i: (layer, i, 0)), 0, {3: 1, 4: 2}
        in_specs += [pl.BlockSpec(memory_space=pl.ANY)] * 2
        args += tuple(kv_all)
    return pl.pallas_call(
        functools.partial(_proj_kernel, slab=slab),
        grid=(n // tm,),
        in_specs=in_specs,
        out_specs=[row(SB_WIDTH), kv_spec, kv_spec, row(SB_WIDTH), row(SB_WIDTH), row(4 * HG_WIDTH)],
        out_shape=[
            jax.ShapeDtypeStruct((n, SB_WIDTH), BF16),
            jax.ShapeDtypeStruct((DEPTH, n, SB_WIDTH), F32),
            jax.ShapeDtypeStruct((DEPTH, n, SB_WIDTH), F32),
            jax.ShapeDtypeStruct((n, SB_WIDTH), BF16),
            jax.ShapeDtypeStruct((n, SB_WIDTH), BF16),
            jax.ShapeDtypeStruct((n, 4 * HG_WIDTH), F32),
        ],
        input_output_aliases=aliases,
        compiler_params=_params("arbitrary"),
        name="proj",
    )(*args)


SB_BK = LANES
SB_DEAD = -104.0
SB_NPAIR = 2


def _suffix_matrix():
    j = np.arange(SB_BK)[:, None]
    s = np.arange(SB_BK)[None, :]
    u = np.concatenate([(j > s).astype(np.float32), np.ones((SB_BK, SB_BK), np.float32)], axis=1)
    return jnp.asarray(np.concatenate([u, u], axis=0), dtype=BF16)


def _sbp_kernel(q_ref, k_ref, v_ref, u_ref, o_ref, acc_ref, run_ref, *, bq, npair):
    bk = SB_BK
    qi = pl.program_id(2)
    lane = lax.broadcasted_iota(jnp.int32, (1, LANES), 1)
    m_a = (lane < SB_HEAD_DIM).astype(BF16)
    m_b = (lane >= SB_HEAD_DIM).astype(BF16)
    acc_ref[...] = jnp.zeros_like(acc_ref)
    run_ref[...] = jnp.zeros_like(run_ref)
    u = u_ref[...]
    nd = bq // bk
    assert nd % 2 == 0, "earlier key blocks are taken two per trip"

    def block(j, rel):
        r0 = pl.multiple_of(j * bk, bk)
        if rel is not None:
            mask = (lax.broadcasted_iota(jnp.int32, (bq, bk), 1) + rel
                    < lax.broadcasted_iota(jnp.int32, (bq, bk), 0))
        for p in range(npair):
            ps = slice(p * LANES, (p + 1) * LANES)
            ks = k_ref[pl.ds(r0, bk), ps]
            vs = v_ref[pl.ds(r0, bk), ps]
            kcat = jnp.concatenate([ks * m_a, ks * m_b], axis=0)
            vcat = jnp.concatenate([vs * m_a, vs * m_b], axis=0)
            z = lax.dot_general(q_ref[:, ps], kcat, _NT, preferred_element_type=F32)
            a_parts = []
            for hh in range(2):
                zh = z[:, hh * bk:(hh + 1) * bk]
                ls, lk = _log_sigmoid_parts(zh)
                if rel is not None:
                    lk = jnp.where(mask, lk, 0.0)
                hi, lo = _split_bf16(lk)
                cs = jnp.dot(jnp.concatenate([hi, lo], axis=1), u, preferred_element_type=F32)
                run = run_ref[2 * p + hh]
                a = jnp.exp(ls + cs[:, :bk] + run)
                if rel is not None:
                    a = jnp.where(mask, a, 0.0)
                run_ref[2 * p + hh] = run + cs[:, bk:]
                a_parts.append(a.astype(BF16))
            acc_ref[:, ps] += jnp.dot(jnp.concatenate(a_parts, axis=1), vcat, preferred_element_type=F32)

    for d in range(nd - 1, -1, -1):
        block(qi * nd + d, d * bk)

    def live():
        return jnp.max(run_ref[...]) > SB_DEAD

    def cond(state):
        j, alive = state
        return jnp.logical_and(j >= 0, alive)

    def body(state):
        j, _ = state
        block(j, None)
        block(j - 1, None)
        return j - 2, live()

    lax.while_loop(cond, body, (qi * nd - 1, live()))
    o_ref[...] = acc_ref[...]


def _sb_prompt(qb, kb, vb, nb, t, bq):
    nq = t // bq
    w = SB_NPAIR * LANES
    return pl.pallas_call(
        functools.partial(_sbp_kernel, bq=bq, npair=SB_NPAIR),
        grid=(nb, SB_WIDTH // w, nq),
        in_specs=[
            pl.BlockSpec((bq, w), lambda b, h, i: (b * nq + i, h)),
            pl.BlockSpec((t, w), lambda b, h, i: (b, h)),
            pl.BlockSpec((t, w), lambda b, h, i: (b, h)),
            _const_spec((2 * SB_BK, 2 * SB_BK)),
        ],
        out_specs=pl.BlockSpec((bq, w), lambda b, h, i: (b * nq + i, h)),
        out_shape=jax.ShapeDtypeStruct((nb * t, SB_WIDTH), F32),
        scratch_shapes=[pltpu.VMEM((bq, w), F32), pltpu.VMEM((2 * SB_NPAIR, bq, LANES), F32)],
        compiler_params=_params("arbitrary", "arbitrary", "arbitrary"),
        name="sb_prompt",
    )(qb, kb, vb, _suffix_matrix())


SBS_PAD = 16


def _prefix_matrix():
    s = np.arange(SB_BK)[:, None]
    j = np.arange(SB_BK)[None, :]
    u = np.concatenate([(j > s).astype(np.float32), np.ones((SBS_PAD, SB_BK), np.float32)], axis=0)
    return jnp.asarray(np.concatenate([u, u], axis=1), dtype=BF16)


def _sbs_kernel(q_ref, kn_ref, vn_ref, kc_ref, vc_ref, u_ref, o_ref, alive_ref, acc_ref, run_ref,
                *, tq, past):
    bk = SB_BK
    nl = SB_HEADS * tq
    q = q_ref[...]
    row = lax.broadcasted_iota(jnp.int32, (nl, SB_WIDTH), 0)
    colw = lax.broadcasted_iota(jnp.int32, (nl, SB_WIDTH), 1)
    same_head = (row >> _log2(tq)) == (colw >> _log2(SB_HEAD_DIM))
    qx = jnp.where(same_head, jnp.concatenate([q] * SB_HEADS, axis=0), jnp.zeros((), BF16))
    u = u_ref[...]
    acc_ref[...] = jnp.zeros_like(acc_ref)
    run_ref[...] = jnp.zeros_like(run_ref)

    def block(kb, vb, mask):
        z = lax.dot_general(kb, qx, _NT, preferred_element_type=F32)
        ls, lk = _log_sigmoid_parts(z)
        if mask is not None:
            lk = jnp.where(mask, lk, 0.0)
        hi, lo = _split_bf16(lk)
        cs = jnp.dot(u, jnp.concatenate([hi, lo], axis=0), preferred_element_type=F32)
        run = run_ref[...]
        a = jnp.exp(ls + cs[:bk] + run)
        if mask is not None:
            a = jnp.where(mask, a, 0.0)
        run_ref[...] = run + cs[bk:bk + 1]
        acc_ref[...] += lax.dot_general(a.astype(BF16), vb, _TN, preferred_element_type=F32)

    pad = jnp.zeros((bk - tq, SB_WIDTH), BF16)
    srow = lax.broadcasted_iota(jnp.int32, (bk, nl), 0)
    tlane = lax.broadcasted_iota(jnp.int32, (bk, nl), 1) & (tq - 1)
    block(jnp.concatenate([kn_ref[...], pad], axis=0), jnp.concatenate([vn_ref[...], pad], axis=0),
          srow < tlane)

    def live():
        return jnp.max(run_ref[...]) > SB_DEAD

    def cond(state):
        j, alive = state
        return jnp.logical_and(j >= 0, alive)

    def body(state):
        j, _ = state
        r0 = pl.multiple_of(j * bk, bk)
        block(kc_ref[0, pl.ds(r0, bk), :].astype(BF16), vc_ref[0, pl.ds(r0, bk), :].astype(BF16), None)
        return j - 1, live()

    _, alive = lax.while_loop(cond, body, (past // bk - 1, live()))
    alive_ref[...] = jnp.where(alive, jnp.ones(alive_ref.shape, F32), jnp.zeros(alive_ref.shape, F32))
    accm = jnp.where(same_head, acc_ref[...], 0.0)
    out = accm[0:tq]
    for h in range(1, SB_HEADS):
        out = out + accm[h * tq:(h + 1) * tq]
    o_ref[...] = out


def _sb_sample(qb, kb, vb, kcache, vcache, layer, nb, tq):
    past = kcache.shape[1]
    rowspec = pl.BlockSpec((tq, SB_WIDTH), lambda s: (s, 0))
    cspec = pl.BlockSpec((1, past, SB_WIDTH), lambda s: (layer * nb + s, 0, 0))
    return pl.pallas_call(
        functools.partial(_sbs_kernel, tq=tq, past=past),
        grid=(nb,),
        in_specs=[rowspec, rowspec, rowspec, cspec, cspec, _const_spec((SB_BK + SBS_PAD, 2 * SB_BK))],
        out_specs=[rowspec, pl.BlockSpec((1, 1, LANES), lambda s: (s, 0, 0))],
        out_shape=[jax.ShapeDtypeStruct((nb * tq, SB_WIDTH), F32),
                   jax.ShapeDtypeStruct((nb, 1, LANES), F32)],
        scratch_shapes=[pltpu.VMEM((SB_HEADS * tq, SB_WIDTH), F32), pltpu.VMEM((1, SB_HEADS * tq), F32)],
        compiler_params=_params("arbitrary"),
        name="sb_sample",
    )(qb, kb, vb, kcache, vcache, _prefix_matrix())


def _hgrn_consts(c):
    levels = []
    m = 1
    while m < c:
        levels.append(m)
        m *= 2
    t = np.arange(c)[:, None]
    u = np.arange(c)[None, :]
    mats = [(u <= t).astype(np.float32)]
    masks = [np.eye(c, dtype=np.float32)]
    for m in levels:
        end_a = (t // (2 * m)) * 2 * m + m - 1
        in_b = (t % (2 * m)) >= m
        w = np.where(in_b, (u > end_a) & (u <= t), False).astype(np.float32)
        w = w - np.where(~in_b, (u > t) & (u <= end_a), False).astype(np.float32)
        mats.append(w)
        masks.append((((t // (2 * m)) == (u // (2 * m))) & in_b & ((u % (2 * m)) < m)).astype(np.float32))
    w = np.concatenate(mats, axis=0)
    return tuple(levels), jnp.asarray(np.concatenate([w, w], axis=1), dtype=BF16), jnp.asarray(np.stack(masks))


def _hgrn_kernel(hg_ref, lbl_ref, gain_ref, s0_ref, w_ref, mk_ref, *refs, c, levels, layer, slab, nchunks):
    o_ref, sfin_ref, st_ref = refs[-3:]
    step = pl.program_id(1)
    hd = HG_HEAD_DIM

    @pl.when(step == 0)
    def _():
        for h in range(HG_HEADS):
            st_ref[h] = s0_ref[0, h].T

    lg = lbl_ref[...]
    ex = jnp.exp(lg - jnp.max(lg, axis=0, keepdims=True))
    sm = ex / jnp.sum(ex, axis=0, keepdims=True)
    csum = sm[0:1]
    for l in range(1, layer + 1):
        csum = csum + sm[l:l + 1]
    lb = csum - sm[0:1]

    zf = hg_ref[:, HG_WIDTH:2 * HG_WIDTH]
    ez = jnp.exp(-jnp.abs(zf))
    lsig = jnp.minimum(zf, 0.0) - jnp.log(1.0 + ez)
    bterm = jnp.log(1.0 - lb) + lsig
    la = jnp.log(jnp.maximum(lb, 1e-30))
    logf_mix = jnp.maximum(la, bterm) + jnp.log(1.0 + jnp.exp(-jnp.abs(la - bterm)))
    logf = jnp.where(lb > 0.0, logf_mix, bterm)
    khg = (1.0 - lb) * (jnp.where(zf >= 0.0, ez, 1.0) / (1.0 + ez))

    hi, lo = _split_bf16(logf * LOG2E)
    dd = jnp.dot(w_ref[...], jnp.concatenate([hi, lo], axis=0), preferred_element_type=F32)

    for h in range(HG_HEADS):
        sl = slice(h * hd, (h + 1) * hd)
        q = hg_ref[:, h * hd:(h + 1) * hd] * (hd ** -0.5)
        k = khg[:, sl]
        v = hg_ref[:, 2 * HG_WIDTH + h * hd:2 * HG_WIDTH + (h + 1) * hd]
        g = hg_ref[:, 3 * HG_WIDTH + h * hd:3 * HG_WIDTH + (h + 1) * hd]
        vb = v.astype(BF16)
        cum = dd[0:c, sl]
        st = st_ref[h]
        o = lax.dot_general((q * jnp.exp2(cum)).astype(BF16), st.astype(BF16), _NT,
                            preferred_element_type=F32)
        sc = lax.dot_general(q.astype(BF16), k.astype(BF16), _NT, preferred_element_type=F32) * mk_ref[0]
        for li in range(len(levels)):
            e = _exp2_neg_abs(dd[(li + 1) * c:(li + 2) * c, sl])
            sc = sc + lax.dot_general((q * e).astype(BF16), (k * e).astype(BF16), _NT,
                                      preferred_element_type=F32) * mk_ref[li + 1]
        o = o + jnp.dot(sc.astype(BF16), vb, preferred_element_type=F32)
        last = cum[c - 1:c, :]
        kdec = (k * jnp.exp2(last - cum)).astype(BF16)
        st_ref[h] = st * jnp.exp2(last) + lax.dot_general(vb, kdec, _TN, preferred_element_type=F32)

        eg = jnp.exp(-jnp.abs(g))
        gate = g * (jnp.where(g >= 0.0, 1.0, eg) / (1.0 + eg))
        o_ref[:, sl] = _rms(o, gain_ref[:, sl]) * gate

    @pl.when(step == nchunks - 1)
    def _():
        for s in range(sfin_ref.shape[0]):
            for h in range(HG_HEADS):
                sfin_ref[s, 0, h] = st_ref[h].T if s == slab else jnp.zeros((hd, hd), F32)


def _hgrn(hg, lb_logits, hg_gain, s0, s0_base, layer, nb, t, s_all):
    c = min(LANES, t)
    nchunks = t // c
    levels, wcat, masks = _hgrn_consts(c)
    nl = len(levels) + 1
    sdims = (HG_HEADS, HG_HEAD_DIM, HG_HEAD_DIM)
    in_specs = [
        pl.BlockSpec((c, 4 * HG_WIDTH), lambda b, s: (b * nchunks + s, 0)),
        _const_spec((DEPTH, HG_WIDTH)),
        _const_spec((1, HG_WIDTH)),
        pl.BlockSpec((1,) + sdims, lambda b, s: (s0_base + b, 0, 0, 0)),
        _const_spec((nl * c, 2 * c)),
        _const_spec((nl, c, c)),
    ]
    args = (hg, lb_logits, hg_gain.reshape(1, HG_WIDTH), s0, wcat, masks)
    if s_all is None:
        s_spec, slab, aliases = pl.BlockSpec((DEPTH, 1) + sdims, lambda b, s: (0, b, 0, 0, 0)), layer, {}
    else:
        s_spec, slab, aliases = pl.BlockSpec((1, 1) + sdims, lambda b, s: (layer, b, 0, 0, 0)), 0, {6: 1}
        in_specs.append(pl.BlockSpec(memory_space=pl.ANY))
        args += (s_all,)
    return pl.pallas_call(
        functools.partial(_hgrn_kernel, c=c, levels=levels, layer=layer, slab=slab, nchunks=nchunks),
        grid=(nb, nchunks),
        in_specs=in_specs,
        out_specs=[pl.BlockSpec((c, HG_WIDTH), lambda b, s: (b * nchunks + s, 0)), s_spec],
        out_shape=[
            jax.ShapeDtypeStruct((nb * t, HG_WIDTH), F32),
            jax.ShapeDtypeStruct((DEPTH, nb) + sdims, F32),
        ],
        input_output_aliases=aliases,
        scratch_shapes=[pltpu.VMEM(sdims, F32)],
        compiler_params=_params("arbitrary", "arbitrary"),
        name="hgrn",
    )(*args)


def _mix_kernel(x_ref, osb_ref, ohg_ref, sbg_ref, wout_ref, nmq_ref, wmq_ref, x1_ref, qm_ref):
    a = _rms(osb_ref[...], sbg_ref[...]).astype(BF16)
    b = ohg_ref[...].astype(BF16)
    x1 = (x_ref[...]
          + jnp.dot(a, wout_ref[0:SB_WIDTH, :], preferred_element_type=F32)
          + jnp.dot(b, wout_ref[SB_WIDTH:, :], preferred_element_type=F32))
    x1_ref[...] = x1
    h = _rms(x1, nmq_ref[...]).astype(BF16)
    qm_ref[...] = jnp.dot(h, wmq_ref[...], preferred_element_type=F32).astype(BF16)


def _mix(x2d, osb, ohg, sb_gain, w_out, norm_mem_q, w_mq, tm):
    n = x2d.shape[0]
    row = lambda w: pl.BlockSpec((tm, w), lambda i: (i, 0))
    return pl.pallas_call(
        _mix_kernel,
        grid=(n // tm,),
        in_specs=[row(D_MODEL), row(SB_WIDTH), row(HG_WIDTH), _const_spec((1, SB_WIDTH)),
                  _const_spec((D_MODEL, D_MODEL)), _const_spec((1, D_MODEL)),
                  _const_spec((D_MODEL, MEM_WIDTH))],
        out_specs=[row(D_MODEL), row(MEM_WIDTH)],
        out_shape=[jax.ShapeDtypeStruct((n, D_MODEL), F32), jax.ShapeDtypeStruct((n, MEM_WIDTH), BF16)],
        compiler_params=_params("arbitrary"),
        name="mix",
    )(x2d, osb, ohg, sb_gain.reshape(1, SB_WIDTH), w_out, norm_mem_q.reshape(1, D_MODEL), w_mq)


def _memattn_kernel(q_ref, mk_ref, mv_ref, o_ref):
    hd = MEM_HEAD_DIM
    for h in range(MEM_HEADS):
        sl = slice(h * hd, (h + 1) * hd)
        s = lax.dot_general(q_ref[:, sl], mk_ref[0, :, sl].astype(BF16), _NT,
                            preferred_element_type=F32) * (hd ** -0.5)
        p = jnp.exp(s - jnp.max(s, axis=-1, keepdims=True))
        den = jnp.sum(p, axis=-1, keepdims=True)
        o = jnp.dot(p.astype(BF16), mv_ref[0, :, sl].astype(BF16), preferred_element_type=F32)
        o_ref[:, sl] = (o / den).astype(BF16)


def _memattn(qm, mk3, mv3, base, nb, t, tm):
    nt = t // tm
    mspec = pl.BlockSpec((1, N_MEM, MEM_WIDTH), lambda b, i: (base + b, 0, 0))
    return pl.pallas_call(
        _memattn_kernel,
        grid=(nb, nt),
        in_specs=[pl.BlockSpec((tm, MEM_WIDTH), lambda b, i: (b * nt + i, 0)), mspec, mspec],
        out_specs=pl.BlockSpec((tm, MEM_WIDTH), lambda b, i: (b * nt + i, 0)),
        out_shape=jax.ShapeDtypeStruct((nb * t, MEM_WIDTH), BF16),
        compiler_params=_params("arbitrary", "arbitrary"),
        name="memattn",
    )(qm, mk3, mv3)


FF_CHUNK = 1024


def _ffn_kernel(x1_ref, om_ref, wmo_ref, nf_ref, w1_ref, w2_ref, nfin_ref, o_ref, *, final):
    x2 = x1_ref[...] + jnp.dot(om_ref[...], wmo_ref[...], preferred_element_type=F32)
    h = _rms(x2, nf_ref[...]).astype(BF16)
    x3 = x2
    for c in range(D_FF // FF_CHUNK):
        cs = slice(c * FF_CHUNK, (c + 1) * FF_CHUNK)
        r = jnp.maximum(jnp.dot(h, w1_ref[:, cs], preferred_element_type=F32), 0.0)
        x3 = x3 + jnp.dot((r * r).astype(BF16), w2_ref[cs, :], preferred_element_type=F32)
    o_ref[...] = _rms(x3, nfin_ref[...]) if final else x3


def _ffn(x1, om, w_mo, norm_ffn, w1, w2, norm_final, final, tm):
    n = x1.shape[0]
    row = lambda w: pl.BlockSpec((tm, w), lambda i: (i, 0))
    return pl.pallas_call(
        functools.partial(_ffn_kernel, final=final),
        grid=(n // tm,),
        in_specs=[row(D_MODEL), row(MEM_WIDTH), _const_spec((MEM_WIDTH, D_MODEL)),
                  _const_spec((1, D_MODEL)), _const_spec((D_MODEL, D_FF)), _const_spec((D_FF, D_MODEL)),
                  _const_spec((1, D_MODEL))],
        out_specs=row(D_MODEL),
        out_shape=jax.ShapeDtypeStruct((n, D_MODEL), F32),
        compiler_params=_params("arbitrary"),
        name="ffn",
    )(x1, om, w_mo, norm_ffn.reshape(1, D_MODEL), w1, w2, norm_final.reshape(1, D_MODEL))


TM = 512
SB_BQ = 256
SBS_NEAR = 256


def kernel(x_prompt, x_sample, mem_prompt, cache_sb_k, cache_sb_v, state_hgrn, cache_mem_k, cache_mem_v,
           lb_logits, norm_mix, w_in, sb_gain, hg_gain, w_out, norm_mem_q, norm_mem_kv, w_mq, w_mk, w_mv,
           w_mo, norm_ffn, w_ffn1, w_ffn2, norm_final):
    nbp, tp, _ = x_prompt.shape
    nbs, ts, _ = x_sample.shape
    past = cache_sb_k.shape[2]
    w_in_b, w_out_b, w_mq_b, w_mk_b, w_mv_b, w_mo_b, w1_b, w2_b = (
        w.astype(BF16) for w in (w_in, w_out, w_mq, w_mk, w_mv, w_mo, w_ffn1, w_ffn2))
    lbl = lb_logits.astype(F32)

    mk_p, mv_p = _memkv(mem_prompt.reshape(nbp * N_MEM, D_MODEL), norm_mem_kv, w_mk_b, w_mv_b)
    mk_p3 = mk_p.reshape(DEPTH * nbp, N_MEM, MEM_WIDTH)
    mv_p3 = mv_p.reshape(DEPTH * nbp, N_MEM, MEM_WIDTH)
    mk_s3 = cache_mem_k.reshape(DEPTH * nbs, N_MEM, MEM_WIDTH)
    mv_s3 = cache_mem_v.reshape(DEPTH * nbs, N_MEM, MEM_WIDTH)
    s0_s = state_hgrn.reshape(DEPTH * nbs, HG_HEADS, HG_HEAD_DIM, HG_HEAD_DIM)
    s0_p = jnp.zeros((nbp, HG_HEADS, HG_HEAD_DIM, HG_HEAD_DIM), F32)
    near = min(SBS_NEAR, past)
    kc_near = cache_sb_k[:, :, past - near:].reshape(DEPTH * nbs, near, SB_WIDTH)
    vc_near = cache_sb_v[:, :, past - near:].reshape(DEPTH * nbs, near, SB_WIDTH)

    def sb_sample(l, qb, kb, vb, nb, t):
        o_near, alive = _sb_sample(qb, kb, vb, kc_near, vc_near, l, nb, t)
        if near == past:
            return o_near

        def full(ck, cv):
            return _sb_sample(qb, kb, vb, ck.reshape(DEPTH * nbs, past, SB_WIDTH),
                              cv.reshape(DEPTH * nbs, past, SB_WIDTH), l, nb, t)[0]

        return lax.cond(jnp.max(alive) > 0.0, full, lambda ck, cv: o_near, cache_sb_k, cache_sb_v)

    def layer(l, x2d, nb, t, prompt, carried):
        kv_all, s_all = carried
        qb, k_all, v_all, kb, vb, hg = _proj(x2d, norm_mix[l], w_in_b[l], TM, l, kv_all)
        if prompt:
            osb = _sb_prompt(qb, kb, vb, nb, t, SB_BQ)
            ohg, s_all = _hgrn(hg, lbl, hg_gain[l], s0_p, 0, l, nb, t, s_all)
            mk3, mv3, base = mk_p3, mv_p3, l * nb
        else:
            osb = sb_sample(l, qb, kb, vb, nb, t)
            ohg, s_all = _hgrn(hg, lbl, hg_gain[l], s0_s, l * nb, l, nb, t, s_all)
            mk3, mv3, base = mk_s3, mv_s3, l * nb
        x1, qm = _mix(x2d, osb, ohg, sb_gain[l], w_out_b[l], norm_mem_q[l], w_mq_b[l], TM)
        om = _memattn(qm, mk3, mv3, base, nb, t, min(TM, t))
        xo = _ffn(x1, om, w_mo_b[l], norm_ffn[l], w1_b[l], w2_b[l], norm_final, l == DEPTH - 1, TM)
        return xo, ((k_all, v_all), s_all)

    xp = x_prompt.reshape(nbp * tp, D_MODEL)
    xs = x_sample.reshape(nbs * ts, D_MODEL)
    out_p = out_s = (None, None)
    for l in range(DEPTH):
        xp, out_p = layer(l, xp, nbp, tp, True, out_p)
        xs, out_s = layer(l, xs, nbs, ts, False, out_s)
    (kp, vp), sp = out_p
    (kn, vn), sn = out_s

    sbp = (DEPTH, nbp, tp, SB_HEADS, SB_HEAD_DIM)
    sbs = (DEPTH, nbs, ts, SB_HEADS, SB_HEAD_DIM)
    memp = (DEPTH, nbp, N_MEM, MEM_HEADS, MEM_HEAD_DIM)
    return (xp.reshape(nbp, tp, D_MODEL), xs.reshape(nbs, ts, D_MODEL),
            kp.reshape(sbp), vp.reshape(sbp), sp, mk_p.reshape(memp), mv_p.reshape(memp),
            kn.reshape(sbs), vn.reshape(sbs), sn)
```

```python
import functools

import jax
import jax.numpy as jnp
import numpy as np
from jax import lax
from jax.experimental import pallas as pl
from jax.experimental.pallas import tpu as pltpu

D_MODEL = 1024
DEPTH = 2
SB_HEADS = 8
SB_HEAD_DIM = 64
SB_WIDTH = SB_HEADS * SB_HEAD_DIM
HG_HEADS = 4
HG_HEAD_DIM = 128
HG_WIDTH = HG_HEADS * HG_HEAD_DIM
IN_COLS = 3 * SB_WIDTH + 4 * HG_WIDTH
N_MEM = 256
MEM_HEADS = 4
MEM_HEAD_DIM = 128
MEM_WIDTH = MEM_HEADS * MEM_HEAD_DIM
D_FF = 4 * D_MODEL
EPS = 1e-6

LANES = 128
VMEM_LIMIT = 56 * 1024 * 1024
F32 = jnp.float32
BF16 = jnp.bfloat16

_NT = (((1,), (1,)), ((), ()))
_TN = (((0,), (0,)), ((), ()))


def _params(*sem):
    return pltpu.CompilerParams(dimension_semantics=sem, vmem_limit_bytes=VMEM_LIMIT)


def _const_spec(shape):
    nd = len(shape)
    return pl.BlockSpec(shape, lambda *_: (0,) * nd, pipeline_mode=pl.Buffered(1))


def _log2(n):
    assert n > 0 and n & (n - 1) == 0, n
    return n.bit_length() - 1


def _rms(x, gain):
    ms = jnp.mean(x * x, axis=-1, keepdims=True)
    return x * lax.rsqrt(ms + EPS) * gain


def _log_sigmoid_parts(z):
    l = jnp.log(1.0 + jnp.exp(-jnp.abs(z)))
    ls = jnp.minimum(z, 0.0) - l
    return ls, ls - z


LOG2E = 1.4426950408889634


def _exp2_neg_abs(x):
    return jnp.exp2(-jnp.abs(x))


def _split_bf16(x):
    hi = x.astype(BF16)
    lo = (x - hi.astype(F32)).astype(BF16)
    return hi, lo


def _memkv_kernel(mem_ref, g_ref, wk_ref, wv_ref, mk_ref, mv_ref):
    h = _rms(mem_ref[...], g_ref[0]).astype(BF16)
    mk_ref[0] = jnp.dot(h, wk_ref[0], preferred_element_type=F32)
    mv_ref[0] = jnp.dot(h, wv_ref[0], preferred_element_type=F32)


def _memkv(mem2d, gains, wk, wv):
    n = mem2d.shape[0]
    out = jax.ShapeDtypeStruct((DEPTH, n, MEM_WIDTH), F32)
    return pl.pallas_call(
        _memkv_kernel,
        grid=(DEPTH,),
        in_specs=[
            pl.BlockSpec((n, D_MODEL), lambda l: (0, 0)),
            pl.BlockSpec((1, 1, D_MODEL), lambda l: (l, 0, 0)),
            pl.BlockSpec((1, D_MODEL, MEM_WIDTH), lambda l: (l, 0, 0)),
            pl.BlockSpec((1, D_MODEL, MEM_WIDTH), lambda l: (l, 0, 0)),
        ],
        out_specs=[pl.BlockSpec((1, n, MEM_WIDTH), lambda l: (l, 0, 0))] * 2,
        out_shape=[out, out],
        compiler_params=_params("arbitrary"),
        name="memkv",
    )(mem2d, gains.reshape(DEPTH, 1, D_MODEL), wk, wv)


def _proj_kernel(x_ref, g_ref, w_ref, *refs, slab):
    qsb_ref, ksb_ref, vsb_ref, kbf_ref, vbf_ref, hg_ref = refs[-6:]
    h = _rms(x_ref[...], g_ref[...]).astype(BF16)

    def col(c):
        return jnp.dot(h, w_ref[:, c * SB_WIDTH:(c + 1) * SB_WIDTH], preferred_element_type=F32)

    def put(ref, val):
        tiles = val.reshape(val.shape[0], SB_HEADS, SB_HEAD_DIM)
        for s in range(ref.shape[0]):
            ref[s] = tiles if s == slab else jnp.zeros_like(tiles)

    qsb_ref[...] = (col(0) * (SB_HEAD_DIM ** -0.5)).astype(BF16)
    k = col(1)
    put(ksb_ref, k)
    kbf_ref[...] = k.astype(BF16)
    v = col(2)
    put(vsb_ref, v)
    vbf_ref[...] = v.astype(BF16)
    for c in range(4):
        hg_ref[:, c * HG_WIDTH:(c + 1) * HG_WIDTH] = col(3 + c)


def _proj(x2d, gain, w_in, tm, layer, kv_all):
    n = x2d.shape[0]
    row = lambda w: pl.BlockSpec((tm, w), lambda i: (i, 0))
    in_specs = [row(D_MODEL), _const_spec((1, D_MODEL)), _const_spec((D_MODEL, IN_COLS))]
    args = (x2d, gain.reshape(1, D_MODEL), w_in)
    hdims = (SB_HEADS, SB_HEAD_DIM)
    if kv_all is None:
        kv_spec, slab, aliases = pl.BlockSpec((DEPTH, tm) + hdims, lambda i: (0, i, 0, 0)), layer, {}
    else:
        kv_spec, slab, aliases = pl.BlockSpec((1, tm) + hdims, lambda i: (layer, i, 0, 0)), 0, {3: 1, 4: 2}
        in_specs += [pl.BlockSpec(memory_space=pl.ANY)] * 2
        args += tuple(kv_all)
    return pl.pallas_call(
        functools.partial(_proj_kernel, slab=slab),
        grid=(n // tm,),
        in_specs=in_specs,
        out_specs=[row(SB_WIDTH), kv_spec, kv_spec, row(SB_WIDTH), row(SB_WIDTH), row(4 * HG_WIDTH)],
        out_shape=[
            jax.ShapeDtypeStruct((n, SB_WIDTH), BF16),
            jax.ShapeDtypeStruct((DEPTH, n) + hdims, F32),
            jax.ShapeDtypeStruct((DEPTH, n) + hdims, F32),
            jax.ShapeDtypeStruct((n, SB_WIDTH), BF16),
            jax.ShapeDtypeStruct((n, SB_WIDTH), BF16),
            jax.ShapeDtypeStruct((n, 4 * HG_WIDTH), F32),
        ],
        input_output_aliases=aliases,
        compiler_params=_params("arbitrary"),
        name="proj",
    )(*args)


SB_BK = LANES
SB_DEAD = -104.0
SB_NPAIR = 2


def _suffix_matrix():
    j = np.arange(SB_BK)[:, None]
    s = np.arange(SB_BK)[None, :]
    u = np.concatenate([(j > s).astype(np.float32), np.ones((SB_BK, SB_BK), np.float32)], axis=1)
    return jnp.asarray(np.concatenate([u, u], axis=0), dtype=BF16)


def _sbp_kernel(q_ref, k_ref, v_ref, u_ref, o_ref, acc_ref, run_ref, *, bq, npair):
    bk = SB_BK
    qi = pl.program_id(2)
    lane = lax.broadcasted_iota(jnp.int32, (1, LANES), 1)
    m_a = (lane < SB_HEAD_DIM).astype(BF16)
    m_b = (lane >= SB_HEAD_DIM).astype(BF16)
    acc_ref[...] = jnp.zeros_like(acc_ref)
    run_ref[...] = jnp.zeros_like(run_ref)
    u = u_ref[...]
    nd = bq // bk
    assert nd % 2 == 0, "earlier key blocks are taken two per trip"

    def block(j, rel):
        r0 = pl.multiple_of(j * bk, bk)
        if rel is not None:
            mask = (lax.broadcasted_iota(jnp.int32, (bq, bk), 1) + rel
                    < lax.broadcasted_iota(jnp.int32, (bq, bk), 0))
        for p in range(npair):
            ps = slice(p * LANES, (p + 1) * LANES)
            ks = k_ref[pl.ds(r0, bk), ps]
            vs = v_ref[pl.ds(r0, bk), ps]
            kcat = jnp.concatenate([ks * m_a, ks * m_b], axis=0)
            vcat = jnp.concatenate([vs * m_a, vs * m_b], axis=0)
            z = lax.dot_general(q_ref[:, ps], kcat, _NT, preferred_element_type=F32)
            a_parts = []
            for hh in range(2):
                zh = z[:, hh * bk:(hh + 1) * bk]
                ls, lk = _log_sigmoid_parts(zh)
                if rel is not None:
                    lk = jnp.where(mask, lk, 0.0)
                hi, lo = _split_bf16(lk)
                cs = jnp.dot(jnp.concatenate([hi, lo], axis=1), u, preferred_element_type=F32)
                run = run_ref[2 * p + hh]
                a = jnp.exp(ls + cs[:, :bk] + run)
                if rel is not None:
                    a = jnp.where(mask, a, 0.0)
                run_ref[2 * p + hh] = run + cs[:, bk:]
                a_parts.append(a.astype(BF16))
            acc_ref[:, ps] += jnp.dot(jnp.concatenate(a_parts, axis=1), vcat, preferred_element_type=F32)

    for d in range(nd - 1, -1, -1):
        block(qi * nd + d, d * bk)

    def live():
        return jnp.max(run_ref[...]) > SB_DEAD

    def cond(state):
        j, alive = state
        return jnp.logical_and(j >= 0, alive)

    def body(state):
        j, _ = state
        block(j, None)
        block(j - 1, None)
        return j - 2, live()

    lax.while_loop(cond, body, (qi * nd - 1, live()))
    o_ref[...] = acc_ref[...]


def _sb_prompt(qb, kb, vb, nb, t, bq):
    nq = t // bq
    w = SB_NPAIR * LANES
    return pl.pallas_call(
        functools.partial(_sbp_kernel, bq=bq, npair=SB_NPAIR),
        grid=(nb, SB_WIDTH // w, nq),
        in_specs=[
            pl.BlockSpec((bq, w), lambda b, h, i: (b * nq + i, h)),
            pl.BlockSpec((t, w), lambda b, h, i: (b, h)),
            pl.BlockSpec((t, w), lambda b, h, i: (b, h)),
            _const_spec((2 * SB_BK, 2 * SB_BK)),
        ],
        out_specs=pl.BlockSpec((bq, w), lambda b, h, i: (b * nq + i, h)),
        out_shape=jax.ShapeDtypeStruct((nb * t, SB_WIDTH), F32),
        scratch_shapes=[pltpu.VMEM((bq, w), F32), pltpu.VMEM((2 * SB_NPAIR, bq, LANES), F32)],
        compiler_params=_params("arbitrary", "arbitrary", "arbitrary"),
        name="sb_prompt",
    )(qb, kb, vb, _suffix_matrix())


SBS_PAD = 16


def _prefix_matrix():
    s = np.arange(SB_BK)[:, None]
    j = np.arange(SB_BK)[None, :]
    u = np.concatenate([(j > s).astype(np.float32), np.ones((SBS_PAD, SB_BK), np.float32)], axis=0)
    return jnp.asarray(np.concatenate([u, u], axis=1), dtype=BF16)


def _sbs_kernel(q_ref, kn_ref, vn_ref, kc_ref, vc_ref, u_ref, o_ref, alive_ref, acc_ref, run_ref,
                *, tq, past):
    bk = SB_BK
    nl = SB_HEADS * tq
    q = q_ref[...]
    row = lax.broadcasted_iota(jnp.int32, (nl, SB_WIDTH), 0)
    colw = lax.broadcasted_iota(jnp.int32, (nl, SB_WIDTH), 1)
    same_head = (row >> _log2(tq)) == (colw >> _log2(SB_HEAD_DIM))
    qx = jnp.where(same_head, jnp.concatenate([q] * SB_HEADS, axis=0), jnp.zeros((), BF16))
    u = u_ref[...]
    acc_ref[...] = jnp.zeros_like(acc_ref)
    run_ref[...] = jnp.zeros_like(run_ref)

    def block(kb, vb, mask):
        z = lax.dot_general(kb, qx, _NT, preferred_element_type=F32)
        ls, lk = _log_sigmoid_parts(z)
        if mask is not None:
            lk = jnp.where(mask, lk, 0.0)
        hi, lo = _split_bf16(lk)
        cs = jnp.dot(u, jnp.concatenate([hi, lo], axis=0), preferred_element_type=F32)
        run = run_ref[...]
        a = jnp.exp(ls + cs[:bk] + run)
        if mask is not None:
            a = jnp.where(mask, a, 0.0)
        run_ref[...] = run + cs[bk:bk + 1]
        acc_ref[...] += lax.dot_general(a.astype(BF16), vb, _TN, preferred_element_type=F32)

    pad = jnp.zeros((bk - tq, SB_WIDTH), BF16)
    srow = lax.broadcasted_iota(jnp.int32, (bk, nl), 0)
    tlane = lax.broadcasted_iota(jnp.int32, (bk, nl), 1) & (tq - 1)
    block(jnp.concatenate([kn_ref[...], pad], axis=0), jnp.concatenate([vn_ref[...], pad], axis=0),
          srow < tlane)

    def cached(ref, r0):
        return ref[0, pl.ds(r0, bk)].reshape(bk, SB_WIDTH).astype(BF16)

    def live():
        return jnp.max(run_ref[...]) > SB_DEAD

    def cond(state):
        j, alive = state
        return jnp.logical_and(j >= 0, alive)

    def body(state):
        j, _ = state
        r0 = pl.multiple_of(j * bk, bk)
        block(cached(kc_ref, r0), cached(vc_ref, r0), None)
        return j - 1, live()

    _, alive = lax.while_loop(cond, body, (past // bk - 1, live()))
    alive_ref[...] = jnp.where(alive, jnp.ones(alive_ref.shape, F32), jnp.zeros(alive_ref.shape, F32))
    accm = jnp.where(same_head, acc_ref[...], 0.0)
    out = accm[0:tq]
    for h in range(1, SB_HEADS):
        out = out + accm[h * tq:(h + 1) * tq]
    o_ref[...] = out


def _sb_sample(qb, kb, vb, kcache, vcache, layer, nb, tq, span):
    past = kcache.shape[1]
    assert past % span == 0 and span % SB_BK == 0
    rowspec = pl.BlockSpec((tq, SB_WIDTH), lambda s: (s, 0))
    cspec = pl.BlockSpec((1, span, SB_HEADS, SB_HEAD_DIM), lambda s: (layer * nb + s, past // span - 1, 0, 0))
    return pl.pallas_call(
        functools.partial(_sbs_kernel, tq=tq, past=span),
        grid=(nb,),
        in_specs=[rowspec, rowspec, rowspec, cspec, cspec, _const_spec((SB_BK + SBS_PAD, 2 * SB_BK))],
        out_specs=[rowspec, pl.BlockSpec((1, 1, LANES), lambda s: (s, 0, 0))],
        out_shape=[jax.ShapeDtypeStruct((nb * tq, SB_WIDTH), F32),
                   jax.ShapeDtypeStruct((nb, 1, LANES), F32)],
        scratch_shapes=[pltpu.VMEM((SB_HEADS * tq, SB_WIDTH), F32), pltpu.VMEM((1, SB_HEADS * tq), F32)],
        compiler_params=_params("arbitrary"),
        name="sb_sample",
    )(qb, kb, vb, kcache, vcache, _prefix_matrix())


def _hgrn_consts(c):
    levels = []
    m = 1
    while m < c:
        levels.append(m)
        m *= 2
    t = np.arange(c)[:, None]
    u = np.arange(c)[None, :]
    mats = [(u <= t).astype(np.float32)]
    masks = [np.eye(c, dtype=np.float32)]
    for m in levels:
        end_a = (t // (2 * m)) * 2 * m + m - 1
        in_b = (t % (2 * m)) >= m
        w = np.where(in_b, (u > end_a) & (u <= t), False).astype(np.float32)
        w = w - np.where(~in_b, (u > t) & (u <= end_a), False).astype(np.float32)
        mats.append(w)
        masks.append((((t // (2 * m)) == (u // (2 * m))) & in_b & ((u % (2 * m)) < m)).astype(np.float32))
    w = np.concatenate(mats, axis=0)
    return tuple(levels), jnp.asarray(np.concatenate([w, w], axis=1), dtype=BF16), jnp.asarray(np.stack(masks))


def _hgrn_kernel(hg_ref, lbl_ref, gain_ref, s0_ref, w_ref, mk_ref, *refs, c, levels, layer, slab, nchunks):
    o_ref, sfin_ref, st_ref = refs[-3:]
    step = pl.program_id(1)
    hd = HG_HEAD_DIM

    @pl.when(step == 0)
    def _():
        for h in range(HG_HEADS):
            st_ref[h] = s0_ref[0, h].T

    lg = lbl_ref[...]
    ex = jnp.exp(lg - jnp.max(lg, axis=0, keepdims=True))
    sm = ex / jnp.sum(ex, axis=0, keepdims=True)
    csum = sm[0:1]
    for l in range(1, layer + 1):
        csum = csum + sm[l:l + 1]
    lb = csum - sm[0:1]

    zf = hg_ref[:, HG_WIDTH:2 * HG_WIDTH]
    ez = jnp.exp(-jnp.abs(zf))
    lsig = jnp.minimum(zf, 0.0) - jnp.log(1.0 + ez)
    bterm = jnp.log(1.0 - lb) + lsig
    la = jnp.log(jnp.maximum(lb, 1e-30))
    logf_mix = jnp.maximum(la, bterm) + jnp.log(1.0 + jnp.exp(-jnp.abs(la - bterm)))
    logf = jnp.where(lb > 0.0, logf_mix, bterm)
    khg = (1.0 - lb) * (jnp.where(zf >= 0.0, ez, 1.0) / (1.0 + ez))

    hi, lo = _split_bf16(logf * LOG2E)
    dd = jnp.dot(w_ref[...], jnp.concatenate([hi, lo], axis=0), preferred_element_type=F32)

    for h in range(HG_HEADS):
        sl = slice(h * hd, (h + 1) * hd)
        q = hg_ref[:, h * hd:(h + 1) * hd] * (hd ** -0.5)
        k = khg[:, sl]
        v = hg_ref[:, 2 * HG_WIDTH + h * hd:2 * HG_WIDTH + (h + 1) * hd]
        g = hg_ref[:, 3 * HG_WIDTH + h * hd:3 * HG_WIDTH + (h + 1) * hd]
        vb = v.astype(BF16)
        cum = dd[0:c, sl]
        st = st_ref[h]
        o = lax.dot_general((q * jnp.exp2(cum)).astype(BF16), st.astype(BF16), _NT,
                            preferred_element_type=F32)
        sc = lax.dot_general(q.astype(BF16), k.astype(BF16), _NT, preferred_element_type=F32) * mk_ref[0]
        for li in range(len(levels)):
            e = _exp2_neg_abs(dd[(li + 1) * c:(li + 2) * c, sl])
            sc = sc + lax.dot_general((q * e).astype(BF16), (k * e).astype(BF16), _NT,
                                      preferred_element_type=F32) * mk_ref[li + 1]
        o = o + jnp.dot(sc.astype(BF16), vb, preferred_element_type=F32)
        last = cum[c - 1:c, :]
        kdec = (k * jnp.exp2(last - cum)).astype(BF16)
        st_ref[h] = st * jnp.exp2(last) + lax.dot_general(vb, kdec, _TN, preferred_element_type=F32)

        eg = jnp.exp(-jnp.abs(g))
        gate = g * (jnp.where(g >= 0.0, 1.0, eg) / (1.0 + eg))
        o_ref[:, sl] = _rms(o, gain_ref[:, sl]) * gate

    @pl.when(step == nchunks - 1)
    def _():
        for s in range(sfin_ref.shape[0]):
            for h in range(HG_HEADS):
                sfin_ref[s, 0, h] = st_ref[h].T if s == slab else jnp.zeros((hd, hd), F32)


def _hgrn(hg, lb_logits, hg_gain, s0, s0_base, layer, nb, t, s_all):
    c = min(LANES, t)
    nchunks = t // c
    levels, wcat, masks = _hgrn_consts(c)
    nl = len(levels) + 1
    sdims = (HG_HEADS, HG_HEAD_DIM, HG_HEAD_DIM)
    in_specs = [
        pl.BlockSpec((c, 4 * HG_WIDTH), lambda b, s: (b * nchunks + s, 0)),
        _const_spec((DEPTH, HG_WIDTH)),
        _const_spec((1, HG_WIDTH)),
        pl.BlockSpec((1,) + sdims, lambda b, s: (s0_base + b, 0, 0, 0)),
        _const_spec((nl * c, 2 * c)),
        _const_spec((nl, c, c)),
    ]
    args = (hg, lb_logits, hg_gain.reshape(1, HG_WIDTH), s0, wcat, masks)
    if s_all is None:
        s_spec, slab, aliases = pl.BlockSpec((DEPTH, 1) + sdims, lambda b, s: (0, b, 0, 0, 0)), layer, {}
    else:
        s_spec, slab, aliases = pl.BlockSpec((1, 1) + sdims, lambda b, s: (layer, b, 0, 0, 0)), 0, {6: 1}
        in_specs.append(pl.BlockSpec(memory_space=pl.ANY))
        args += (s_all,)
    return pl.pallas_call(
        functools.partial(_hgrn_kernel, c=c, levels=levels, layer=layer, slab=slab, nchunks=nchunks),
        grid=(nb, nchunks),
        in_specs=in_specs,
        out_specs=[pl.BlockSpec((c, HG_WIDTH), lambda b, s: (b * nchunks + s, 0)), s_spec],
        out_shape=[
            jax.ShapeDtypeStruct((nb * t, HG_WIDTH), F32),
            jax.ShapeDtypeStruct((DEPTH, nb) + sdims, F32),
        ],
        input_output_aliases=aliases,
        scratch_shapes=[pltpu.VMEM(sdims, F32)],
        compiler_params=_params("arbitrary", "arbitrary"),
        name="hgrn",
    )(*args)


def _mix_kernel(x_ref, osb_ref, ohg_ref, sbg_ref, wout_ref, nmq_ref, wmq_ref, x1_ref, qm_ref):
    a = _rms(osb_ref[...], sbg_ref[...]).astype(BF16)
    b = ohg_ref[...].astype(BF16)
    x1 = (x_ref[...]
          + jnp.dot(a, wout_ref[0:SB_WIDTH, :], preferred_element_type=F32)
          + jnp.dot(b, wout_ref[SB_WIDTH:, :], preferred_element_type=F32))
    x1_ref[...] = x1
    h = _rms(x1, nmq_ref[...]).astype(BF16)
    qm_ref[...] = jnp.dot(h, wmq_ref[...], preferred_element_type=F32).astype(BF16)


def _mix(x2d, osb, ohg, sb_gain, w_out, norm_mem_q, w_mq, tm):
    n = x2d.shape[0]
    row = lambda w: pl.BlockSpec((tm, w), lambda i: (i, 0))
    return pl.pallas_call(
        _mix_kernel,
        grid=(n // tm,),
        in_specs=[row(D_MODEL), row(SB_WIDTH), row(HG_WIDTH), _const_spec((1, SB_WIDTH)),
                  _const_spec((D_MODEL, D_MODEL)), _const_spec((1, D_MODEL)),
                  _const_spec((D_MODEL, MEM_WIDTH))],
        out_specs=[row(D_MODEL), row(MEM_WIDTH)],
        out_shape=[jax.ShapeDtypeStruct((n, D_MODEL), F32), jax.ShapeDtypeStruct((n, MEM_WIDTH), BF16)],
        compiler_params=_params("arbitrary"),
        name="mix",
    )(x2d, osb, ohg, sb_gain.reshape(1, SB_WIDTH), w_out, norm_mem_q.reshape(1, D_MODEL), w_mq)


def _memattn_kernel(q_ref, mk_ref, mv_ref, o_ref):
    hd = MEM_HEAD_DIM
    mk = mk_ref[0].reshape(N_MEM, MEM_WIDTH).astype(BF16)
    mv = mv_ref[0].reshape(N_MEM, MEM_WIDTH).astype(BF16)
    for h in range(MEM_HEADS):
        sl = slice(h * hd, (h + 1) * hd)
        s = lax.dot_general(q_ref[:, sl], mk[:, sl], _NT, preferred_element_type=F32) * (hd ** -0.5)
        p = jnp.exp(s - jnp.max(s, axis=-1, keepdims=True))
        den = jnp.sum(p, axis=-1, keepdims=True)
        o = jnp.dot(p.astype(BF16), mv[:, sl], preferred_element_type=F32)
        o_ref[:, sl] = (o / den).astype(BF16)


def _memattn(qm, mk3, mv3, base, nb, t, tm):
    nt = t // tm
    mspec = pl.BlockSpec((1, N_MEM, MEM_HEADS, MEM_HEAD_DIM), lambda b, i: (base + b, 0, 0, 0))
    return pl.pallas_call(
        _memattn_kernel,
        grid=(nb, nt),
        in_specs=[pl.BlockSpec((tm, MEM_WIDTH), lambda b, i: (b * nt + i, 0)), mspec, mspec],
        out_specs=pl.BlockSpec((tm, MEM_WIDTH), lambda b, i: (b * nt + i, 0)),
        out_shape=jax.ShapeDtypeStruct((nb * t, MEM_WIDTH), BF16),
        compiler_params=_params("arbitrary", "arbitrary"),
        name="memattn",
    )(qm, mk3, mv3)


FF_CHUNK = 1024


def _ffn_kernel(x1_ref, om_ref, wmo_ref, nf_ref, w1_ref, w2_ref, nfin_ref, o_ref, *, final):
    x2 = x1_ref[...] + jnp.dot(om_ref[...], wmo_ref[...], preferred_element_type=F32)
    h = _rms(x2, nf_ref[...]).astype(BF16)
    x3 = x2
    for c in range(D_FF // FF_CHUNK):
        cs = slice(c * FF_CHUNK, (c + 1) * FF_CHUNK)
        r = jnp.maximum(jnp.dot(h, w1_ref[:, cs], preferred_element_type=F32), 0.0)
        x3 = x3 + jnp.dot((r * r).astype(BF16), w2_ref[cs, :], preferred_element_type=F32)
    o_ref[...] = _rms(x3, nfin_ref[...]) if final else x3


def _ffn(x1, om, w_mo, norm_ffn, w1, w2, norm_final, final, tm):
    n = x1.shape[0]
    row = lambda w: pl.BlockSpec((tm, w), lambda i: (i, 0))
    return pl.pallas_call(
        functools.partial(_ffn_kernel, final=final),
        grid=(n // tm,),
        in_specs=[row(D_MODEL), row(MEM_WIDTH), _const_spec((MEM_WIDTH, D_MODEL)),
                  _const_spec((1, D_MODEL)), _const_spec((D_MODEL, D_FF)), _const_spec((D_FF, D_MODEL)),
                  _const_spec((1, D_MODEL))],
        out_specs=row(D_MODEL),
        out_shape=jax.ShapeDtypeStruct((n, D_MODEL), F32),
        compiler_params=_params("arbitrary"),
        name="ffn",
    )(x1, om, w_mo, norm_ffn.reshape(1, D_MODEL), w1, w2, norm_final.reshape(1, D_MODEL))


TM = 512
SB_BQ = 256
SBS_NEAR = 256


def kernel(x_prompt, x_sample, mem_prompt, cache_sb_k, cache_sb_v, state_hgrn, cache_mem_k, cache_mem_v,
           lb_logits, norm_mix, w_in, sb_gain, hg_gain, w_out, norm_mem_q, norm_mem_kv, w_mq, w_mk, w_mv,
           w_mo, norm_ffn, w_ffn1, w_ffn2, norm_final):
    nbp, tp, _ = x_prompt.shape
    nbs, ts, _ = x_sample.shape
    past = cache_sb_k.shape[2]
    w_in_b, w_out_b, w_mq_b, w_mk_b, w_mv_b, w_mo_b, w1_b, w2_b = (
        w.astype(BF16) for w in (w_in, w_out, w_mq, w_mk, w_mv, w_mo, w_ffn1, w_ffn2))
    lbl = lb_logits.astype(F32)

    mk_p, mv_p = _memkv(mem_prompt.reshape(nbp * N_MEM, D_MODEL), norm_mem_kv, w_mk_b, w_mv_b)
    mdims = (N_MEM, MEM_HEADS, MEM_HEAD_DIM)
    mk_p3 = mk_p.reshape((DEPTH * nbp,) + mdims)
    mv_p3 = mv_p.reshape((DEPTH * nbp,) + mdims)
    mk_s3 = cache_mem_k.reshape((DEPTH * nbs,) + mdims)
    mv_s3 = cache_mem_v.reshape((DEPTH * nbs,) + mdims)
    s0_s = state_hgrn.reshape(DEPTH * nbs, HG_HEADS, HG_HEAD_DIM, HG_HEAD_DIM)
    s0_p = jnp.zeros((nbp, HG_HEADS, HG_HEAD_DIM, HG_HEAD_DIM), F32)
    near = min(SBS_NEAR, past)
    kc4 = cache_sb_k.reshape(DEPTH * nbs, past, SB_HEADS, SB_HEAD_DIM)
    vc4 = cache_sb_v.reshape(DEPTH * nbs, past, SB_HEADS, SB_HEAD_DIM)

    def sb_sample(l, qb, kb, vb, nb, t):
        o_near, alive = _sb_sample(qb, kb, vb, kc4, vc4, l, nb, t, near)
        if near == past:
            return o_near
        return lax.cond(jnp.max(alive) > 0.0,
                        lambda: _sb_sample(qb, kb, vb, kc4, vc4, l, nb, t, past)[0],
                        lambda: o_near)

    def layer(l, x2d, nb, t, prompt, carried):
        kv_all, s_all = carried
        qb, k_all, v_all, kb, vb, hg = _proj(x2d, norm_mix[l], w_in_b[l], TM, l, kv_all)
        if prompt:
            osb = _sb_prompt(qb, kb, vb, nb, t, SB_BQ)
            ohg, s_all = _hgrn(hg, lbl, hg_gain[l], s0_p, 0, l, nb, t, s_all)
            mk3, mv3, base = mk_p3, mv_p3, l * nb
        else:
            osb = sb_sample(l, qb, kb, vb, nb, t)
            ohg, s_all = _hgrn(hg, lbl, hg_gain[l], s0_s, l * nb, l, nb, t, s_all)
            mk3, mv3, base = mk_s3, mv_s3, l * nb
        x1, qm = _mix(x2d, osb, ohg, sb_gain[l], w_out_b[l], norm_mem_q[l], w_mq_b[l], TM)
        om = _memattn(qm, mk3, mv3, base, nb, t, min(TM, t))
        xo = _ffn(x1, om, w_mo_b[l], norm_ffn[l], w1_b[l], w2_b[l], norm_final, l == DEPTH - 1, TM)
        return xo, ((k_all, v_all), s_all)

    xp = x_prompt.reshape(nbp * tp, D_MODEL)
    xs = x_sample.reshape(nbs * ts, D_MODEL)
    out_p = out_s = (None, None)
    for l in range(DEPTH):
        xp, out_p = layer(l, xp, nbp, tp, True, out_p)
        xs, out_s = layer(l, xs, nbs, ts, False, out_s)
    (kp, vp), sp = out_p
    (kn, vn), sn = out_s

    sbp = (DEPTH, nbp, tp, SB_HEADS, SB_HEAD_DIM)
    sbs = (DEPTH, nbs, ts, SB_HEADS, SB_HEAD_DIM)
    memp = (DEPTH, nbp, N_MEM, MEM_HEADS, MEM_HEAD_DIM)
    return (xp.reshape(nbp, tp, D_MODEL), xs.reshape(nbs, ts, D_MODEL),
            kp.reshape(sbp), vp.reshape(sbp), sp, mk_p3.reshape(memp), mv_p3.reshape(memp),
            kn.reshape(sbs), vn.reshape(sbs), sn)
```

```python
import functools

import jax
import jax.numpy as jnp
import numpy as np
from jax import lax
from jax.experimental import pallas as pl
from jax.experimental.pallas import tpu as pltpu

D_MODEL = 1024
DEPTH = 2
SB_HEADS = 8
SB_HEAD_DIM = 64
SB_WIDTH = SB_HEADS * SB_HEAD_DIM
HG_HEADS = 4
HG_HEAD_DIM = 128
HG_WIDTH = HG_HEADS * HG_HEAD_DIM
IN_COLS = 3 * SB_WIDTH + 4 * HG_WIDTH
N_MEM = 256
MEM_HEADS = 4
MEM_HEAD_DIM = 128
MEM_WIDTH = MEM_HEADS * MEM_HEAD_DIM
D_FF = 4 * D_MODEL
EPS = 1e-6

LANES = 128
VMEM_LIMIT = 56 * 1024 * 1024
F32 = jnp.float32
BF16 = jnp.bfloat16

_NT = (((1,), (1,)), ((), ()))
_TN = (((0,), (0,)), ((), ()))


def _params(*sem):
    return pltpu.CompilerParams(dimension_semantics=sem, vmem_limit_bytes=VMEM_LIMIT)


def _const_spec(shape):
    nd = len(shape)
    return pl.BlockSpec(shape, lambda *_: (0,) * nd, pipeline_mode=pl.Buffered(1))


def _log2(n):
    assert n > 0 and n & (n - 1) == 0, n
    return n.bit_length() - 1


def _rms(x, gain):
    ms = jnp.mean(x * x, axis=-1, keepdims=True)
    return x * lax.rsqrt(ms + EPS) * gain


def _log_sigmoid_parts(z):
    l = jnp.log(1.0 + jnp.exp(-jnp.abs(z)))
    ls = jnp.minimum(z, 0.0) - l
    return ls, ls - z


LOG2E = 1.4426950408889634


def _exp2_neg_abs(x):
    return jnp.exp2(-jnp.abs(x))


def _split_bf16(x):
    hi = x.astype(BF16)
    lo = (x - hi.astype(F32)).astype(BF16)
    return hi, lo


def _memkv_kernel(mem_ref, g_ref, wk_ref, wv_ref, mk_ref, mv_ref):
    h = _rms(mem_ref[...], g_ref[0]).astype(BF16)
    mk_ref[0] = jnp.dot(h, wk_ref[0], preferred_element_type=F32)
    mv_ref[0] = jnp.dot(h, wv_ref[0], preferred_element_type=F32)


def _memkv(mem2d, gains, wk, wv):
    n = mem2d.shape[0]
    out = jax.ShapeDtypeStruct((DEPTH, n, MEM_WIDTH), F32)
    return pl.pallas_call(
        _memkv_kernel,
        grid=(DEPTH,),
        in_specs=[
            pl.BlockSpec((n, D_MODEL), lambda l: (0, 0)),
            pl.BlockSpec((1, 1, D_MODEL), lambda l: (l, 0, 0)),
            pl.BlockSpec((1, D_MODEL, MEM_WIDTH), lambda l: (l, 0, 0)),
            pl.BlockSpec((1, D_MODEL, MEM_WIDTH), lambda l: (l, 0, 0)),
        ],
        out_specs=[pl.BlockSpec((1, n, MEM_WIDTH), lambda l: (l, 0, 0))] * 2,
        out_shape=[out, out],
        compiler_params=_params("arbitrary"),
        name="memkv",
    )(mem2d, gains.reshape(DEPTH, 1, D_MODEL), wk, wv)


def _proj_kernel(x_ref, g_ref, w_ref, *refs, slab, time_minor):
    qsb_ref, ksb_ref, vsb_ref, kbf_ref, vbf_ref, hg_ref = refs[-6:]
    h = _rms(x_ref[...], g_ref[...]).astype(BF16)

    def col(c):
        return jnp.dot(h, w_ref[:, c * SB_WIDTH:(c + 1) * SB_WIDTH], preferred_element_type=F32)

    def put(ref, val):
        out = val.T if time_minor else val.reshape(val.shape[0], SB_HEADS, SB_HEAD_DIM)
        for s in range(ref.shape[0]):
            ref[s, 0] = out if s == slab else jnp.zeros_like(out)

    qsb_ref[...] = (col(0) * (SB_HEAD_DIM ** -0.5)).astype(BF16)
    k = col(1)
    put(ksb_ref, k)
    kbf_ref[...] = k.astype(BF16)
    v = col(2)
    put(vsb_ref, v)
    vbf_ref[...] = v.astype(BF16)
    for c in range(4):
        hg_ref[:, c * HG_WIDTH:(c + 1) * HG_WIDTH] = col(3 + c)


def _proj(x2d, gain, w_in, tm, layer, kv_all, nb, t):
    n = x2d.shape[0]
    row = lambda w: pl.BlockSpec((tm, w), lambda i: (i, 0))
    in_specs = [row(D_MODEL), _const_spec((1, D_MODEL)), _const_spec((D_MODEL, IN_COLS))]
    args = (x2d, gain.reshape(1, D_MODEL), w_in)
    time_minor = t % tm == 0
    if time_minor:
        nq = t // tm
        kv_shape, blk = (DEPTH, nb, SB_WIDTH, t), (1, SB_WIDTH, tm)
        where = lambda i: (i // nq, 0, i % nq)
    else:
        kv_shape, blk = (DEPTH, n // tm, tm, SB_HEADS, SB_HEAD_DIM), (1, tm, SB_HEADS, SB_HEAD_DIM)
        where = lambda i: (i, 0, 0, 0)
    if kv_all is None:
        kv_spec, slab, aliases = pl.BlockSpec((DEPTH,) + blk, lambda i: (0,) + where(i)), layer, {}
    else:
        kv_spec, slab, aliases = pl.BlockSpec((1,) + blk, lambda i: (layer,) + where(i)), 0, {3: 1, 4: 2}
        in_specs += [pl.BlockSpec(memory_space=pl.ANY)] * 2
        args += tuple(kv_all)
    return pl.pallas_call(
        functools.partial(_proj_kernel, slab=slab, time_minor=time_minor),
        grid=(n // tm,),
        in_specs=in_specs,
        out_specs=[row(SB_WIDTH), kv_spec, kv_spec, row(SB_WIDTH), row(SB_WIDTH), row(4 * HG_WIDTH)],
        out_shape=[
            jax.ShapeDtypeStruct((n, SB_WIDTH), BF16),
            jax.ShapeDtypeStruct(kv_shape, F32),
            jax.ShapeDtypeStruct(kv_shape, F32),
            jax.ShapeDtypeStruct((n, SB_WIDTH), BF16),
            jax.ShapeDtypeStruct((n, SB_WIDTH), BF16),
            jax.ShapeDtypeStruct((n, 4 * HG_WIDTH), F32),
        ],
        input_output_aliases=aliases,
        compiler_params=_params("arbitrary"),
        name="proj",
    )(*args)


SB_BK = LANES
SB_DEAD = -104.0
SB_NPAIR = 2


def _suffix_matrix():
    j = np.arange(SB_BK)[:, None]
    s = np.arange(SB_BK)[None, :]
    u = np.concatenate([(j > s).astype(np.float32), np.ones((SB_BK, SB_BK), np.float32)], axis=1)
    return jnp.asarray(np.concatenate([u, u], axis=0), dtype=BF16)


def _sbp_kernel(q_ref, k_ref, v_ref, u_ref, o_ref, acc_ref, run_ref, *, bq, npair):
    bk = SB_BK
    qi = pl.program_id(2)
    lane = lax.broadcasted_iota(jnp.int32, (1, LANES), 1)
    m_a = (lane < SB_HEAD_DIM).astype(BF16)
    m_b = (lane >= SB_HEAD_DIM).astype(BF16)
    acc_ref[...] = jnp.zeros_like(acc_ref)
    run_ref[...] = jnp.zeros_like(run_ref)
    u = u_ref[...]
    nd = bq // bk
    assert nd % 2 == 0, "earlier key blocks are taken two per trip"

    def block(j, rel):
        r0 = pl.multiple_of(j * bk, bk)
        if rel is not None:
            mask = (lax.broadcasted_iota(jnp.int32, (bq, bk), 1) + rel
                    < lax.broadcasted_iota(jnp.int32, (bq, bk), 0))
        for p in range(npair):
            ps = slice(p * LANES, (p + 1) * LANES)
            ks = k_ref[pl.ds(r0, bk), ps]
            vs = v_ref[pl.ds(r0, bk), ps]
            kcat = jnp.concatenate([ks * m_a, ks * m_b], axis=0)
            vcat = jnp.concatenate([vs * m_a, vs * m_b], axis=0)
            z = lax.dot_general(q_ref[:, ps], kcat, _NT, preferred_element_type=F32)
            a_parts = []
            for hh in range(2):
                zh = z[:, hh * bk:(hh + 1) * bk]
                ls, lk = _log_sigmoid_parts(zh)
                if rel is not None:
                    lk = jnp.where(mask, lk, 0.0)
                hi, lo = _split_bf16(lk)
                cs = jnp.dot(jnp.concatenate([hi, lo], axis=1), u, preferred_element_type=F32)
                run = run_ref[2 * p + hh]
                a = jnp.exp(ls + cs[:, :bk] + run)
                if rel is not None:
                    a = jnp.where(mask, a, 0.0)
                run_ref[2 * p + hh] = run + cs[:, bk:]
                a_parts.append(a.astype(BF16))
            acc_ref[:, ps] += jnp.dot(jnp.concatenate(a_parts, axis=1), vcat, preferred_element_type=F32)

    for d in range(nd - 1, -1, -1):
        block(qi * nd + d, d * bk)

    def live():
        return jnp.max(run_ref[...]) > SB_DEAD

    def cond(state):
        j, alive = state
        return jnp.logical_and(j >= 0, alive)

    def body(state):
        j, _ = state
        block(j, None)
        block(j - 1, None)
        return j - 2, live()

    lax.while_loop(cond, body, (qi * nd - 1, live()))
    o_ref[...] = acc_ref[...]


def _sb_prompt(qb, kb, vb, nb, t, bq):
    nq = t // bq
    w = SB_NPAIR * LANES
    return pl.pallas_call(
        functools.partial(_sbp_kernel, bq=bq, npair=SB_NPAIR),
        grid=(nb, SB_WIDTH // w, nq),
        in_specs=[
            pl.BlockSpec((bq, w), lambda b, h, i: (b * nq + i, h)),
            pl.BlockSpec((t, w), lambda b, h, i: (b, h)),
            pl.BlockSpec((t, w), lambda b, h, i: (b, h)),
            _const_spec((2 * SB_BK, 2 * SB_BK)),
        ],
        out_specs=pl.BlockSpec((bq, w), lambda b, h, i: (b * nq + i, h)),
        out_shape=jax.ShapeDtypeStruct((nb * t, SB_WIDTH), F32),
        scratch_shapes=[pltpu.VMEM((bq, w), F32), pltpu.VMEM((2 * SB_NPAIR, bq, LANES), F32)],
        compiler_params=_params("arbitrary", "arbitrary", "arbitrary"),
        name="sb_prompt",
    )(qb, kb, vb, _suffix_matrix())


def _sbs_kernel(q_ref, kn_ref, vn_ref, kt_ref, vt_ref, u_ref, o_ref, alive_ref, acc_ref, run_ref,
                *, tq, span):
    bk = SB_BK
    nr = SB_HEADS * tq
    q = q_ref[...]
    row = lax.broadcasted_iota(jnp.int32, (nr, SB_WIDTH), 0)
    colw = lax.broadcasted_iota(jnp.int32, (nr, SB_WIDTH), 1)
    same_head = (row >> _log2(tq)) == (colw >> _log2(SB_HEAD_DIM))
    qx = jnp.where(same_head, jnp.concatenate([q] * SB_HEADS, axis=0), jnp.zeros((), BF16))
    u = u_ref[...]
    acc_ref[...] = jnp.zeros_like(acc_ref)
    run_ref[...] = jnp.zeros_like(run_ref)

    def weights(z, mask):
        ls, lk = _log_sigmoid_parts(z)
        if mask is not None:
            lk = jnp.where(mask, lk, 0.0)
        hi, lo = _split_bf16(lk)
        cs = jnp.dot(jnp.concatenate([hi, lo], axis=1), u, preferred_element_type=F32)
        run = run_ref[...]
        a = jnp.exp(ls + cs[:, :bk] + run)
        if mask is not None:
            a = jnp.where(mask, a, 0.0)
        run_ref[...] = run + cs[:, bk:]
        return a.astype(BF16)

    pad = jnp.zeros((bk - tq, SB_WIDTH), BF16)
    knew = jnp.concatenate([kn_ref[...], pad], axis=0)
    vnew = jnp.concatenate([vn_ref[...], pad], axis=0)
    skey = lax.broadcasted_iota(jnp.int32, (nr, bk), 1)
    tqry = lax.broadcasted_iota(jnp.int32, (nr, bk), 0) & (tq - 1)
    a = weights(lax.dot_general(qx, knew, _NT, preferred_element_type=F32), skey < tqry)
    acc_ref[...] += jnp.dot(a, vnew, preferred_element_type=F32)

    def live():
        return jnp.max(run_ref[...]) > SB_DEAD

    for j in range(span // bk - 1, -1, -1):
        @pl.when(live())
        def _():
            kt = kt_ref[0, :, j * bk:(j + 1) * bk].astype(BF16)
            vt = vt_ref[0, :, j * bk:(j + 1) * bk].astype(BF16)
            a = weights(jnp.dot(qx, kt, preferred_element_type=F32), None)
            acc_ref[...] += lax.dot_general(a, vt, _NT, preferred_element_type=F32)

    alive_ref[...] = jnp.where(live(), jnp.ones(alive_ref.shape, F32), jnp.zeros(alive_ref.shape, F32))
    accm = jnp.where(same_head, acc_ref[...], 0.0)
    out = accm[0:tq]
    for h in range(1, SB_HEADS):
        out = out + accm[h * tq:(h + 1) * tq]
    o_ref[...] = out


def _sb_sample(qb, kb, vb, kcache, vcache, layer, nb, tq, span):
    past = kcache.shape[2]
    assert past % span == 0 and span % SB_BK == 0
    rowspec = pl.BlockSpec((tq, SB_WIDTH), lambda s: (s, 0))
    cspec = pl.BlockSpec((1, SB_WIDTH, span), lambda s: (layer * nb + s, 0, past // span - 1))
    nr = SB_HEADS * tq
    return pl.pallas_call(
        functools.partial(_sbs_kernel, tq=tq, span=span),
        grid=(nb,),
        in_specs=[rowspec, rowspec, rowspec, cspec, cspec, _const_spec((2 * SB_BK, 2 * SB_BK))],
        out_specs=[rowspec, pl.BlockSpec((1, 1, LANES), lambda s: (s, 0, 0))],
        out_shape=[jax.ShapeDtypeStruct((nb * tq, SB_WIDTH), F32),
                   jax.ShapeDtypeStruct((nb, 1, LANES), F32)],
        scratch_shapes=[pltpu.VMEM((nr, SB_WIDTH), F32), pltpu.VMEM((nr, SB_BK), F32)],
        compiler_params=_params("arbitrary"),
        name="sb_sample",
    )(qb, kb, vb, kcache, vcache, _suffix_matrix())


def _hgrn_consts(c):
    levels = []
    m = 1
    while m < c:
        levels.append(m)
        m *= 2
    t = np.arange(c)[:, None]
    u = np.arange(c)[None, :]
    mats = [(u <= t).astype(np.float32)]
    masks = [np.eye(c, dtype=np.float32)]
    for m in levels:
        end_a = (t // (2 * m)) * 2 * m + m - 1
        in_b = (t % (2 * m)) >= m
        w = np.where(in_b, (u > end_a) & (u <= t), False).astype(np.float32)
        w = w - np.where(~in_b, (u > t) & (u <= end_a), False).astype(np.float32)
        mats.append(w)
        masks.append((((t // (2 * m)) == (u // (2 * m))) & in_b & ((u % (2 * m)) < m)).astype(np.float32))
    w = np.concatenate(mats, axis=0)
    return tuple(levels), jnp.asarray(np.concatenate([w, w], axis=1), dtype=BF16), jnp.asarray(np.stack(masks))


def _hgrn_kernel(hg_ref, lbl_ref, gain_ref, s0_ref, w_ref, mk_ref, *refs, c, levels, layer, slab, nchunks):
    o_ref, sfin_ref, st_ref = refs[-3:]
    step = pl.program_id(1)
    hd = HG_HEAD_DIM

    @pl.when(step == 0)
    def _():
        for h in range(HG_HEADS):
            st_ref[h] = s0_ref[0, h].T

    lg = lbl_ref[...]
    ex = jnp.exp(lg - jnp.max(lg, axis=0, keepdims=True))
    sm = ex / jnp.sum(ex, axis=0, keepdims=True)
    csum = sm[0:1]
    for l in range(1, layer + 1):
        csum = csum + sm[l:l + 1]
    lb = csum - sm[0:1]

    zf = hg_ref[:, HG_WIDTH:2 * HG_WIDTH]
    ez = jnp.exp(-jnp.abs(zf))
    lsig = jnp.minimum(zf, 0.0) - jnp.log(1.0 + ez)
    bterm = jnp.log(1.0 - lb) + lsig
    la = jnp.log(jnp.maximum(lb, 1e-30))
    logf_mix = jnp.maximum(la, bterm) + jnp.log(1.0 + jnp.exp(-jnp.abs(la - bterm)))
    logf = jnp.where(lb > 0.0, logf_mix, bterm)
    khg = (1.0 - lb) * (jnp.where(zf >= 0.0, ez, 1.0) / (1.0 + ez))

    hi, lo = _split_bf16(logf * LOG2E)
    dd = jnp.dot(w_ref[...], jnp.concatenate([hi, lo], axis=0), preferred_element_type=F32)

    for h in range(HG_HEADS):
        sl = slice(h * hd, (h + 1) * hd)
        q = hg_ref[:, h * hd:(h + 1) * hd] * (hd ** -0.5)
        k = khg[:, sl]
        v = hg_ref[:, 2 * HG_WIDTH + h * hd:2 * HG_WIDTH + (h + 1) * hd]
        g = hg_ref[:, 3 * HG_WIDTH + h * hd:3 * HG_WIDTH + (h + 1) * hd]
        vb = v.astype(BF16)
        cum = dd[0:c, sl]
        st = st_ref[h]
        o = lax.dot_general((q * jnp.exp2(cum)).astype(BF16), st.astype(BF16), _NT,
                            preferred_element_type=F32)
        sc = lax.dot_general(q.astype(BF16), k.astype(BF16), _NT, preferred_element_type=F32) * mk_ref[0]
        for li in range(len(levels)):
            e = _exp2_neg_abs(dd[(li + 1) * c:(li + 2) * c, sl])
            sc = sc + lax.dot_general((q * e).astype(BF16), (k * e).astype(BF16), _NT,
                                      preferred_element_type=F32) * mk_ref[li + 1]
        o = o + jnp.dot(sc.astype(BF16), vb, preferred_element_type=F32)
        last = cum[c - 1:c, :]
        kdec = (k * jnp.exp2(last - cum)).astype(BF16)
        st_ref[h] = st * jnp.exp2(last) + lax.dot_general(vb, kdec, _TN, preferred_element_type=F32)

        eg = jnp.exp(-jnp.abs(g))
        gate = g * (jnp.where(g >= 0.0, 1.0, eg) / (1.0 + eg))
        o_ref[:, sl] = _rms(o, gain_ref[:, sl]) * gate

    @pl.when(step == nchunks - 1)
    def _():
        for s in range(sfin_ref.shape[0]):
            for h in range(HG_HEADS):
                sfin_ref[s, 0, h] = st_ref[h].T if s == slab else jnp.zeros((hd, hd), F32)


def _hgrn(hg, lb_logits, hg_gain, s0, s0_base, layer, nb, t, s_all):
    c = min(LANES, t)
    nchunks = t // c
    levels, wcat, masks = _hgrn_consts(c)
    nl = len(levels) + 1
    sdims = (HG_HEADS, HG_HEAD_DIM, HG_HEAD_DIM)
    in_specs = [
        pl.BlockSpec((c, 4 * HG_WIDTH), lambda b, s: (b * nchunks + s, 0)),
        _const_spec((DEPTH, HG_WIDTH)),
        _const_spec((1, HG_WIDTH)),
        pl.BlockSpec((1,) + sdims, lambda b, s: (s0_base + b, 0, 0, 0)),
        _const_spec((nl * c, 2 * c)),
        _const_spec((nl, c, c)),
    ]
    args = (hg, lb_logits, hg_gain.reshape(1, HG_WIDTH), s0, wcat, masks)
    if s_all is None:
        s_spec, slab, aliases = pl.BlockSpec((DEPTH, 1) + sdims, lambda b, s: (0, b, 0, 0, 0)), layer, {}
    else:
        s_spec, slab, aliases = pl.BlockSpec((1, 1) + sdims, lambda b, s: (layer, b, 0, 0, 0)), 0, {6: 1}
        in_specs.append(pl.BlockSpec(memory_space=pl.ANY))
        args += (s_all,)
    return pl.pallas_call(
        functools.partial(_hgrn_kernel, c=c, levels=levels, layer=layer, slab=slab, nchunks=nchunks),
        grid=(nb, nchunks),
        in_specs=in_specs,
        out_specs=[pl.BlockSpec((c, HG_WIDTH), lambda b, s: (b * nchunks + s, 0)), s_spec],
        out_shape=[
            jax.ShapeDtypeStruct((nb * t, HG_WIDTH), F32),
            jax.ShapeDtypeStruct((DEPTH, nb) + sdims, F32),
        ],
        input_output_aliases=aliases,
        scratch_shapes=[pltpu.VMEM(sdims, F32)],
        compiler_params=_params("arbitrary", "arbitrary"),
        name="hgrn",
    )(*args)


def _mix_kernel(x_ref, osb_ref, ohg_ref, sbg_ref, wout_ref, nmq_ref, wmq_ref, x1_ref, qm_ref):
    a = _rms(osb_ref[...], sbg_ref[...]).astype(BF16)
    b = ohg_ref[...].astype(BF16)
    x1 = (x_ref[...]
          + jnp.dot(a, wout_ref[0:SB_WIDTH, :], preferred_element_type=F32)
          + jnp.dot(b, wout_ref[SB_WIDTH:, :], preferred_element_type=F32))
    x1_ref[...] = x1
    h = _rms(x1, nmq_ref[...]).astype(BF16)
    qm_ref[...] = jnp.dot(h, wmq_ref[...], preferred_element_type=F32).astype(BF16)


def _mix(x2d, osb, ohg, sb_gain, w_out, norm_mem_q, w_mq, tm):
    n = x2d.shape[0]
    row = lambda w: pl.BlockSpec((tm, w), lambda i: (i, 0))
    return pl.pallas_call(
        _mix_kernel,
        grid=(n // tm,),
        in_specs=[row(D_MODEL), row(SB_WIDTH), row(HG_WIDTH), _const_spec((1, SB_WIDTH)),
                  _const_spec((D_MODEL, D_MODEL)), _const_spec((1, D_MODEL)),
                  _const_spec((D_MODEL, MEM_WIDTH))],
        out_specs=[row(D_MODEL), row(MEM_WIDTH)],
        out_shape=[jax.ShapeDtypeStruct((n, D_MODEL), F32), jax.ShapeDtypeStruct((n, MEM_WIDTH), BF16)],
        compiler_params=_params("arbitrary"),
        name="mix",
    )(x2d, osb, ohg, sb_gain.reshape(1, SB_WIDTH), w_out, norm_mem_q.reshape(1, D_MODEL), w_mq)


def _memattn_kernel(q_ref, mk_ref, mv_ref, o_ref):
    hd = MEM_HEAD_DIM
    mk = mk_ref[0].reshape(N_MEM, MEM_WIDTH).astype(BF16)
    mv = mv_ref[0].reshape(N_MEM, MEM_WIDTH).astype(BF16)
    for h in range(MEM_HEADS):
        sl = slice(h * hd, (h + 1) * hd)
        s = lax.dot_general(q_ref[:, sl], mk[:, sl], _NT, preferred_element_type=F32) * (hd ** -0.5)
        p = jnp.exp(s - jnp.max(s, axis=-1, keepdims=True))
        den = jnp.sum(p, axis=-1, keepdims=True)
        o = jnp.dot(p.astype(BF16), mv[:, sl], preferred_element_type=F32)
        o_ref[:, sl] = (o / den).astype(BF16)


def _memattn(qm, mk3, mv3, base, nb, t, tm):
    nt = t // tm
    mspec = pl.BlockSpec((1, N_MEM, MEM_HEADS, MEM_HEAD_DIM), lambda b, i: (base + b, 0, 0, 0))
    return pl.pallas_call(
        _memattn_kernel,
        grid=(nb, nt),
        in_specs=[pl.BlockSpec((tm, MEM_WIDTH), lambda b, i: (b * nt + i, 0)), mspec, mspec],
        out_specs=pl.BlockSpec((tm, MEM_WIDTH), lambda b, i: (b * nt + i, 0)),
        out_shape=jax.ShapeDtypeStruct((nb * t, MEM_WIDTH), BF16),
        compiler_params=_params("arbitrary", "arbitrary"),
        name="memattn",
    )(qm, mk3, mv3)


FF_CHUNK = 1024


def _ffn_kernel(x1_ref, om_ref, wmo_ref, nf_ref, w1_ref, w2_ref, nfin_ref, o_ref, *, final):
    x2 = x1_ref[...] + jnp.dot(om_ref[...], wmo_ref[...], preferred_element_type=F32)
    h = _rms(x2, nf_ref[...]).astype(BF16)
    x3 = x2
    for c in range(D_FF // FF_CHUNK):
        cs = slice(c * FF_CHUNK, (c + 1) * FF_CHUNK)
        r = jnp.maximum(jnp.dot(h, w1_ref[:, cs], preferred_element_type=F32), 0.0)
        x3 = x3 + jnp.dot((r * r).astype(BF16), w2_ref[cs, :], preferred_element_type=F32)
    o_ref[...] = _rms(x3, nfin_ref[...]) if final else x3


def _ffn(x1, om, w_mo, norm_ffn, w1, w2, norm_final, final, tm):
    n = x1.shape[0]
    row = lambda w: pl.BlockSpec((tm, w), lambda i: (i, 0))
    return pl.pallas_call(
        functools.partial(_ffn_kernel, final=final),
        grid=(n // tm,),
        in_specs=[row(D_MODEL), row(MEM_WIDTH), _const_spec((MEM_WIDTH, D_MODEL)),
                  _const_spec((1, D_MODEL)), _const_spec((D_MODEL, D_FF)), _const_spec((D_FF, D_MODEL)),
                  _const_spec((1, D_MODEL))],
        out_specs=row(D_MODEL),
        out_shape=jax.ShapeDtypeStruct((n, D_MODEL), F32),
        compiler_params=_params("arbitrary"),
        name="ffn",
    )(x1, om, w_mo, norm_ffn.reshape(1, D_MODEL), w1, w2, norm_final.reshape(1, D_MODEL))


TM = 512
SB_BQ = 256
SBS_NEAR = 256


def kernel(x_prompt, x_sample, mem_prompt, cache_sb_k, cache_sb_v, state_hgrn, cache_mem_k, cache_mem_v,
           lb_logits, norm_mix, w_in, sb_gain, hg_gain, w_out, norm_mem_q, norm_mem_kv, w_mq, w_mk, w_mv,
           w_mo, norm_ffn, w_ffn1, w_ffn2, norm_final):
    nbp, tp, _ = x_prompt.shape
    nbs, ts, _ = x_sample.shape
    past = cache_sb_k.shape[2]
    w_in_b, w_out_b, w_mq_b, w_mk_b, w_mv_b, w_mo_b, w1_b, w2_b = (
        w.astype(BF16) for w in (w_in, w_out, w_mq, w_mk, w_mv, w_mo, w_ffn1, w_ffn2))
    lbl = lb_logits.astype(F32)

    mk_p, mv_p = _memkv(mem_prompt.reshape(nbp * N_MEM, D_MODEL), norm_mem_kv, w_mk_b, w_mv_b)
    mdims = (N_MEM, MEM_HEADS, MEM_HEAD_DIM)
    mk_p3 = mk_p.reshape((DEPTH * nbp,) + mdims)
    mv_p3 = mv_p.reshape((DEPTH * nbp,) + mdims)
    mk_s3 = cache_mem_k.reshape((DEPTH * nbs,) + mdims)
    mv_s3 = cache_mem_v.reshape((DEPTH * nbs,) + mdims)
    s0_s = state_hgrn.reshape(DEPTH * nbs, HG_HEADS, HG_HEAD_DIM, HG_HEAD_DIM)
    s0_p = jnp.zeros((nbp, HG_HEADS, HG_HEAD_DIM, HG_HEAD_DIM), F32)
    near = min(SBS_NEAR, past)
    kc4 = cache_sb_k.transpose(0, 1, 3, 4, 2).reshape(DEPTH * nbs, SB_WIDTH, past)
    vc4 = cache_sb_v.transpose(0, 1, 3, 4, 2).reshape(DEPTH * nbs, SB_WIDTH, past)

    def sb_sample(l, qb, kb, vb, nb, t):
        o_near, alive = _sb_sample(qb, kb, vb, kc4, vc4, l, nb, t, near)
        if near == past:
            return o_near
        return lax.cond(jnp.max(alive) > 0.0,
                        lambda: _sb_sample(qb, kb, vb, kc4, vc4, l, nb, t, past)[0],
                        lambda: o_near)

    def layer(l, x2d, nb, t, prompt, carried):
        kv_all, s_all = carried
        qb, k_all, v_all, kb, vb, hg = _proj(x2d, norm_mix[l], w_in_b[l], TM, l, kv_all, nb, t)
        if prompt:
            osb = _sb_prompt(qb, kb, vb, nb, t, SB_BQ)
            ohg, s_all = _hgrn(hg, lbl, hg_gain[l], s0_p, 0, l, nb, t, s_all)
            mk3, mv3, base = mk_p3, mv_p3, l * nb
        else:
            osb = sb_sample(l, qb, kb, vb, nb, t)
            ohg, s_all = _hgrn(hg, lbl, hg_gain[l], s0_s, l * nb, l, nb, t, s_all)
            mk3, mv3, base = mk_s3, mv_s3, l * nb
        x1, qm = _mix(x2d, osb, ohg, sb_gain[l], w_out_b[l], norm_mem_q[l], w_mq_b[l], TM)
        om = _memattn(qm, mk3, mv3, base, nb, t, min(TM, t))
        xo = _ffn(x1, om, w_mo_b[l], norm_ffn[l], w1_b[l], w2_b[l], norm_final, l == DEPTH - 1, TM)
        return xo, ((k_all, v_all), s_all)

    xp = x_prompt.reshape(nbp * tp, D_MODEL)
    xs = x_sample.reshape(nbs * ts, D_MODEL)
    out_p = out_s = (None, None)
    for l in range(DEPTH):
        xp, out_p = layer(l, xp, nbp, tp, True, out_p)
        xs, out_s = layer(l, xs, nbs, ts, False, out_s)
    (kp, vp), sp = out_p
    (kn, vn), sn = out_s

    def heads_last(a):
        return a.reshape(DEPTH, nbp, SB_HEADS, SB_HEAD_DIM, tp).transpose(0, 1, 4, 2, 3)

    sbs = (DEPTH, nbs, ts, SB_HEADS, SB_HEAD_DIM)
    memp = (DEPTH, nbp, N_MEM, MEM_HEADS, MEM_HEAD_DIM)
    return (xp.reshape(nbp, tp, D_MODEL), xs.reshape(nbs, ts, D_MODEL),
            heads_last(kp), heads_last(vp), sp, mk_p3.reshape(memp), mv_p3.reshape(memp),
            kn.reshape(sbs), vn.reshape(sbs), sn)
```

```python
import functools

import jax
import jax.numpy as jnp
import numpy as np
from jax import lax
from jax.experimental import pallas as pl
from jax.experimental.pallas import tpu as pltpu

D_MODEL = 1024
DEPTH = 2
SB_HEADS = 8
SB_HEAD_DIM = 64
SB_WIDTH = SB_HEADS * SB_HEAD_DIM
HG_HEADS = 4
HG_HEAD_DIM = 128
HG_WIDTH = HG_HEADS * HG_HEAD_DIM
IN_COLS = 3 * SB_WIDTH + 4 * HG_WIDTH
HG_COLS = 6 * HG_WIDTH
N_MEM = 256
MEM_HEADS = 4
MEM_HEAD_DIM = 128
MEM_WIDTH = MEM_HEADS * MEM_HEAD_DIM
D_FF = 4 * D_MODEL
EPS = 1e-6

LANES = 128
VMEM_LIMIT = 56 * 1024 * 1024
F32 = jnp.float32
BF16 = jnp.bfloat16

_NT = (((1,), (1,)), ((), ()))
_TN = (((0,), (0,)), ((), ()))


def _params(*sem):
    return pltpu.CompilerParams(dimension_semantics=sem, vmem_limit_bytes=VMEM_LIMIT)


def _const_spec(shape):
    nd = len(shape)
    return pl.BlockSpec(shape, lambda *_: (0,) * nd, pipeline_mode=pl.Buffered(1))


def _log2(n):
    assert n > 0 and n & (n - 1) == 0, n
    return n.bit_length() - 1


def _rms(x, gain):
    ms = jnp.mean(x * x, axis=-1, keepdims=True)
    return x * lax.rsqrt(ms + EPS) * gain


def _log_sigmoid_parts(z):
    l = jnp.log(1.0 + jnp.exp(-jnp.abs(z)))
    ls = jnp.minimum(z, 0.0) - l
    return ls, ls - z


LOG2E = 1.4426950408889634


def _exp2_neg_abs(x):
    return jnp.exp2(-jnp.abs(x))


def _split_bf16(x):
    hi = x.astype(BF16)
    lo = (x - hi.astype(F32)).astype(BF16)
    return hi, lo


def _memkv_kernel(mem_ref, g_ref, wk_ref, wv_ref, mk_ref, mv_ref):
    h = _rms(mem_ref[...], g_ref[0]).astype(BF16)
    mk_ref[0] = jnp.dot(h, wk_ref[0], preferred_element_type=F32)
    mv_ref[0] = jnp.dot(h, wv_ref[0], preferred_element_type=F32)


def _memkv(mem2d, gains, wk, wv):
    n = mem2d.shape[0]
    out = jax.ShapeDtypeStruct((DEPTH, n, MEM_WIDTH), F32)
    return pl.pallas_call(
        _memkv_kernel,
        grid=(DEPTH,),
        in_specs=[
            pl.BlockSpec((n, D_MODEL), lambda l: (0, 0)),
            pl.BlockSpec((1, 1, D_MODEL), lambda l: (l, 0, 0)),
            pl.BlockSpec((1, D_MODEL, MEM_WIDTH), lambda l: (l, 0, 0)),
            pl.BlockSpec((1, D_MODEL, MEM_WIDTH), lambda l: (l, 0, 0)),
        ],
        out_specs=[pl.BlockSpec((1, n, MEM_WIDTH), lambda l: (l, 0, 0))] * 2,
        out_shape=[out, out],
        compiler_params=_params("arbitrary"),
        name="memkv",
    )(mem2d, gains.reshape(DEPTH, 1, D_MODEL), wk, wv)


def _proj_kernel(x_ref, g_ref, w_ref, lbl_ref, *refs, layer, slab, time_minor):
    qsb_ref, ksb_ref, vsb_ref, kbf_ref, vbf_ref, hg_ref = refs[-6:]
    h = _rms(x_ref[...], g_ref[...]).astype(BF16)

    def col(c):
        return jnp.dot(h, w_ref[:, c * SB_WIDTH:(c + 1) * SB_WIDTH], preferred_element_type=F32)

    def put(ref, val):
        out = val.T if time_minor else val.reshape(val.shape[0], SB_HEADS, SB_HEAD_DIM)
        for s in range(ref.shape[0]):
            ref[s, 0] = out if s == slab else jnp.zeros_like(out)

    qsb_ref[...] = (col(0) * (SB_HEAD_DIM ** -0.5)).astype(BF16)
    k = col(1)
    put(ksb_ref, k)
    kbf_ref[...] = k.astype(BF16)
    v = col(2)
    put(vsb_ref, v)
    vbf_ref[...] = v.astype(BF16)

    def put_hg(c, val):
        hg_ref[:, c * HG_WIDTH:(c + 1) * HG_WIDTH] = val.astype(BF16)

    lg = lbl_ref[...]
    ex = jnp.exp(lg - jnp.max(lg, axis=0, keepdims=True))
    sm = ex / jnp.sum(ex, axis=0, keepdims=True)
    csum = sm[0:1]
    for l in range(1, layer + 1):
        csum = csum + sm[l:l + 1]
    lb = csum - sm[0:1]

    put_hg(0, col(3) * (HG_HEAD_DIM ** -0.5))
    zf = col(4)
    ez = jnp.exp(-jnp.abs(zf))
    lsig = jnp.minimum(zf, 0.0) - jnp.log(1.0 + ez)
    bterm = jnp.log(1.0 - lb) + lsig
    la = jnp.log(jnp.maximum(lb, 1e-30))
    logf_mix = jnp.maximum(la, bterm) + jnp.log(1.0 + jnp.exp(-jnp.abs(la - bterm)))
    logf = jnp.where(lb > 0.0, logf_mix, bterm)
    put_hg(1, (1.0 - lb) * (jnp.where(zf >= 0.0, ez, 1.0) / (1.0 + ez)))
    put_hg(2, col(5))
    hi, lo = _split_bf16(logf * LOG2E)
    put_hg(3, hi)
    put_hg(4, lo)
    g = col(6)
    eg = jnp.exp(-jnp.abs(g))
    put_hg(5, g * (jnp.where(g >= 0.0, 1.0, eg) / (1.0 + eg)))


def _proj(x2d, gain, w_in, lb_logits, tm, layer, kv_all, nb, t):
    n = x2d.shape[0]
    row = lambda w: pl.BlockSpec((tm, w), lambda i: (i, 0))
    in_specs = [row(D_MODEL), _const_spec((1, D_MODEL)), _const_spec((D_MODEL, IN_COLS)),
                _const_spec((DEPTH, HG_WIDTH))]
    args = (x2d, gain.reshape(1, D_MODEL), w_in, lb_logits)
    time_minor = t % tm == 0
    if time_minor:
        nq = t // tm
        kv_shape, blk = (DEPTH, nb, SB_WIDTH, t), (1, SB_WIDTH, tm)
        where = lambda i: (i // nq, 0, i % nq)
    else:
        kv_shape, blk = (DEPTH, n // tm, tm, SB_HEADS, SB_HEAD_DIM), (1, tm, SB_HEADS, SB_HEAD_DIM)
        where = lambda i: (i, 0, 0, 0)
    if kv_all is None:
        kv_spec, slab, aliases = pl.BlockSpec((DEPTH,) + blk, lambda i: (0,) + where(i)), layer, {}
    else:
        kv_spec, slab, aliases = pl.BlockSpec((1,) + blk, lambda i: (layer,) + where(i)), 0, {4: 1, 5: 2}
        in_specs += [pl.BlockSpec(memory_space=pl.ANY)] * 2
        args += tuple(kv_all)
    return pl.pallas_call(
        functools.partial(_proj_kernel, layer=layer, slab=slab, time_minor=time_minor),
        grid=(n // tm,),
        in_specs=in_specs,
        out_specs=[row(SB_WIDTH), kv_spec, kv_spec, row(SB_WIDTH), row(SB_WIDTH), row(HG_COLS)],
        out_shape=[
            jax.ShapeDtypeStruct((n, SB_WIDTH), BF16),
            jax.ShapeDtypeStruct(kv_shape, F32),
            jax.ShapeDtypeStruct(kv_shape, F32),
            jax.ShapeDtypeStruct((n, SB_WIDTH), BF16),
            jax.ShapeDtypeStruct((n, SB_WIDTH), BF16),
            jax.ShapeDtypeStruct((n, HG_COLS), BF16),
        ],
        input_output_aliases=aliases,
        compiler_params=_params("arbitrary"),
        name="proj",
    )(*args)


SB_BK = LANES
SB_DEAD = -104.0
SB_NPAIR = 2


def _suffix_matrix():
    j = np.arange(SB_BK)[:, None]
    s = np.arange(SB_BK)[None, :]
    u = np.concatenate([(j > s).astype(np.float32), np.ones((SB_BK, SB_BK), np.float32)], axis=1)
    return jnp.asarray(np.concatenate([u, u], axis=0), dtype=BF16)


def _sbp_kernel(q_ref, k_ref, v_ref, u_ref, o_ref, acc_ref, run_ref, *, bq, npair):
    bk = SB_BK
    qi = pl.program_id(2)
    lane = lax.broadcasted_iota(jnp.int32, (1, LANES), 1)
    m_a = (lane < SB_HEAD_DIM).astype(BF16)
    m_b = (lane >= SB_HEAD_DIM).astype(BF16)
    acc_ref[...] = jnp.zeros_like(acc_ref)
    run_ref[...] = jnp.zeros_like(run_ref)
    u = u_ref[...]
    nd = bq // bk
    assert nd % 2 == 0, "earlier key blocks are taken two per trip"

    def sweep(blocks):
        units = [(j, rel, p) for j, rel in blocks for p in range(npair)]
        ps = lambda p: slice(p * LANES, (p + 1) * LANES)
        rows = lambda j: pl.ds(pl.multiple_of(j * bk, bk), bk)
        masks = {rel: (lax.broadcasted_iota(jnp.int32, (bq, bk), 1) + rel
                       < lax.broadcasted_iota(jnp.int32, (bq, bk), 0))
                 for _, rel in blocks if rel is not None}
        zs = []
        for j, rel, p in units:
            ks = k_ref[rows(j), ps(p)]
            kcat = jnp.concatenate([ks * m_a, ks * m_b], axis=0)
            zs.append(lax.dot_general(q_ref[:, ps(p)], kcat, _NT, preferred_element_type=F32))
        lss, css = [], []
        for (j, rel, p), z in zip(units, zs):
            for hh in range(2):
                ls, lk = _log_sigmoid_parts(z[:, hh * bk:(hh + 1) * bk])
                if rel is not None:
                    lk = jnp.where(masks[rel], lk, 0.0)
                hi, lo = _split_bf16(lk)
                lss.append(ls)
                css.append(jnp.dot(jnp.concatenate([hi, lo], axis=1), u, preferred_element_type=F32))
        for i, (j, rel, p) in enumerate(units):
            a_parts = []
            for hh in range(2):
                ls, cs = lss[2 * i + hh], css[2 * i + hh]
                run = run_ref[2 * p + hh]
                a = jnp.exp(ls + cs[:, :bk] + run)
                if rel is not None:
                    a = jnp.where(masks[rel], a, 0.0)
                run_ref[2 * p + hh] = run + cs[:, bk:]
                a_parts.append(a.astype(BF16))
            vs = v_ref[rows(j), ps(p)]
            vcat = jnp.concatenate([vs * m_a, vs * m_b], axis=0)
            acc_ref[:, ps(p)] += jnp.dot(jnp.concatenate(a_parts, axis=1), vcat,
                                         preferred_element_type=F32)

    sweep([(qi * nd + d, d * bk) for d in range(nd - 1, -1, -1)])

    def live():
        return jnp.max(run_ref[...]) > SB_DEAD

    def cond(state):
        j, alive = state
        return jnp.logical_and(j >= 0, alive)

    def body(state):
        j, _ = state
        sweep([(j, None), (j - 1, None)])
        return j - 2, live()

    lax.while_loop(cond, body, (qi * nd - 1, live()))
    o_ref[...] = acc_ref[...]


def _sb_prompt(qb, kb, vb, nb, t, bq):
    nq = t // bq
    w = SB_NPAIR * LANES
    return pl.pallas_call(
        functools.partial(_sbp_kernel, bq=bq, npair=SB_NPAIR),
        grid=(nb, SB_WIDTH // w, nq),
        in_specs=[
            pl.BlockSpec((bq, w), lambda b, h, i: (b * nq + i, h)),
            pl.BlockSpec((t, w), lambda b, h, i: (b, h)),
            pl.BlockSpec((t, w), lambda b, h, i: (b, h)),
            _const_spec((2 * SB_BK, 2 * SB_BK)),
        ],
        out_specs=pl.BlockSpec((bq, w), lambda b, h, i: (b * nq + i, h)),
        out_shape=jax.ShapeDtypeStruct((nb * t, SB_WIDTH), F32),
        scratch_shapes=[pltpu.VMEM((bq, w), F32), pltpu.VMEM((2 * SB_NPAIR, bq, LANES), F32)],
        compiler_params=_params("arbitrary", "arbitrary", "arbitrary"),
        name="sb_prompt",
    )(qb, kb, vb, _suffix_matrix())


def _sbs_kernel(q_ref, kn_ref, vn_ref, kt_ref, vt_ref, u_ref, o_ref, alive_ref, acc_ref, run_ref,
                *, tq, span):
    bk = SB_BK
    nr = SB_HEADS * tq
    q = q_ref[...]
    row = lax.broadcasted_iota(jnp.int32, (nr, SB_WIDTH), 0)
    colw = lax.broadcasted_iota(jnp.int32, (nr, SB_WIDTH), 1)
    same_head = (row >> _log2(tq)) == (colw >> _log2(SB_HEAD_DIM))
    qx = jnp.where(same_head, jnp.concatenate([q] * SB_HEADS, axis=0), jnp.zeros((), BF16))
    u = u_ref[...]
    acc_ref[...] = jnp.zeros_like(acc_ref)
    run_ref[...] = jnp.zeros_like(run_ref)

    def weights(z, mask):
        ls, lk = _log_sigmoid_parts(z)
        if mask is not None:
            lk = jnp.where(mask, lk, 0.0)
        hi, lo = _split_bf16(lk)
        cs = jnp.dot(jnp.concatenate([hi, lo], axis=1), u, preferred_element_type=F32)
        run = run_ref[...]
        a = jnp.exp(ls + cs[:, :bk] + run)
        if mask is not None:
            a = jnp.where(mask, a, 0.0)
        run_ref[...] = run + cs[:, bk:]
        return a.astype(BF16)

    pad = jnp.zeros((bk - tq, SB_WIDTH), BF16)
    knew = jnp.concatenate([kn_ref[...], pad], axis=0)
    vnew = jnp.concatenate([vn_ref[...], pad], axis=0)
    skey = lax.broadcasted_iota(jnp.int32, (nr, bk), 1)
    tqry = lax.broadcasted_iota(jnp.int32, (nr, bk), 0) & (tq - 1)
    a = weights(lax.dot_general(qx, knew, _NT, preferred_element_type=F32), skey < tqry)
    acc_ref[...] += jnp.dot(a, vnew, preferred_element_type=F32)

    def live():
        return jnp.max(run_ref[...]) > SB_DEAD

    for j in range(span // bk - 1, -1, -1):
        @pl.when(live())
        def _():
            kt = kt_ref[0, :, j * bk:(j + 1) * bk].astype(BF16)
            vt = vt_ref[0, :, j * bk:(j + 1) * bk].astype(BF16)
            a = weights(jnp.dot(qx, kt, preferred_element_type=F32), None)
            acc_ref[...] += lax.dot_general(a, vt, _NT, preferred_element_type=F32)

    alive_ref[...] = jnp.where(live(), jnp.ones(alive_ref.shape, F32), jnp.zeros(alive_ref.shape, F32))
    accm = jnp.where(same_head, acc_ref[...], 0.0)
    out = accm[0:tq]
    for h in range(1, SB_HEADS):
        out = out + accm[h * tq:(h + 1) * tq]
    o_ref[...] = out


def _sb_sample(qb, kb, vb, kcache, vcache, layer, nb, tq, span):
    past = kcache.shape[2]
    assert past % span == 0 and span % SB_BK == 0
    rowspec = pl.BlockSpec((tq, SB_WIDTH), lambda s: (s, 0))
    cspec = pl.BlockSpec((1, SB_WIDTH, span), lambda s: (layer * nb + s, 0, past // span - 1))
    nr = SB_HEADS * tq
    return pl.pallas_call(
        functools.partial(_sbs_kernel, tq=tq, span=span),
        grid=(nb,),
        in_specs=[rowspec, rowspec, rowspec, cspec, cspec, _const_spec((2 * SB_BK, 2 * SB_BK))],
        out_specs=[rowspec, pl.BlockSpec((1, 1, LANES), lambda s: (s, 0, 0))],
        out_shape=[jax.ShapeDtypeStruct((nb * tq, SB_WIDTH), F32),
                   jax.ShapeDtypeStruct((nb, 1, LANES), F32)],
        scratch_shapes=[pltpu.VMEM((nr, SB_WIDTH), F32), pltpu.VMEM((nr, SB_BK), F32)],
        compiler_params=_params("arbitrary"),
        name="sb_sample",
    )(qb, kb, vb, kcache, vcache, _suffix_matrix())


def _hgrn_consts(c):
    levels = []
    m = 1
    while m < c:
        levels.append(m)
        m *= 2
    t = np.arange(c)[:, None]
    u = np.arange(c)[None, :]
    mats = [(u <= t).astype(np.float32)]
    masks = [np.eye(c, dtype=np.float32)]
    for m in levels:
        end_a = (t // (2 * m)) * 2 * m + m - 1
        in_b = (t % (2 * m)) >= m
        w = np.where(in_b, (u > end_a) & (u <= t), False).astype(np.float32)
        w = w - np.where(~in_b, (u > t) & (u <= end_a), False).astype(np.float32)
        mats.append(w)
        masks.append((((t // (2 * m)) == (u // (2 * m))) & in_b & ((u % (2 * m)) < m)).astype(np.float32))
    w = np.concatenate(mats, axis=0)
    return tuple(levels), jnp.asarray(np.concatenate([w, w], axis=1), dtype=BF16), jnp.asarray(np.stack(masks))


def _hgrn_kernel(hg_ref, gain_ref, s0_ref, w_ref, mk_ref, *refs, c, levels, slab, nchunks):
    o_ref, sfin_ref, st_ref = refs[-3:]
    step = pl.program_id(1)
    hd = HG_HEAD_DIM

    @pl.when(step == 0)
    def _():
        for h in range(HG_HEADS):
            st_ref[h] = s0_ref[0, h].T

    hilo = jnp.concatenate([hg_ref[:, 3 * HG_WIDTH:4 * HG_WIDTH], hg_ref[:, 4 * HG_WIDTH:5 * HG_WIDTH]],
                           axis=0)
    dd = jnp.dot(w_ref[...], hilo, preferred_element_type=F32)

    heads = range(HG_HEADS)
    hsl = [slice(h * hd, (h + 1) * hd) for h in heads]
    col = lambda base, h: hg_ref[:, base * HG_WIDTH + h * hd:base * HG_WIDTH + (h + 1) * hd]
    q, k, v = ([col(b, h) for h in heads] for b in range(3))
    cum = [dd[0:c, hsl[h]] for h in heads]
    st = [st_ref[h] for h in heads]

    o = [lax.dot_general(q[h] * jnp.exp2(cum[h]).astype(BF16), st[h].astype(BF16), _NT,
                         preferred_element_type=F32) for h in heads]
    sc = [lax.dot_general(q[h], k[h], _NT, preferred_element_type=F32) * mk_ref[0] for h in heads]
    for li in range(len(levels)):
        for h in heads:
            e = _exp2_neg_abs(dd[(li + 1) * c:(li + 2) * c, hsl[h]]).astype(BF16)
            sc[h] = sc[h] + lax.dot_general(q[h] * e, k[h] * e, _NT,
                                            preferred_element_type=F32) * mk_ref[li + 1]
    for h in heads:
        last = cum[h][c - 1:c, :]
        kdec = k[h] * jnp.exp2(last - cum[h]).astype(BF16)
        st_ref[h] = st[h] * jnp.exp2(last) + lax.dot_general(v[h], kdec, _TN, preferred_element_type=F32)
    for h in heads:
        o[h] = o[h] + jnp.dot(sc[h].astype(BF16), v[h], preferred_element_type=F32)
    for h in heads:
        o_ref[:, hsl[h]] = (_rms(o[h], gain_ref[:, hsl[h]]) * col(5, h).astype(F32)).astype(BF16)

    @pl.when(step == nchunks - 1)
    def _():
        for s in range(sfin_ref.shape[0]):
            for h in range(HG_HEADS):
                sfin_ref[s, 0, h] = st_ref[h].T if s == slab else jnp.zeros((hd, hd), F32)


def _hgrn(hg, hg_gain, s0, s0_base, layer, nb, t, s_all):
    c = min(LANES, t)
    nchunks = t // c
    levels, wcat, masks = _hgrn_consts(c)
    nl = len(levels) + 1
    sdims = (HG_HEADS, HG_HEAD_DIM, HG_HEAD_DIM)
    in_specs = [
        pl.BlockSpec((c, HG_COLS), lambda b, s: (b * nchunks + s, 0)),
        _const_spec((1, HG_WIDTH)),
        pl.BlockSpec((1,) + sdims, lambda b, s: (s0_base + b, 0, 0, 0)),
        _const_spec((nl * c, 2 * c)),
        _const_spec((nl, c, c)),
    ]
    args = (hg, hg_gain.reshape(1, HG_WIDTH), s0, wcat, masks)
    if s_all is None:
        s_spec, slab, aliases = pl.BlockSpec((DEPTH, 1) + sdims, lambda b, s: (0, b, 0, 0, 0)), layer, {}
    else:
        s_spec, slab, aliases = pl.BlockSpec((1, 1) + sdims, lambda b, s: (layer, b, 0, 0, 0)), 0, {5: 1}
        in_specs.append(pl.BlockSpec(memory_space=pl.ANY))
        args += (s_all,)
    return pl.pallas_call(
        functools.partial(_hgrn_kernel, c=c, levels=levels, slab=slab, nchunks=nchunks),
        grid=(nb, nchunks),
        in_specs=in_specs,
        out_specs=[pl.BlockSpec((c, HG_WIDTH), lambda b, s: (b * nchunks + s, 0)), s_spec],
        out_shape=[
            jax.ShapeDtypeStruct((nb * t, HG_WIDTH), BF16),
            jax.ShapeDtypeStruct((DEPTH, nb) + sdims, F32),
        ],
        input_output_aliases=aliases,
        scratch_shapes=[pltpu.VMEM(sdims, F32)],
        compiler_params=_params("arbitrary", "arbitrary"),
        name="hgrn",
    )(*args)


def _mix_kernel(x_ref, osb_ref, ohg_ref, sbg_ref, wout_ref, nmq_ref, wmq_ref, x1_ref, qm_ref):
    a = _rms(osb_ref[...], sbg_ref[...]).astype(BF16)
    b = ohg_ref[...]
    x1 = (x_ref[...]
          + jnp.dot(a, wout_ref[0:SB_WIDTH, :], preferred_element_type=F32)
          + jnp.dot(b, wout_ref[SB_WIDTH:, :], preferred_element_type=F32))
    x1_ref[...] = x1
    h = _rms(x1, nmq_ref[...]).astype(BF16)
    qm_ref[...] = jnp.dot(h, wmq_ref[...], preferred_element_type=F32).astype(BF16)


def _mix(x2d, osb, ohg, sb_gain, w_out, norm_mem_q, w_mq, tm):
    n = x2d.shape[0]
    row = lambda w: pl.BlockSpec((tm, w), lambda i: (i, 0))
    return pl.pallas_call(
        _mix_kernel,
        grid=(n // tm,),
        in_specs=[row(D_MODEL), row(SB_WIDTH), row(HG_WIDTH), _const_spec((1, SB_WIDTH)),
                  _const_spec((D_MODEL, D_MODEL)), _const_spec((1, D_MODEL)),
                  _const_spec((D_MODEL, MEM_WIDTH))],
        out_specs=[row(D_MODEL), row(MEM_WIDTH)],
        out_shape=[jax.ShapeDtypeStruct((n, D_MODEL), F32), jax.ShapeDtypeStruct((n, MEM_WIDTH), BF16)],
        compiler_params=_params("arbitrary"),
        name="mix",
    )(x2d, osb, ohg, sb_gain.reshape(1, SB_WIDTH), w_out, norm_mem_q.reshape(1, D_MODEL), w_mq)


def _memattn_kernel(q_ref, mk_ref, mv_ref, o_ref):
    hd = MEM_HEAD_DIM
    mk = mk_ref[0].reshape(N_MEM, MEM_WIDTH).astype(BF16)
    mv = mv_ref[0].reshape(N_MEM, MEM_WIDTH).astype(BF16)
    for h in range(MEM_HEADS):
        sl = slice(h * hd, (h + 1) * hd)
        s = lax.dot_general(q_ref[:, sl], mk[:, sl], _NT, preferred_element_type=F32) * (hd ** -0.5)
        p = jnp.exp(s - jnp.max(s, axis=-1, keepdims=True))
        den = jnp.sum(p, axis=-1, keepdims=True)
        o = jnp.dot(p.astype(BF16), mv[:, sl], preferred_element_type=F32)
        o_ref[:, sl] = (o / den).astype(BF16)


def _memattn(qm, mk3, mv3, base, nb, t, tm):
    nt = t // tm
    mspec = pl.BlockSpec((1, N_MEM, MEM_HEADS, MEM_HEAD_DIM), lambda b, i: (base + b, 0, 0, 0))
    return pl.pallas_call(
        _memattn_kernel,
        grid=(nb, nt),
        in_specs=[pl.BlockSpec((tm, MEM_WIDTH), lambda b, i: (b * nt + i, 0)), mspec, mspec],
        out_specs=pl.BlockSpec((tm, MEM_WIDTH), lambda b, i: (b * nt + i, 0)),
        out_shape=jax.ShapeDtypeStruct((nb * t, MEM_WIDTH), BF16),
        compiler_params=_params("arbitrary", "arbitrary"),
        name="memattn",
    )(qm, mk3, mv3)


FF_CHUNK = 1024


def _ffn_kernel(x1_ref, om_ref, wmo_ref, nf_ref, w1_ref, w2_ref, nfin_ref, o_ref, *, final):
    x2 = x1_ref[...] + jnp.dot(om_ref[...], wmo_ref[...], preferred_element_type=F32)
    h = _rms(x2, nf_ref[...]).astype(BF16)
    x3 = x2
    for c in range(D_FF // FF_CHUNK):
        cs = slice(c * FF_CHUNK, (c + 1) * FF_CHUNK)
        r = jnp.maximum(jnp.dot(h, w1_ref[:, cs], preferred_element_type=F32), 0.0)
        x3 = x3 + jnp.dot((r * r).astype(BF16), w2_ref[cs, :], preferred_element_type=F32)
    o_ref[...] = _rms(x3, nfin_ref[...]) if final else x3


def _ffn(x1, om, w_mo, norm_ffn, w1, w2, norm_final, final, tm):
    n = x1.shape[0]
    row = lambda w: pl.BlockSpec((tm, w), lambda i: (i, 0))
    return pl.pallas_call(
        functools.partial(_ffn_kernel, final=final),
        grid=(n // tm,),
        in_specs=[row(D_MODEL), row(MEM_WIDTH), _const_spec((MEM_WIDTH, D_MODEL)),
                  _const_spec((1, D_MODEL)), _const_spec((D_MODEL, D_FF)), _const_spec((D_FF, D_MODEL)),
                  _const_spec((1, D_MODEL))],
        out_specs=row(D_MODEL),
        out_shape=jax.ShapeDtypeStruct((n, D_MODEL), F32),
        compiler_params=_params("arbitrary"),
        name="ffn",
    )(x1, om, w_mo, norm_ffn.reshape(1, D_MODEL), w1, w2, norm_final.reshape(1, D_MODEL))


TM = 512
SB_BQ = 256
SBS_NEAR = 256


def kernel(x_prompt, x_sample, mem_prompt, cache_sb_k, cache_sb_v, state_hgrn, cache_mem_k, cache_mem_v,
           lb_logits, norm_mix, w_in, sb_gain, hg_gain, w_out, norm_mem_q, norm_mem_kv, w_mq, w_mk, w_mv,
           w_mo, norm_ffn, w_ffn1, w_ffn2, norm_final):
    nbp, tp, _ = x_prompt.shape
    nbs, ts, _ = x_sample.shape
    past = cache_sb_k.shape[2]
    w_in_b, w_out_b, w_mq_b, w_mk_b, w_mv_b, w_mo_b, w1_b, w2_b = (
        w.astype(BF16) for w in (w_in, w_out, w_mq, w_mk, w_mv, w_mo, w_ffn1, w_ffn2))
    lbl = lb_logits.astype(F32)

    mk_p, mv_p = _memkv(mem_prompt.reshape(nbp * N_MEM, D_MODEL), norm_mem_kv, w_mk_b, w_mv_b)
    mdims = (N_MEM, MEM_HEADS, MEM_HEAD_DIM)
    mk_p3 = mk_p.reshape((DEPTH * nbp,) + mdims)
    mv_p3 = mv_p.reshape((DEPTH * nbp,) + mdims)
    mk_s3 = cache_mem_k.reshape((DEPTH * nbs,) + mdims)
    mv_s3 = cache_mem_v.reshape((DEPTH * nbs,) + mdims)
    s0_s = state_hgrn.reshape(DEPTH * nbs, HG_HEADS, HG_HEAD_DIM, HG_HEAD_DIM)
    s0_p = jnp.zeros((nbp, HG_HEADS, HG_HEAD_DIM, HG_HEAD_DIM), F32)
    near = min(SBS_NEAR, past)
    kc4 = cache_sb_k.transpose(0, 1, 3, 4, 2).reshape(DEPTH * nbs, SB_WIDTH, past)
    vc4 = cache_sb_v.transpose(0, 1, 3, 4, 2).reshape(DEPTH * nbs, SB_WIDTH, past)

    def sb_sample(l, qb, kb, vb, nb, t):
        o_near, alive = _sb_sample(qb, kb, vb, kc4, vc4, l, nb, t, near)
        if near == past:
            return o_near
        return lax.cond(jnp.max(alive) > 0.0,
                        lambda: _sb_sample(qb, kb, vb, kc4, vc4, l, nb, t, past)[0],
                        lambda: o_near)

    def layer(l, x2d, nb, t, prompt, carried):
        kv_all, s_all = carried
        qb, k_all, v_all, kb, vb, hg = _proj(x2d, norm_mix[l], w_in_b[l], lbl, TM, l, kv_all, nb, t)
        if prompt:
            osb = _sb_prompt(qb, kb, vb, nb, t, SB_BQ)
            ohg, s_all = _hgrn(hg, hg_gain[l], s0_p, 0, l, nb, t, s_all)
            mk3, mv3, base = mk_p3, mv_p3, l * nb
        else:
            osb = sb_sample(l, qb, kb, vb, nb, t)
            ohg, s_all = _hgrn(hg, hg_gain[l], s0_s, l * nb, l, nb, t, s_all)
            mk3, mv3, base = mk_s3, mv_s3, l * nb
        x1, qm = _mix(x2d, osb, ohg, sb_gain[l], w_out_b[l], norm_mem_q[l], w_mq_b[l], TM)
        om = _memattn(qm, mk3, mv3, base, nb, t, min(TM, t))
        xo = _ffn(x1, om, w_mo_b[l], norm_ffn[l], w1_b[l], w2_b[l], norm_final, l == DEPTH - 1, TM)
        return xo, ((k_all, v_all), s_all)

    xp = x_prompt.reshape(nbp * tp, D_MODEL)
    xs = x_sample.reshape(nbs * ts, D_MODEL)
    out_p = out_s = (None, None)
    for l in range(DEPTH):
        xp, out_p = layer(l, xp, nbp, tp, True, out_p)
        xs, out_s = layer(l, xs, nbs, ts, False, out_s)
    (kp, vp), sp = out_p
    (kn, vn), sn = out_s

    def heads_last(a):
        return a.reshape(DEPTH, nbp, SB_HEADS, SB_HEAD_DIM, tp).transpose(0, 1, 4, 2, 3)

    sbs = (DEPTH, nbs, ts, SB_HEADS, SB_HEAD_DIM)
    memp = (DEPTH, nbp, N_MEM, MEM_HEADS, MEM_HEAD_DIM)
    return (xp.reshape(nbp, tp, D_MODEL), xs.reshape(nbs, ts, D_MODEL),
            heads_last(kp), heads_last(vp), sp, mk_p3.reshape(memp), mv_p3.reshape(memp),
            kn.reshape(sbs), vn.reshape(sbs), sn)
```

```python
import functools

import jax
import jax.numpy as jnp
import numpy as np
from jax import lax
from jax.experimental import pallas as pl
from jax.experimental.pallas import tpu as pltpu

D_MODEL = 1024
DEPTH = 2
SB_HEADS = 8
SB_HEAD_DIM = 64
SB_WIDTH = SB_HEADS * SB_HEAD_DIM
HG_HEADS = 4
HG_HEAD_DIM = 128
HG_WIDTH = HG_HEADS * HG_HEAD_DIM
IN_COLS = 3 * SB_WIDTH + 4 * HG_WIDTH
HG_COLS = 6 * HG_WIDTH
N_MEM = 256
MEM_HEADS = 4
MEM_HEAD_DIM = 128
MEM_WIDTH = MEM_HEADS * MEM_HEAD_DIM
D_FF = 4 * D_MODEL
EPS = 1e-6

LANES = 128
VMEM_LIMIT = 56 * 1024 * 1024
F32 = jnp.float32
BF16 = jnp.bfloat16

_NT = (((1,), (1,)), ((), ()))
_TN = (((0,), (0,)), ((), ()))


def _params(*sem):
    return pltpu.CompilerParams(dimension_semantics=sem, vmem_limit_bytes=VMEM_LIMIT)


def _const_spec(shape):
    nd = len(shape)
    return pl.BlockSpec(shape, lambda *_: (0,) * nd, pipeline_mode=pl.Buffered(1))


def _log2(n):
    assert n > 0 and n & (n - 1) == 0, n
    return n.bit_length() - 1


def _rms(x, gain):
    ms = jnp.mean(x * x, axis=-1, keepdims=True)
    return x * lax.rsqrt(ms + EPS) * gain


def _log_sigmoid_parts(z):
    l = jnp.log(1.0 + jnp.exp(-jnp.abs(z)))
    ls = jnp.minimum(z, 0.0) - l
    return ls, ls - z


LOG2E = 1.4426950408889634


def _exp2_neg_abs(x):
    return jnp.exp2(-jnp.abs(x))


def _split_bf16(x):
    hi = x.astype(BF16)
    lo = (x - hi.astype(F32)).astype(BF16)
    return hi, lo


def _memkv_kernel(mem_ref, g_ref, wk_ref, wv_ref, mk_ref, mv_ref):
    h = _rms(mem_ref[...], g_ref[0]).astype(BF16)
    mk_ref[0] = jnp.dot(h, wk_ref[0], preferred_element_type=F32)
    mv_ref[0] = jnp.dot(h, wv_ref[0], preferred_element_type=F32)


def _memkv(mem2d, gains, wk, wv):
    n = mem2d.shape[0]
    out = jax.ShapeDtypeStruct((DEPTH, n, MEM_WIDTH), F32)
    return pl.pallas_call(
        _memkv_kernel,
        grid=(DEPTH,),
        in_specs=[
            pl.BlockSpec((n, D_MODEL), lambda l: (0, 0)),
            pl.BlockSpec((1, 1, D_MODEL), lambda l: (l, 0, 0)),
            pl.BlockSpec((1, D_MODEL, MEM_WIDTH), lambda l: (l, 0, 0)),
            pl.BlockSpec((1, D_MODEL, MEM_WIDTH), lambda l: (l, 0, 0)),
        ],
        out_specs=[pl.BlockSpec((1, n, MEM_WIDTH), lambda l: (l, 0, 0))] * 2,
        out_shape=[out, out],
        compiler_params=_params("arbitrary"),
        name="memkv",
    )(mem2d, gains.reshape(DEPTH, 1, D_MODEL), wk, wv)


def _proj_kernel(x_ref, g_ref, w_ref, lbl_ref, *refs, layer, slab, time_minor):
    qsb_ref, ksb_ref, vsb_ref, kbf_ref, vbf_ref, hg_ref = refs[-6:]
    h = _rms(x_ref[...], g_ref[...]).astype(BF16)

    def col(c):
        return jnp.dot(h, w_ref[:, c * SB_WIDTH:(c + 1) * SB_WIDTH], preferred_element_type=F32)

    def put(ref, val):
        out = val.T if time_minor else val.reshape(val.shape[0], SB_HEADS, SB_HEAD_DIM)
        for s in range(ref.shape[0]):
            ref[s, 0] = out if s == slab else jnp.zeros_like(out)

    def put_hg(c, val):
        hg_ref[:, c * HG_WIDTH:(c + 1) * HG_WIDTH] = val.astype(BF16)

    lg = lbl_ref[...]
    ex = jnp.exp(lg - jnp.max(lg, axis=0, keepdims=True))
    sm = ex / jnp.sum(ex, axis=0, keepdims=True)
    csum = sm[0:1]
    for l in range(1, layer + 1):
        csum = csum + sm[l:l + 1]
    lb = csum - sm[0:1]

    zf = col(4)
    ez = jnp.exp(-jnp.abs(zf))
    lsig = jnp.minimum(zf, 0.0) - jnp.log(1.0 + ez)
    bterm = jnp.log(1.0 - lb) + lsig
    la = jnp.log(jnp.maximum(lb, 1e-30))
    logf_mix = jnp.maximum(la, bterm) + jnp.log(1.0 + jnp.exp(-jnp.abs(la - bterm)))
    logf = jnp.where(lb > 0.0, logf_mix, bterm)
    put_hg(1, (1.0 - lb) * (jnp.where(zf >= 0.0, ez, 1.0) / (1.0 + ez)))
    hi, lo = _split_bf16(logf * LOG2E)
    put_hg(3, hi)
    put_hg(4, lo)
    g = col(6)
    eg = jnp.exp(-jnp.abs(g))
    put_hg(5, g * (jnp.where(g >= 0.0, 1.0, eg) / (1.0 + eg)))
    put_hg(0, col(3) * (HG_HEAD_DIM ** -0.5))
    put_hg(2, col(5))

    qsb_ref[...] = (col(0) * (SB_HEAD_DIM ** -0.5)).astype(BF16)
    k = col(1)
    put(ksb_ref, k)
    kbf_ref[...] = k.astype(BF16)
    v = col(2)
    put(vsb_ref, v)
    vbf_ref[...] = v.astype(BF16)


def _proj(x2d, gain, w_in, lb_logits, tm, layer, kv_all, nb, t):
    n = x2d.shape[0]
    row = lambda w: pl.BlockSpec((tm, w), lambda i: (i, 0))
    in_specs = [row(D_MODEL), _const_spec((1, D_MODEL)), _const_spec((D_MODEL, IN_COLS)),
                _const_spec((DEPTH, HG_WIDTH))]
    args = (x2d, gain.reshape(1, D_MODEL), w_in, lb_logits)
    time_minor = t % tm == 0
    if time_minor:
        nq = t // tm
        kv_shape, blk = (DEPTH, nb, SB_WIDTH, t), (1, SB_WIDTH, tm)
        where = lambda i: (i // nq, 0, i % nq)
    else:
        kv_shape, blk = (DEPTH, n // tm, tm, SB_HEADS, SB_HEAD_DIM), (1, tm, SB_HEADS, SB_HEAD_DIM)
        where = lambda i: (i, 0, 0, 0)
    if kv_all is None:
        kv_spec, slab, aliases = pl.BlockSpec((DEPTH,) + blk, lambda i: (0,) + where(i)), layer, {}
    else:
        kv_spec, slab, aliases = pl.BlockSpec((1,) + blk, lambda i: (layer,) + where(i)), 0, {4: 1, 5: 2}
        in_specs += [pl.BlockSpec(memory_space=pl.ANY)] * 2
        args += tuple(kv_all)
    return pl.pallas_call(
        functools.partial(_proj_kernel, layer=layer, slab=slab, time_minor=time_minor),
        grid=(n // tm,),
        in_specs=in_specs,
        out_specs=[row(SB_WIDTH), kv_spec, kv_spec, row(SB_WIDTH), row(SB_WIDTH), row(HG_COLS)],
        out_shape=[
            jax.ShapeDtypeStruct((n, SB_WIDTH), BF16),
            jax.ShapeDtypeStruct(kv_shape, F32),
            jax.ShapeDtypeStruct(kv_shape, F32),
            jax.ShapeDtypeStruct((n, SB_WIDTH), BF16),
            jax.ShapeDtypeStruct((n, SB_WIDTH), BF16),
            jax.ShapeDtypeStruct((n, HG_COLS), BF16),
        ],
        input_output_aliases=aliases,
        compiler_params=_params("arbitrary"),
        name="proj",
    )(*args)


SB_BK = LANES
SB_DEAD = -104.0
SB_NPAIR = 2


def _suffix_matrix():
    j = np.arange(SB_BK)[:, None]
    s = np.arange(SB_BK)[None, :]
    u = np.concatenate([(j > s).astype(np.float32), np.ones((SB_BK, SB_BK), np.float32)], axis=1)
    return jnp.asarray(np.concatenate([u, u], axis=0), dtype=BF16)


def _sbp_kernel(q_ref, k_ref, v_ref, u_ref, o_ref, acc_ref, run_ref, *, bq, npair):
    bk = SB_BK
    qi = pl.program_id(2)
    lane = lax.broadcasted_iota(jnp.int32, (1, LANES), 1)
    m_a = (lane < SB_HEAD_DIM).astype(BF16)
    m_b = (lane >= SB_HEAD_DIM).astype(BF16)
    acc_ref[...] = jnp.zeros_like(acc_ref)
    run_ref[...] = jnp.zeros_like(run_ref)
    u = u_ref[...]
    nd = bq // bk
    assert nd % 2 == 0, "earlier key blocks are taken two per trip"

    def sweep(blocks):
        units = [(j, rel, p) for j, rel in blocks for p in range(npair)]
        ps = lambda p: slice(p * LANES, (p + 1) * LANES)
        keys = lambda j: pl.ds(pl.multiple_of(j * bk, bk), bk)
        qrows = lambda rel: slice(0 if rel is None else rel, bq)
        masks = {rel: (lax.broadcasted_iota(jnp.int32, (bq - rel, bk), 1)
                       < lax.broadcasted_iota(jnp.int32, (bq - rel, bk), 0))
                 for _, rel in blocks if rel is not None}
        zs = []
        for j, rel, p in units:
            ks = k_ref[keys(j), ps(p)]
            kcat = jnp.concatenate([ks * m_a, ks * m_b], axis=0)
            zs.append(lax.dot_general(q_ref[qrows(rel), ps(p)], kcat, _NT, preferred_element_type=F32))
        lss, css = [], []
        for (j, rel, p), z in zip(units, zs):
            for hh in range(2):
                ls, lk = _log_sigmoid_parts(z[:, hh * bk:(hh + 1) * bk])
                if rel is not None:
                    lk = jnp.where(masks[rel], lk, 0.0)
                hi, lo = _split_bf16(lk)
                lss.append(ls)
                css.append(jnp.dot(jnp.concatenate([hi, lo], axis=1), u, preferred_element_type=F32))
        for i, (j, rel, p) in enumerate(units):
            a_parts = []
            for hh in range(2):
                ls, cs = lss[2 * i + hh], css[2 * i + hh]
                run = run_ref[2 * p + hh, qrows(rel)]
                a = jnp.exp(ls + cs[:, :bk] + run)
                if rel is not None:
                    a = jnp.where(masks[rel], a, 0.0)
                run_ref[2 * p + hh, qrows(rel)] = run + cs[:, bk:]
                a_parts.append(a.astype(BF16))
            vs = v_ref[keys(j), ps(p)]
            vcat = jnp.concatenate([vs * m_a, vs * m_b], axis=0)
            acc_ref[qrows(rel), ps(p)] += jnp.dot(jnp.concatenate(a_parts, axis=1), vcat,
                                                  preferred_element_type=F32)

    sweep([(qi * nd + d, d * bk) for d in range(nd - 1, -1, -1)])

    def live():
        return jnp.max(run_ref[...]) > SB_DEAD

    def cond(state):
        j, alive = state
        return jnp.logical_and(j >= 0, alive)

    def body(state):
        j, _ = state
        sweep([(j, None), (j - 1, None)])
        return j - 2, live()

    lax.while_loop(cond, body, (qi * nd - 1, live()))
    o_ref[...] = acc_ref[...]


def _sb_prompt(qb, kb, vb, nb, t, bq):
    nq = t // bq
    w = SB_NPAIR * LANES
    return pl.pallas_call(
        functools.partial(_sbp_kernel, bq=bq, npair=SB_NPAIR),
        grid=(nb, SB_WIDTH // w, nq),
        in_specs=[
            pl.BlockSpec((bq, w), lambda b, h, i: (b * nq + i, h)),
            pl.BlockSpec((t, w), lambda b, h, i: (b, h)),
            pl.BlockSpec((t, w), lambda b, h, i: (b, h)),
            _const_spec((2 * SB_BK, 2 * SB_BK)),
        ],
        out_specs=pl.BlockSpec((bq, w), lambda b, h, i: (b * nq + i, h)),
        out_shape=jax.ShapeDtypeStruct((nb * t, SB_WIDTH), F32),
        scratch_shapes=[pltpu.VMEM((bq, w), F32), pltpu.VMEM((2 * SB_NPAIR, bq, LANES), F32)],
        compiler_params=_params("arbitrary", "arbitrary", "arbitrary"),
        name="sb_prompt",
    )(qb, kb, vb, _suffix_matrix())


def _sbs_kernel(q_ref, kn_ref, vn_ref, kt_ref, vt_ref, u_ref, o_ref, alive_ref, acc_ref, run_ref,
                *, tq, span, spb):
    bk = SB_BK
    nr = SB_HEADS * tq
    streams = range(spb)
    srows = lambda s: slice(s * tq, (s + 1) * tq)
    row = lax.broadcasted_iota(jnp.int32, (nr, SB_WIDTH), 0)
    colw = lax.broadcasted_iota(jnp.int32, (nr, SB_WIDTH), 1)
    same_head = (row >> _log2(tq)) == (colw >> _log2(SB_HEAD_DIM))
    qx = [jnp.where(same_head, jnp.concatenate([q_ref[srows(s)]] * SB_HEADS, axis=0), jnp.zeros((), BF16))
          for s in streams]
    u = u_ref[...]
    acc_ref[...] = jnp.zeros_like(acc_ref)
    run_ref[...] = jnp.zeros_like(run_ref)

    def weights(zs, mask):
        parts = [_log_sigmoid_parts(z) for z in zs]
        css = []
        for ls, lk in parts:
            if mask is not None:
                lk = jnp.where(mask, lk, 0.0)
            hi, lo = _split_bf16(lk)
            css.append(jnp.dot(jnp.concatenate([hi, lo], axis=1), u, preferred_element_type=F32))
        out = []
        for s, ((ls, _), cs) in enumerate(zip(parts, css)):
            run = run_ref[s]
            a = jnp.exp(ls + cs[:, :bk] + run)
            if mask is not None:
                a = jnp.where(mask, a, 0.0)
            run_ref[s] = run + cs[:, bk:]
            out.append(a.astype(BF16))
        return out

    pad = jnp.zeros((bk - tq, SB_WIDTH), BF16)
    skey = lax.broadcasted_iota(jnp.int32, (nr, bk), 1)
    tqry = lax.broadcasted_iota(jnp.int32, (nr, bk), 0) & (tq - 1)
    a = weights([lax.dot_general(qx[s], jnp.concatenate([kn_ref[srows(s)], pad], axis=0), _NT,
                                 preferred_element_type=F32) for s in streams], skey < tqry)
    for s in streams:
        acc_ref[s] += jnp.dot(a[s], jnp.concatenate([vn_ref[srows(s)], pad], axis=0),
                              preferred_element_type=F32)

    def live():
        return jnp.max(run_ref[...]) > SB_DEAD

    for j in range(span // bk - 1, -1, -1):
        @pl.when(live())
        def _():
            cols = slice(j * bk, (j + 1) * bk)
            a = weights([jnp.dot(qx[s], kt_ref[s, :, cols].astype(BF16), preferred_element_type=F32)
                         for s in streams], None)
            for s in streams:
                acc_ref[s] += lax.dot_general(a[s], vt_ref[s, :, cols].astype(BF16), _NT,
                                              preferred_element_type=F32)

    alive_ref[...] = jnp.where(live(), jnp.ones(alive_ref.shape, F32), jnp.zeros(alive_ref.shape, F32))
    for s in streams:
        accm = jnp.where(same_head, acc_ref[s], 0.0)
        out = accm[0:tq]
        for h in range(1, SB_HEADS):
            out = out + accm[h * tq:(h + 1) * tq]
        o_ref[srows(s)] = out


def _sb_sample(qb, kb, vb, kcache, vcache, layer, nb, tq, span):
    past = kcache.shape[2]
    assert past % span == 0 and span % SB_BK == 0
    spb = SBS_SPB if nb % SBS_SPB == 0 and span <= SBS_NEAR else 1
    rowspec = pl.BlockSpec((spb * tq, SB_WIDTH), lambda s: (s, 0))
    cspec = pl.BlockSpec((spb, SB_WIDTH, span), lambda s: (layer * nb // spb + s, 0, past // span - 1))
    nr = SB_HEADS * tq
    return pl.pallas_call(
        functools.partial(_sbs_kernel, tq=tq, span=span, spb=spb),
        grid=(nb // spb,),
        in_specs=[rowspec, rowspec, rowspec, cspec, cspec, _const_spec((2 * SB_BK, 2 * SB_BK))],
        out_specs=[rowspec, pl.BlockSpec((spb, 1, LANES), lambda s: (s, 0, 0))],
        out_shape=[jax.ShapeDtypeStruct((nb * tq, SB_WIDTH), F32),
                   jax.ShapeDtypeStruct((nb, 1, LANES), F32)],
        scratch_shapes=[pltpu.VMEM((spb, nr, SB_WIDTH), F32), pltpu.VMEM((spb, nr, SB_BK), F32)],
        compiler_params=_params("arbitrary"),
        name="sb_sample",
    )(qb, kb, vb, kcache, vcache, _suffix_matrix())


def _hgrn_consts(c):
    levels = []
    m = 1
    while m < c:
        levels.append(m)
        m *= 2
    t = np.arange(c)[:, None]
    u = np.arange(c)[None, :]
    mats = [(u <= t).astype(np.float32)]
    masks = [np.eye(c, dtype=np.float32)]
    for m in levels:
        end_a = (t // (2 * m)) * 2 * m + m - 1
        in_b = (t % (2 * m)) >= m
        w = np.where(in_b, (u > end_a) & (u <= t), False).astype(np.float32)
        w = w - np.where(~in_b, (u > t) & (u <= end_a), False).astype(np.float32)
        mats.append(w)
        masks.append((((t // (2 * m)) == (u // (2 * m))) & in_b & ((u % (2 * m)) < m)).astype(np.float32))
    w = np.concatenate(mats, axis=0)
    return tuple(levels), jnp.asarray(np.concatenate([w, w], axis=1), dtype=BF16), jnp.asarray(np.stack(masks))


def _hgrn_kernel(hg_ref, gain_ref, s0_ref, w_ref, mk_ref, *refs, c, levels, slab, nchunks, spb):
    o_ref, sfin_ref, st_ref = refs[-3:]
    step = pl.program_id(1)
    hd = HG_HEAD_DIM
    units = [(s, h) for s in range(spb) for h in range(HG_HEADS)]
    nu = range(len(units))

    @pl.when(step == 0)
    def _():
        for i, (s, h) in enumerate(units):
            st_ref[i] = s0_ref[s, h].T

    rows = lambda s: slice(s * c, (s + 1) * c)
    col = lambda base, s, h: hg_ref[rows(s), base * HG_WIDTH + h * hd:base * HG_WIDTH + (h + 1) * hd]
    w = w_ref[...]
    dd = [jnp.dot(w, jnp.concatenate([hg_ref[rows(s), 3 * HG_WIDTH:4 * HG_WIDTH],
                                      hg_ref[rows(s), 4 * HG_WIDTH:5 * HG_WIDTH]], axis=0),
                  preferred_element_type=F32) for s in range(spb)]
    hsl = lambda h: slice(h * hd, (h + 1) * hd)
    q, k, v = ([col(b, s, h) for s, h in units] for b in range(3))
    cum = [dd[s][0:c, hsl(h)] for s, h in units]
    st = [st_ref[i] for i in nu]

    o = [lax.dot_general(q[i] * jnp.exp2(cum[i]).astype(BF16), st[i].astype(BF16), _NT,
                         preferred_element_type=F32) for i in nu]
    sc = [lax.dot_general(q[i], k[i], _NT, preferred_element_type=F32) * mk_ref[0] for i in nu]
    for li in range(len(levels)):
        for i, (s, h) in enumerate(units):
            e = _exp2_neg_abs(dd[s][(li + 1) * c:(li + 2) * c, hsl(h)]).astype(BF16)
            sc[i] = sc[i] + lax.dot_general(q[i] * e, k[i] * e, _NT,
                                            preferred_element_type=F32) * mk_ref[li + 1]
    for i in nu:
        last = cum[i][c - 1:c, :]
        kdec = k[i] * jnp.exp2(last - cum[i]).astype(BF16)
        st_ref[i] = st[i] * jnp.exp2(last) + lax.dot_general(v[i], kdec, _TN, preferred_element_type=F32)
    for i in nu:
        o[i] = o[i] + jnp.dot(sc[i].astype(BF16), v[i], preferred_element_type=F32)
    for i, (s, h) in enumerate(units):
        o_ref[rows(s), hsl(h)] = (_rms(o[i], gain_ref[:, hsl(h)]) * col(5, s, h).astype(F32)).astype(BF16)

    @pl.when(step == nchunks - 1)
    def _():
        for d in range(sfin_ref.shape[0]):
            for i, (s, h) in enumerate(units):
                sfin_ref[d, s, h] = st_ref[i].T if d == slab else jnp.zeros((hd, hd), F32)


def _hgrn(hg, hg_gain, s0, s0_base, layer, nb, t, s_all):
    c = min(LANES, t)
    nchunks = t // c
    levels, wcat, masks = _hgrn_consts(c)
    nl = len(levels) + 1
    sdims = (HG_HEADS, HG_HEAD_DIM, HG_HEAD_DIM)
    spb = HG_SPB if nchunks == 1 and nb % HG_SPB == 0 and s0_base % HG_SPB == 0 else 1
    in_specs = [
        pl.BlockSpec((spb * c, HG_COLS), lambda b, s: (b * nchunks + s, 0)),
        _const_spec((1, HG_WIDTH)),
        pl.BlockSpec((spb,) + sdims, lambda b, s: (s0_base // spb + b, 0, 0, 0)),
        _const_spec((nl * c, 2 * c)),
        _const_spec((nl, c, c)),
    ]
    args = (hg, hg_gain.reshape(1, HG_WIDTH), s0, wcat, masks)
    if s_all is None:
        s_spec, slab, aliases = pl.BlockSpec((DEPTH, spb) + sdims, lambda b, s: (0, b, 0, 0, 0)), layer, {}
    else:
        s_spec, slab, aliases = pl.BlockSpec((1, spb) + sdims, lambda b, s: (layer, b, 0, 0, 0)), 0, {5: 1}
        in_specs.append(pl.BlockSpec(memory_space=pl.ANY))
        args += (s_all,)
    return pl.pallas_call(
        functools.partial(_hgrn_kernel, c=c, levels=levels, slab=slab, nchunks=nchunks, spb=spb),
        grid=(nb // spb, nchunks),
        in_specs=in_specs,
        out_specs=[pl.BlockSpec((spb * c, HG_WIDTH), lambda b, s: (b * nchunks + s, 0)), s_spec],
        out_shape=[
            jax.ShapeDtypeStruct((nb * t, HG_WIDTH), BF16),
            jax.ShapeDtypeStruct((DEPTH, nb) + sdims, F32),
        ],
        input_output_aliases=aliases,
        scratch_shapes=[pltpu.VMEM((spb * HG_HEADS, HG_HEAD_DIM, HG_HEAD_DIM), F32)],
        compiler_params=_params("arbitrary", "arbitrary"),
        name="hgrn",
    )(*args)


def _mix_kernel(x_ref, osb_ref, ohg_ref, sbg_ref, wout_ref, nmq_ref, wmq_ref, x1_ref, qm_ref):
    a = _rms(osb_ref[...], sbg_ref[...]).astype(BF16)
    b = ohg_ref[...]
    x1 = (x_ref[...]
          + jnp.dot(a, wout_ref[0:SB_WIDTH, :], preferred_element_type=F32)
          + jnp.dot(b, wout_ref[SB_WIDTH:, :], preferred_element_type=F32))
    x1_ref[...] = x1
    h = _rms(x1, nmq_ref[...]).astype(BF16)
    qm_ref[...] = jnp.dot(h, wmq_ref[...], preferred_element_type=F32).astype(BF16)


def _mix(x2d, osb, ohg, sb_gain, w_out, norm_mem_q, w_mq, tm):
    n = x2d.shape[0]
    row = lambda w: pl.BlockSpec((tm, w), lambda i: (i, 0))
    return pl.pallas_call(
        _mix_kernel,
        grid=(n // tm,),
        in_specs=[row(D_MODEL), row(SB_WIDTH), row(HG_WIDTH), _const_spec((1, SB_WIDTH)),
                  _const_spec((D_MODEL, D_MODEL)), _const_spec((1, D_MODEL)),
                  _const_spec((D_MODEL, MEM_WIDTH))],
        out_specs=[row(D_MODEL), row(MEM_WIDTH)],
        out_shape=[jax.ShapeDtypeStruct((n, D_MODEL), F32), jax.ShapeDtypeStruct((n, MEM_WIDTH), BF16)],
        compiler_params=_params("arbitrary"),
        name="mix",
    )(x2d, osb, ohg, sb_gain.reshape(1, SB_WIDTH), w_out, norm_mem_q.reshape(1, D_MODEL), w_mq)


def _memattn_kernel(q_ref, mk_ref, mv_ref, o_ref, *, spb, tm):
    hd = MEM_HEAD_DIM
    mk = [mk_ref[s].reshape(N_MEM, MEM_WIDTH).astype(BF16) for s in range(spb)]
    mv = [mv_ref[s].reshape(N_MEM, MEM_WIDTH).astype(BF16) for s in range(spb)]
    units = [(s, h) for s in range(spb) for h in range(MEM_HEADS)]
    rows = lambda s: slice(s * tm, (s + 1) * tm)
    sl = lambda h: slice(h * hd, (h + 1) * hd)
    sc = [lax.dot_general(q_ref[rows(s), sl(h)], mk[s][:, sl(h)], _NT,
                          preferred_element_type=F32) * (hd ** -0.5) for s, h in units]
    p = [jnp.exp(x - jnp.max(x, axis=-1, keepdims=True)) for x in sc]
    den = [jnp.sum(x, axis=-1, keepdims=True) for x in p]
    o = [jnp.dot(x.astype(BF16), mv[s][:, sl(h)], preferred_element_type=F32) for x, (s, h) in zip(p, units)]
    for x, d, (s, h) in zip(o, den, units):
        o_ref[rows(s), sl(h)] = (x / d).astype(BF16)


def _memattn(qm, mk3, mv3, base, nb, t, tm):
    nt = t // tm
    spb = MEM_SPB if nt == 1 and nb % MEM_SPB == 0 and base % MEM_SPB == 0 else 1
    mspec = pl.BlockSpec((spb, N_MEM, MEM_HEADS, MEM_HEAD_DIM), lambda b, i: (base // spb + b, 0, 0, 0))
    qspec = pl.BlockSpec((spb * tm, MEM_WIDTH), lambda b, i: (b * nt + i, 0))
    return pl.pallas_call(
        functools.partial(_memattn_kernel, spb=spb, tm=tm),
        grid=(nb // spb, nt),
        in_specs=[qspec, mspec, mspec],
        out_specs=qspec,
        out_shape=jax.ShapeDtypeStruct((nb * t, MEM_WIDTH), BF16),
        compiler_params=_params("arbitrary", "arbitrary"),
        name="memattn",
    )(qm, mk3, mv3)


FF_CHUNK = 1024


def _ffn_kernel(x1_ref, om_ref, wmo_ref, nf_ref, w1_ref, w2_ref, nfin_ref, o_ref, *, final):
    x2 = x1_ref[...] + jnp.dot(om_ref[...], wmo_ref[...], preferred_element_type=F32)
    h = _rms(x2, nf_ref[...]).astype(BF16)
    x3 = x2
    for c in range(D_FF // FF_CHUNK):
        cs = slice(c * FF_CHUNK, (c + 1) * FF_CHUNK)
        r = jnp.maximum(jnp.dot(h, w1_ref[:, cs], preferred_element_type=F32), 0.0)
        x3 = x3 + jnp.dot((r * r).astype(BF16), w2_ref[cs, :], preferred_element_type=F32)
    o_ref[...] = _rms(x3, nfin_ref[...]) if final else x3


def _ffn(x1, om, w_mo, norm_ffn, w1, w2, norm_final, final, tm):
    n = x1.shape[0]
    row = lambda w: pl.BlockSpec((tm, w), lambda i: (i, 0))
    return pl.pallas_call(
        functools.partial(_ffn_kernel, final=final),
        grid=(n // tm,),
        in_specs=[row(D_MODEL), row(MEM_WIDTH), _const_spec((MEM_WIDTH, D_MODEL)),
                  _const_spec((1, D_MODEL)), _const_spec((D_MODEL, D_FF)), _const_spec((D_FF, D_MODEL)),
                  _const_spec((1, D_MODEL))],
        out_specs=row(D_MODEL),
        out_shape=jax.ShapeDtypeStruct((n, D_MODEL), F32),
        compiler_params=_params("arbitrary"),
        name="ffn",
    )(x1, om, w_mo, norm_ffn.reshape(1, D_MODEL), w1, w2, norm_final.reshape(1, D_MODEL))


TM = 512
MEM_SPB = 4
SBS_SPB = 4
HG_SPB = 4
SB_BQ = 256
SBS_NEAR = 256


def kernel(x_prompt, x_sample, mem_prompt, cache_sb_k, cache_sb_v, state_hgrn, cache_mem_k, cache_mem_v,
           lb_logits, norm_mix, w_in, sb_gain, hg_gain, w_out, norm_mem_q, norm_mem_kv, w_mq, w_mk, w_mv,
           w_mo, norm_ffn, w_ffn1, w_ffn2, norm_final):
    nbp, tp, _ = x_prompt.shape
    nbs, ts, _ = x_sample.shape
    past = cache_sb_k.shape[2]
    w_in_b, w_out_b, w_mq_b, w_mk_b, w_mv_b, w_mo_b, w1_b, w2_b = (
        w.astype(BF16) for w in (w_in, w_out, w_mq, w_mk, w_mv, w_mo, w_ffn1, w_ffn2))
    lbl = lb_logits.astype(F32)

    mk_p, mv_p = _memkv(mem_prompt.reshape(nbp * N_MEM, D_MODEL), norm_mem_kv, w_mk_b, w_mv_b)
    mdims = (N_MEM, MEM_HEADS, MEM_HEAD_DIM)
    mk_p3 = mk_p.reshape((DEPTH * nbp,) + mdims)
    mv_p3 = mv_p.reshape((DEPTH * nbp,) + mdims)
    mk_s3 = cache_mem_k.reshape((DEPTH * nbs,) + mdims)
    mv_s3 = cache_mem_v.reshape((DEPTH * nbs,) + mdims)
    s0_s = state_hgrn.reshape(DEPTH * nbs, HG_HEADS, HG_HEAD_DIM, HG_HEAD_DIM)
    s0_p = jnp.zeros((nbp, HG_HEADS, HG_HEAD_DIM, HG_HEAD_DIM), F32)
    near = min(SBS_NEAR, past)
    kc4 = cache_sb_k.transpose(0, 1, 3, 4, 2).reshape(DEPTH * nbs, SB_WIDTH, past)
    vc4 = cache_sb_v.transpose(0, 1, 3, 4, 2).reshape(DEPTH * nbs, SB_WIDTH, past)

    def sb_sample(l, qb, kb, vb, nb, t):
        o_near, alive = _sb_sample(qb, kb, vb, kc4, vc4, l, nb, t, near)
        if near == past:
            return o_near
        return lax.cond(jnp.max(alive) > 0.0,
                        lambda: _sb_sample(qb, kb, vb, kc4, vc4, l, nb, t, past)[0],
                        lambda: o_near)

    def layer(l, x2d, nb, t, prompt, carried):
        kv_all, s_all = carried
        qb, k_all, v_all, kb, vb, hg = _proj(x2d, norm_mix[l], w_in_b[l], lbl, TM, l, kv_all, nb, t)
        if prompt:
            osb = _sb_prompt(qb, kb, vb, nb, t, SB_BQ)
            ohg, s_all = _hgrn(hg, hg_gain[l], s0_p, 0, l, nb, t, s_all)
            mk3, mv3, base = mk_p3, mv_p3, l * nb
        else:
            osb = sb_sample(l, qb, kb, vb, nb, t)
            ohg, s_all = _hgrn(hg, hg_gain[l], s0_s, l * nb, l, nb, t, s_all)
            mk3, mv3, base = mk_s3, mv_s3, l * nb
        x1, qm = _mix(x2d, osb, ohg, sb_gain[l], w_out_b[l], norm_mem_q[l], w_mq_b[l], TM)
        om = _memattn(qm, mk3, mv3, base, nb, t, min(TM, t))
        xo = _ffn(x1, om, w_mo_b[l], norm_ffn[l], w1_b[l], w2_b[l], norm_final, l == DEPTH - 1, TM)
        return xo, ((k_all, v_all), s_all)

    xp = x_prompt.reshape(nbp * tp, D_MODEL)
    xs = x_sample.reshape(nbs * ts, D_MODEL)
    out_p = out_s = (None, None)
    for l in range(DEPTH):
        xp, out_p = layer(l, xp, nbp, tp, True, out_p)
        xs, out_s = layer(l, xs, nbs, ts, False, out_s)
    (kp, vp), sp = out_p
    (kn, vn), sn = out_s

    def heads_last(a):
        return a.reshape(DEPTH, nbp, SB_HEADS, SB_HEAD_DIM, tp).transpose(0, 1, 4, 2, 3)

    sbs = (DEPTH, nbs, ts, SB_HEADS, SB_HEAD_DIM)
    memp = (DEPTH, nbp, N_MEM, MEM_HEADS, MEM_HEAD_DIM)
    return (xp.reshape(nbp, tp, D_MODEL), xs.reshape(nbs, ts, D_MODEL),
            heads_last(kp), heads_last(vp), sp, mk_p3.reshape(memp), mv_p3.reshape(memp),
            kn.reshape(sbs), vn.reshape(sbs), sn)
```

```python
import functools

import jax
import jax.numpy as jnp
import numpy as np
from jax import lax
from jax.experimental import pallas as pl
from jax.experimental.pallas import tpu as pltpu

D_MODEL = 1024
DEPTH = 2
SB_HEADS = 8
SB_HEAD_DIM = 64
SB_WIDTH = SB_HEADS * SB_HEAD_DIM
HG_HEADS = 4
HG_HEAD_DIM = 128
HG_WIDTH = HG_HEADS * HG_HEAD_DIM
IN_COLS = 3 * SB_WIDTH + 4 * HG_WIDTH
HG_COLS = 6 * HG_WIDTH
N_MEM = 256
MEM_HEADS = 4
MEM_HEAD_DIM = 128
MEM_WIDTH = MEM_HEADS * MEM_HEAD_DIM
D_FF = 4 * D_MODEL
EPS = 1e-6

LANES = 128
VMEM_LIMIT = 56 * 1024 * 1024
F32 = jnp.float32
BF16 = jnp.bfloat16

_NT = (((1,), (1,)), ((), ()))
_TN = (((0,), (0,)), ((), ()))


def _params(*sem):
    return pltpu.CompilerParams(dimension_semantics=sem, vmem_limit_bytes=VMEM_LIMIT)


def _const_spec(shape):
    nd = len(shape)
    return pl.BlockSpec(shape, lambda *_: (0,) * nd, pipeline_mode=pl.Buffered(1))


def _log2(n):
    assert n > 0 and n & (n - 1) == 0, n
    return n.bit_length() - 1


def _rms(x, gain):
    ms = jnp.mean(x * x, axis=-1, keepdims=True)
    return x * lax.rsqrt(ms + EPS) * gain


def _log_sigmoid_parts(z):
    l = jnp.log(1.0 + jnp.exp(-jnp.abs(z)))
    ls = jnp.minimum(z, 0.0) - l
    return ls, ls - z


LOG2E = 1.4426950408889634


def _exp2_neg_abs(x):
    return jnp.exp2(-jnp.abs(x))


def _split_bf16(x):
    hi = x.astype(BF16)
    lo = (x - hi.astype(F32)).astype(BF16)
    return hi, lo


def _memkv_kernel(mem_ref, g_ref, wk_ref, wv_ref, mk_ref, mv_ref):
    h = _rms(mem_ref[...], g_ref[0]).astype(BF16)
    mk_ref[0] = jnp.dot(h, wk_ref[0], preferred_element_type=F32)
    mv_ref[0] = jnp.dot(h, wv_ref[0], preferred_element_type=F32)


def _memkv(mem2d, gains, wk, wv):
    n = mem2d.shape[0]
    out = jax.ShapeDtypeStruct((DEPTH, n, MEM_WIDTH), F32)
    return pl.pallas_call(
        _memkv_kernel,
        grid=(DEPTH,),
        in_specs=[
            pl.BlockSpec((n, D_MODEL), lambda l: (0, 0)),
            pl.BlockSpec((1, 1, D_MODEL), lambda l: (l, 0, 0)),
            pl.BlockSpec((1, D_MODEL, MEM_WIDTH), lambda l: (l, 0, 0)),
            pl.BlockSpec((1, D_MODEL, MEM_WIDTH), lambda l: (l, 0, 0)),
        ],
        out_specs=[pl.BlockSpec((1, n, MEM_WIDTH), lambda l: (l, 0, 0))] * 2,
        out_shape=[out, out],
        compiler_params=_params("arbitrary"),
        name="memkv",
    )(mem2d, gains.reshape(DEPTH, 1, D_MODEL), wk, wv)


def _proj_kernel(x_ref, g_ref, w_ref, lbl_ref, *refs, layer, slab, time_minor):
    qsb_ref, ksb_ref, vsb_ref, kbf_ref, vbf_ref, hg_ref = refs[-6:]
    h = _rms(x_ref[...], g_ref[...]).astype(BF16)

    def col(c):
        return jnp.dot(h, w_ref[:, c * SB_WIDTH:(c + 1) * SB_WIDTH], preferred_element_type=F32)

    def put(ref, val):
        out = val.T if time_minor else val.reshape(val.shape[0], SB_HEADS, SB_HEAD_DIM)
        for s in range(ref.shape[0]):
            ref[s, 0] = out if s == slab else jnp.zeros_like(out)

    def put_hg(c, val):
        hg_ref[:, c * HG_WIDTH:(c + 1) * HG_WIDTH] = val.astype(BF16)

    lg = lbl_ref[...]
    ex = jnp.exp(lg - jnp.max(lg, axis=0, keepdims=True))
    sm = ex / jnp.sum(ex, axis=0, keepdims=True)
    csum = sm[0:1]
    for l in range(1, layer + 1):
        csum = csum + sm[l:l + 1]
    lb = csum - sm[0:1]

    zf = col(4)
    ez = jnp.exp(-jnp.abs(zf))
    lsig = jnp.minimum(zf, 0.0) - jnp.log(1.0 + ez)
    bterm = jnp.log(1.0 - lb) + lsig
    la = jnp.log(jnp.maximum(lb, 1e-30))
    logf_mix = jnp.maximum(la, bterm) + jnp.log(1.0 + jnp.exp(-jnp.abs(la - bterm)))
    logf = jnp.where(lb > 0.0, logf_mix, bterm)
    put_hg(1, (1.0 - lb) * (jnp.where(zf >= 0.0, ez, 1.0) / (1.0 + ez)))
    hi, lo = _split_bf16(logf * LOG2E)
    put_hg(3, hi)
    put_hg(4, lo)
    g = col(6)
    eg = jnp.exp(-jnp.abs(g))
    put_hg(5, g * (jnp.where(g >= 0.0, 1.0, eg) / (1.0 + eg)))
    put_hg(0, col(3) * (HG_HEAD_DIM ** -0.5))
    put_hg(2, col(5))

    qsb_ref[...] = (col(0) * (SB_HEAD_DIM ** -0.5)).astype(BF16)
    k = col(1)
    put(ksb_ref, k)
    kbf_ref[...] = k.astype(BF16)
    v = col(2)
    put(vsb_ref, v)
    vbf_ref[...] = v.astype(BF16)


def _proj(x2d, gain, w_in, lb_logits, tm, layer, kv_all, nb, t):
    n = x2d.shape[0]
    row = lambda w: pl.BlockSpec((tm, w), lambda i: (i, 0))
    in_specs = [row(D_MODEL), _const_spec((1, D_MODEL)), _const_spec((D_MODEL, IN_COLS)),
                _const_spec((DEPTH, HG_WIDTH))]
    args = (x2d, gain.reshape(1, D_MODEL), w_in, lb_logits)
    time_minor = t % tm == 0
    if time_minor:
        nq = t // tm
        kv_shape, blk = (DEPTH, nb, SB_WIDTH, t), (1, SB_WIDTH, tm)
        where = lambda i: (i // nq, 0, i % nq)
    else:
        kv_shape, blk = (DEPTH, n // tm, tm, SB_HEADS, SB_HEAD_DIM), (1, tm, SB_HEADS, SB_HEAD_DIM)
        where = lambda i: (i, 0, 0, 0)
    if kv_all is None:
        kv_spec, slab, aliases = pl.BlockSpec((DEPTH,) + blk, lambda i: (0,) + where(i)), layer, {}
    else:
        kv_spec, slab, aliases = pl.BlockSpec((1,) + blk, lambda i: (layer,) + where(i)), 0, {4: 1, 5: 2}
        in_specs += [pl.BlockSpec(memory_space=pl.ANY)] * 2
        args += tuple(kv_all)
    return pl.pallas_call(
        functools.partial(_proj_kernel, layer=layer, slab=slab, time_minor=time_minor),
        grid=(n // tm,),
        in_specs=in_specs,
        out_specs=[row(SB_WIDTH), kv_spec, kv_spec, row(SB_WIDTH), row(SB_WIDTH), row(HG_COLS)],
        out_shape=[
            jax.ShapeDtypeStruct((n, SB_WIDTH), BF16),
            jax.ShapeDtypeStruct(kv_shape, F32),
            jax.ShapeDtypeStruct(kv_shape, F32),
            jax.ShapeDtypeStruct((n, SB_WIDTH), BF16),
            jax.ShapeDtypeStruct((n, SB_WIDTH), BF16),
            jax.ShapeDtypeStruct((n, HG_COLS), BF16),
        ],
        input_output_aliases=aliases,
        compiler_params=_params("arbitrary"),
        name="proj",
    )(*args)


SB_BK = LANES
SB_DEAD = -104.0
SB_NPAIR = 2


def _suffix_matrix():
    j = np.arange(SB_BK)[:, None]
    s = np.arange(SB_BK)[None, :]
    u = np.concatenate([(j > s).astype(np.float32), np.ones((SB_BK, SB_BK), np.float32)], axis=1)
    return jnp.asarray(np.concatenate([u, u], axis=0), dtype=BF16)


def _sbp_kernel(q_ref, k_ref, v_ref, u_ref, o_ref, acc_ref, run_ref, *, bq, npair):
    bk = SB_BK
    qi = pl.program_id(2)
    lane = lax.broadcasted_iota(jnp.int32, (1, LANES), 1)
    m_a = (lane < SB_HEAD_DIM).astype(BF16)
    m_b = (lane >= SB_HEAD_DIM).astype(BF16)
    acc_ref[...] = jnp.zeros_like(acc_ref)
    run_ref[...] = jnp.zeros_like(run_ref)
    u = u_ref[...]
    nd = bq // bk
    assert nd % 2 == 0, "earlier key blocks are taken two per trip"

    def sweep(blocks):
        units = [(j, rel, p) for j, rel in blocks for p in range(npair)]
        ps = lambda p: slice(p * LANES, (p + 1) * LANES)
        keys = lambda j: pl.ds(pl.multiple_of(j * bk, bk), bk)
        qrows = lambda rel: slice(0 if rel is None else rel, bq)
        masks = {rel: (lax.broadcasted_iota(jnp.int32, (bq - rel, bk), 1)
                       < lax.broadcasted_iota(jnp.int32, (bq - rel, bk), 0))
                 for _, rel in blocks if rel is not None}
        zs = []
        for j, rel, p in units:
            ks = k_ref[keys(j), ps(p)]
            kcat = jnp.concatenate([ks * m_a, ks * m_b], axis=0)
            zs.append(lax.dot_general(q_ref[qrows(rel), ps(p)], kcat, _NT, preferred_element_type=F32))
        lss, css = [], []
        for (j, rel, p), z in zip(units, zs):
            for hh in range(2):
                ls, lk = _log_sigmoid_parts(z[:, hh * bk:(hh + 1) * bk])
                if rel is not None:
                    lk = jnp.where(masks[rel], lk, 0.0)
                hi, lo = _split_bf16(lk)
                lss.append(ls)
                css.append(jnp.dot(jnp.concatenate([hi, lo], axis=1), u, preferred_element_type=F32))
        for i, (j, rel, p) in enumerate(units):
            a_parts = []
            for hh in range(2):
                ls, cs = lss[2 * i + hh], css[2 * i + hh]
                run = run_ref[2 * p + hh, qrows(rel)]
                a = jnp.exp(ls + cs[:, :bk] + run)
                if rel is not None:
                    a = jnp.where(masks[rel], a, 0.0)
                run_ref[2 * p + hh, qrows(rel)] = run + cs[:, bk:]
                a_parts.append(a.astype(BF16))
            vs = v_ref[keys(j), ps(p)]
            vcat = jnp.concatenate([vs * m_a, vs * m_b], axis=0)
            acc_ref[qrows(rel), ps(p)] += jnp.dot(jnp.concatenate(a_parts, axis=1), vcat,
                                                  preferred_element_type=F32)

    sweep([(qi * nd + d, d * bk) for d in range(nd - 1, -1, -1)])

    def live():
        return jnp.max(run_ref[...]) > SB_DEAD

    def cond(state):
        j, alive = state
        return jnp.logical_and(j >= 0, alive)

    def body(state):
        j, _ = state
        sweep([(j, None), (j - 1, None)])
        return j - 2, live()

    lax.while_loop(cond, body, (qi * nd - 1, live()))
    o_ref[...] = acc_ref[...]


def _sb_prompt(qb, kb, vb, nb, t, bq):
    nq = t // bq
    w = SB_NPAIR * LANES
    return pl.pallas_call(
        functools.partial(_sbp_kernel, bq=bq, npair=SB_NPAIR),
        grid=(nb, SB_WIDTH // w, nq),
        in_specs=[
            pl.BlockSpec((bq, w), lambda b, h, i: (b * nq + i, h)),
            pl.BlockSpec((t, w), lambda b, h, i: (b, h)),
            pl.BlockSpec((t, w), lambda b, h, i: (b, h)),
            _const_spec((2 * SB_BK, 2 * SB_BK)),
        ],
        out_specs=pl.BlockSpec((bq, w), lambda b, h, i: (b * nq + i, h)),
        out_shape=jax.ShapeDtypeStruct((nb * t, SB_WIDTH), F32),
        scratch_shapes=[pltpu.VMEM((bq, w), F32), pltpu.VMEM((2 * SB_NPAIR, bq, LANES), F32)],
        compiler_params=_params("arbitrary", "arbitrary", "arbitrary"),
        name="sb_prompt",
    )(qb, kb, vb, _suffix_matrix())


def _sbs_kernel(q_ref, kn_ref, vn_ref, kt_ref, vt_ref, u_ref, o_ref, alive_ref, acc_ref, run_ref,
                *, tq, span, spb):
    bk = SB_BK
    nr = SB_HEADS * tq
    streams = range(spb)
    srows = lambda s: slice(s * tq, (s + 1) * tq)
    row = lax.broadcasted_iota(jnp.int32, (nr, SB_WIDTH), 0)
    colw = lax.broadcasted_iota(jnp.int32, (nr, SB_WIDTH), 1)
    same_head = (row >> _log2(tq)) == (colw >> _log2(SB_HEAD_DIM))
    qx = [jnp.where(same_head, jnp.concatenate([q_ref[srows(s)]] * SB_HEADS, axis=0), jnp.zeros((), BF16))
          for s in streams]
    u = u_ref[...]
    acc_ref[...] = jnp.zeros_like(acc_ref)
    run_ref[...] = jnp.zeros_like(run_ref)

    def weights(zs, mask):
        parts = [_log_sigmoid_parts(z) for z in zs]
        css = []
        for ls, lk in parts:
            if mask is not None:
                lk = jnp.where(mask, lk, 0.0)
            hi, lo = _split_bf16(lk)
            css.append(jnp.dot(jnp.concatenate([hi, lo], axis=1), u, preferred_element_type=F32))
        out = []
        for s, ((ls, _), cs) in enumerate(zip(parts, css)):
            run = run_ref[s]
            a = jnp.exp(ls + cs[:, :bk] + run)
            if mask is not None:
                a = jnp.where(mask, a, 0.0)
            run_ref[s] = run + cs[:, bk:]
            out.append(a.astype(BF16))
        return out

    pad = jnp.zeros((bk - tq, SB_WIDTH), BF16)
    skey = lax.broadcasted_iota(jnp.int32, (nr, bk), 1)
    tqry = lax.broadcasted_iota(jnp.int32, (nr, bk), 0) & (tq - 1)
    a = weights([lax.dot_general(qx[s], jnp.concatenate([kn_ref[srows(s)], pad], axis=0), _NT,
                                 preferred_element_type=F32) for s in streams], skey < tqry)
    for s in streams:
        acc_ref[s] += jnp.dot(a[s], jnp.concatenate([vn_ref[srows(s)], pad], axis=0),
                              preferred_element_type=F32)

    def live():
        return jnp.max(run_ref[...]) > SB_DEAD

    for j in range(span // bk - 1, -1, -1):
        @pl.when(live())
        def _():
            cols = slice(j * bk, (j + 1) * bk)
            a = weights([jnp.dot(qx[s], kt_ref[s, :, cols].astype(BF16), preferred_element_type=F32)
                         for s in streams], None)
            for s in streams:
                acc_ref[s] += lax.dot_general(a[s], vt_ref[s, :, cols].astype(BF16), _NT,
                                              preferred_element_type=F32)

    alive_ref[...] = jnp.where(live(), jnp.ones(alive_ref.shape, F32), jnp.zeros(alive_ref.shape, F32))
    for s in streams:
        accm = jnp.where(same_head, acc_ref[s], 0.0)
        out = accm[0:tq]
        for h in range(1, SB_HEADS):
            out = out + accm[h * tq:(h + 1) * tq]
        o_ref[srows(s)] = out


def _sb_sample(qb, kb, vb, kcache, vcache, layer, nb, tq, span):
    past = kcache.shape[2]
    assert past % span == 0 and span % SB_BK == 0
    spb = SBS_SPB if nb % SBS_SPB == 0 and span <= SBS_NEAR else 1
    rowspec = pl.BlockSpec((spb * tq, SB_WIDTH), lambda s: (s, 0))
    cspec = pl.BlockSpec((spb, SB_WIDTH, span), lambda s: (layer * nb // spb + s, 0, past // span - 1))
    nr = SB_HEADS * tq
    return pl.pallas_call(
        functools.partial(_sbs_kernel, tq=tq, span=span, spb=spb),
        grid=(nb // spb,),
        in_specs=[rowspec, rowspec, rowspec, cspec, cspec, _const_spec((2 * SB_BK, 2 * SB_BK))],
        out_specs=[rowspec, pl.BlockSpec((spb, 1, LANES), lambda s: (s, 0, 0))],
        out_shape=[jax.ShapeDtypeStruct((nb * tq, SB_WIDTH), F32),
                   jax.ShapeDtypeStruct((nb, 1, LANES), F32)],
        scratch_shapes=[pltpu.VMEM((spb, nr, SB_WIDTH), F32), pltpu.VMEM((spb, nr, SB_BK), F32)],
        compiler_params=_params("arbitrary"),
        name="sb_sample",
    )(qb, kb, vb, kcache, vcache, _suffix_matrix())


def _hgrn_consts(c):
    levels = []
    m = 1
    while m < c:
        levels.append(m)
        m *= 2
    t = np.arange(c)[:, None]
    u = np.arange(c)[None, :]
    mats = [(u <= t).astype(np.float32)]
    masks = [np.eye(c, dtype=np.float32)]
    for m in levels:
        end_a = (t // (2 * m)) * 2 * m + m - 1
        in_b = (t % (2 * m)) >= m
        w = np.where(in_b, (u > end_a) & (u <= t), False).astype(np.float32)
        w = w - np.where(~in_b, (u > t) & (u <= end_a), False).astype(np.float32)
        mats.append(w)
        masks.append((((t // (2 * m)) == (u // (2 * m))) & in_b & ((u % (2 * m)) < m)).astype(np.float32))
    w = np.concatenate(mats, axis=0)
    return tuple(levels), jnp.asarray(np.concatenate([w, w], axis=1), dtype=BF16), jnp.asarray(np.stack(masks))


def _hgrn_kernel(hg_ref, gain_ref, s0_ref, w_ref, mk_ref, *refs, c, levels, slab, nsteps, spb, cps):
    o_ref, sfin_ref, st_ref = refs[-3:]
    step = pl.program_id(1)
    hd = HG_HEAD_DIM
    assert spb == 1 or cps == 1
    groups = range(spb * cps)
    units = [(g, h) for g in groups for h in range(HG_HEADS)]
    nu = range(len(units))
    slot = lambda g, h: (g if cps == 1 else 0) * HG_HEADS + h

    @pl.when(step == 0)
    def _():
        for s in range(spb):
            for h in range(HG_HEADS):
                st_ref[slot(s, h)] = s0_ref[s, h].T

    rows = lambda g: slice(g * c, (g + 1) * c)
    col = lambda base, g, h: hg_ref[rows(g), base * HG_WIDTH + h * hd:base * HG_WIDTH + (h + 1) * hd]
    w = w_ref[...]
    dd = [jnp.dot(w, jnp.concatenate([hg_ref[rows(g), 3 * HG_WIDTH:4 * HG_WIDTH],
                                      hg_ref[rows(g), 4 * HG_WIDTH:5 * HG_WIDTH]], axis=0),
                  preferred_element_type=F32) for g in groups]
    hsl = lambda h: slice(h * hd, (h + 1) * hd)
    q, k, v = ([col(b, g, h) for g, h in units] for b in range(3))
    cum = [dd[g][0:c, hsl(h)] for g, h in units]

    sc = [lax.dot_general(q[i], k[i], _NT, preferred_element_type=F32) * mk_ref[0] for i in nu]
    for li in range(len(levels)):
        for i, (g, h) in enumerate(units):
            e = _exp2_neg_abs(dd[g][(li + 1) * c:(li + 2) * c, hsl(h)]).astype(BF16)
            sc[i] = sc[i] + lax.dot_general(q[i] * e, k[i] * e, _NT,
                                            preferred_element_type=F32) * mk_ref[li + 1]
    st = {}
    o = [None] * len(units)
    for i, (g, h) in enumerate(units):
        sl_ = slot(g, h)
        cur = st.get(sl_)
        if cur is None:
            cur = st_ref[sl_]
        o[i] = lax.dot_general(q[i] * jnp.exp2(cum[i]).astype(BF16), cur.astype(BF16), _NT,
                               preferred_element_type=F32)
        last = cum[i][c - 1:c, :]
        kdec = k[i] * jnp.exp2(last - cum[i]).astype(BF16)
        st[sl_] = cur * jnp.exp2(last) + lax.dot_general(v[i], kdec, _TN, preferred_element_type=F32)
    for sl_, val in st.items():
        st_ref[sl_] = val
    for i in nu:
        o[i] = o[i] + jnp.dot(sc[i].astype(BF16), v[i], preferred_element_type=F32)
    for i, (g, h) in enumerate(units):
        o_ref[rows(g), hsl(h)] = (_rms(o[i], gain_ref[:, hsl(h)]) * col(5, g, h).astype(F32)).astype(BF16)

    @pl.when(step == nsteps - 1)
    def _():
        for d in range(sfin_ref.shape[0]):
            for s in range(spb):
                for h in range(HG_HEADS):
                    sfin_ref[d, s, h] = st_ref[slot(s, h)].T if d == slab else jnp.zeros((hd, hd), F32)


def _hgrn(hg, hg_gain, s0, s0_base, layer, nb, t, s_all):
    c = min(LANES, t)
    nchunks = t // c
    levels, wcat, masks = _hgrn_consts(c)
    nl = len(levels) + 1
    sdims = (HG_HEADS, HG_HEAD_DIM, HG_HEAD_DIM)
    spb = HG_SPB if nchunks == 1 and nb % HG_SPB == 0 and s0_base % HG_SPB == 0 else 1
    cps = HG_CPS if nchunks % HG_CPS == 0 else 1
    nsteps = nchunks // cps
    rb = spb * cps * c
    in_specs = [
        pl.BlockSpec((rb, HG_COLS), lambda b, s: (b * nsteps + s, 0)),
        _const_spec((1, HG_WIDTH)),
        pl.BlockSpec((spb,) + sdims, lambda b, s: (s0_base // spb + b, 0, 0, 0)),
        _const_spec((nl * c, 2 * c)),
        _const_spec((nl, c, c)),
    ]
    args = (hg, hg_gain.reshape(1, HG_WIDTH), s0, wcat, masks)
    if s_all is None:
        s_spec, slab, aliases = pl.BlockSpec((DEPTH, spb) + sdims, lambda b, s: (0, b, 0, 0, 0)), layer, {}
    else:
        s_spec, slab, aliases = pl.BlockSpec((1, spb) + sdims, lambda b, s: (layer, b, 0, 0, 0)), 0, {5: 1}
        in_specs.append(pl.BlockSpec(memory_space=pl.ANY))
        args += (s_all,)
    return pl.pallas_call(
        functools.partial(_hgrn_kernel, c=c, levels=levels, slab=slab, nsteps=nsteps, spb=spb, cps=cps),
        grid=(nb // spb, nsteps),
        in_specs=in_specs,
        out_specs=[pl.BlockSpec((rb, HG_WIDTH), lambda b, s: (b * nsteps + s, 0)), s_spec],
        out_shape=[
            jax.ShapeDtypeStruct((nb * t, HG_WIDTH), BF16),
            jax.ShapeDtypeStruct((DEPTH, nb) + sdims, F32),
        ],
        input_output_aliases=aliases,
        scratch_shapes=[pltpu.VMEM((spb * HG_HEADS, HG_HEAD_DIM, HG_HEAD_DIM), F32)],
        compiler_params=_params("arbitrary", "arbitrary"),
        name="hgrn",
    )(*args)


def _mix_kernel(x_ref, osb_ref, ohg_ref, sbg_ref, wout_ref, nmq_ref, wmq_ref, x1_ref, qm_ref):
    a = _rms(osb_ref[...], sbg_ref[...]).astype(BF16)
    b = ohg_ref[...]
    x1 = (x_ref[...]
          + jnp.dot(a, wout_ref[0:SB_WIDTH, :], preferred_element_type=F32)
          + jnp.dot(b, wout_ref[SB_WIDTH:, :], preferred_element_type=F32))
    x1_ref[...] = x1
    h = _rms(x1, nmq_ref[...]).astype(BF16)
    qm_ref[...] = jnp.dot(h, wmq_ref[...], preferred_element_type=F32).astype(BF16)


def _mix(x2d, osb, ohg, sb_gain, w_out, norm_mem_q, w_mq, tm):
    n = x2d.shape[0]
    row = lambda w: pl.BlockSpec((tm, w), lambda i: (i, 0))
    return pl.pallas_call(
        _mix_kernel,
        grid=(n // tm,),
        in_specs=[row(D_MODEL), row(SB_WIDTH), row(HG_WIDTH), _const_spec((1, SB_WIDTH)),
                  _const_spec((D_MODEL, D_MODEL)), _const_spec((1, D_MODEL)),
                  _const_spec((D_MODEL, MEM_WIDTH))],
        out_specs=[row(D_MODEL), row(MEM_WIDTH)],
        out_shape=[jax.ShapeDtypeStruct((n, D_MODEL), F32), jax.ShapeDtypeStruct((n, MEM_WIDTH), BF16)],
        compiler_params=_params("arbitrary"),
        name="mix",
    )(x2d, osb, ohg, sb_gain.reshape(1, SB_WIDTH), w_out, norm_mem_q.reshape(1, D_MODEL), w_mq)


def _memattn_kernel(q_ref, mk_ref, mv_ref, o_ref, *, spb, tm):
    hd = MEM_HEAD_DIM
    mk = [mk_ref[s].reshape(N_MEM, MEM_WIDTH).astype(BF16) for s in range(spb)]
    mv = [mv_ref[s].reshape(N_MEM, MEM_WIDTH).astype(BF16) for s in range(spb)]
    units = [(s, h) for s in range(spb) for h in range(MEM_HEADS)]
    rows = lambda s: slice(s * tm, (s + 1) * tm)
    sl = lambda h: slice(h * hd, (h + 1) * hd)
    sc = [lax.dot_general(q_ref[rows(s), sl(h)], mk[s][:, sl(h)], _NT,
                          preferred_element_type=F32) * (hd ** -0.5) for s, h in units]
    p = [jnp.exp(x - jnp.max(x, axis=-1, keepdims=True)) for x in sc]
    den = [jnp.sum(x, axis=-1, keepdims=True) for x in p]
    o = [jnp.dot(x.astype(BF16), mv[s][:, sl(h)], preferred_element_type=F32) for x, (s, h) in zip(p, units)]
    for x, d, (s, h) in zip(o, den, units):
        o_ref[rows(s), sl(h)] = (x / d).astype(BF16)


def _memattn(qm, mk3, mv3, base, nb, t, tm):
    nt = t // tm
    spb = MEM_SPB if nt == 1 and nb % MEM_SPB == 0 and base % MEM_SPB == 0 else 1
    mspec = pl.BlockSpec((spb, N_MEM, MEM_HEADS, MEM_HEAD_DIM), lambda b, i: (base // spb + b, 0, 0, 0))
    qspec = pl.BlockSpec((spb * tm, MEM_WIDTH), lambda b, i: (b * nt + i, 0))
    return pl.pallas_call(
        functools.partial(_memattn_kernel, spb=spb, tm=tm),
        grid=(nb // spb, nt),
        in_specs=[qspec, mspec, mspec],
        out_specs=qspec,
        out_shape=jax.ShapeDtypeStruct((nb * t, MEM_WIDTH), BF16),
        compiler_params=_params("arbitrary", "arbitrary"),
        name="memattn",
    )(qm, mk3, mv3)


FF_CHUNK = 1024


def _ffn_kernel(x1_ref, om_ref, wmo_ref, nf_ref, w1_ref, w2_ref, nfin_ref, o_ref, *, final):
    x2 = x1_ref[...] + jnp.dot(om_ref[...], wmo_ref[...], preferred_element_type=F32)
    h = _rms(x2, nf_ref[...]).astype(BF16)
    x3 = x2
    for c in range(D_FF // FF_CHUNK):
        cs = slice(c * FF_CHUNK, (c + 1) * FF_CHUNK)
        r = jnp.maximum(jnp.dot(h, w1_ref[:, cs], preferred_element_type=F32), 0.0)
        x3 = x3 + jnp.dot((r * r).astype(BF16), w2_ref[cs, :], preferred_element_type=F32)
    o_ref[...] = _rms(x3, nfin_ref[...]) if final else x3


def _ffn(x1, om, w_mo, norm_ffn, w1, w2, norm_final, final, tm):
    n = x1.shape[0]
    row = lambda w: pl.BlockSpec((tm, w), lambda i: (i, 0))
    return pl.pallas_call(
        functools.partial(_ffn_kernel, final=final),
        grid=(n // tm,),
        in_specs=[row(D_MODEL), row(MEM_WIDTH), _const_spec((MEM_WIDTH, D_MODEL)),
                  _const_spec((1, D_MODEL)), _const_spec((D_MODEL, D_FF)), _const_spec((D_FF, D_MODEL)),
                  _const_spec((1, D_MODEL))],
        out_specs=row(D_MODEL),
        out_shape=jax.ShapeDtypeStruct((n, D_MODEL), F32),
        compiler_params=_params("arbitrary"),
        name="ffn",
    )(x1, om, w_mo, norm_ffn.reshape(1, D_MODEL), w1, w2, norm_final.reshape(1, D_MODEL))


TM = 512
TM_FFN = 1024
MEM_SPB = 4
SBS_SPB = 4
HG_SPB = 4
HG_CPS = 2
SB_BQ = 256
SBS_NEAR = 256


def kernel(x_prompt, x_sample, mem_prompt, cache_sb_k, cache_sb_v, state_hgrn, cache_mem_k, cache_mem_v,
           lb_logits, norm_mix, w_in, sb_gain, hg_gain, w_out, norm_mem_q, norm_mem_kv, w_mq, w_mk, w_mv,
           w_mo, norm_ffn, w_ffn1, w_ffn2, norm_final):
    nbp, tp, _ = x_prompt.shape
    nbs, ts, _ = x_sample.shape
    past = cache_sb_k.shape[2]
    w_in_b, w_out_b, w_mq_b, w_mk_b, w_mv_b, w_mo_b, w1_b, w2_b = (
        w.astype(BF16) for w in (w_in, w_out, w_mq, w_mk, w_mv, w_mo, w_ffn1, w_ffn2))
    lbl = lb_logits.astype(F32)

    mk_p, mv_p = _memkv(mem_prompt.reshape(nbp * N_MEM, D_MODEL), norm_mem_kv, w_mk_b, w_mv_b)
    mdims = (N_MEM, MEM_HEADS, MEM_HEAD_DIM)
    mk_p3 = mk_p.reshape((DEPTH * nbp,) + mdims)
    mv_p3 = mv_p.reshape((DEPTH * nbp,) + mdims)
    mk_s3 = cache_mem_k.reshape((DEPTH * nbs,) + mdims)
    mv_s3 = cache_mem_v.reshape((DEPTH * nbs,) + mdims)
    s0_s = state_hgrn.reshape(DEPTH * nbs, HG_HEADS, HG_HEAD_DIM, HG_HEAD_DIM)
    s0_p = jnp.zeros((nbp, HG_HEADS, HG_HEAD_DIM, HG_HEAD_DIM), F32)
    near = min(SBS_NEAR, past)
    kc4 = cache_sb_k.transpose(0, 1, 3, 4, 2).reshape(DEPTH * nbs, SB_WIDTH, past)
    vc4 = cache_sb_v.transpose(0, 1, 3, 4, 2).reshape(DEPTH * nbs, SB_WIDTH, past)

    def sb_sample(l, qb, kb, vb, nb, t):
        o_near, alive = _sb_sample(qb, kb, vb, kc4, vc4, l, nb, t, near)
        if near == past:
            return o_near
        return lax.cond(jnp.max(alive) > 0.0,
                        lambda: _sb_sample(qb, kb, vb, kc4, vc4, l, nb, t, past)[0],
                        lambda: o_near)

    def layer(l, x2d, nb, t, prompt, carried):
        kv_all, s_all = carried
        qb, k_all, v_all, kb, vb, hg = _proj(x2d, norm_mix[l], w_in_b[l], lbl, TM, l, kv_all, nb, t)
        if prompt:
            osb = _sb_prompt(qb, kb, vb, nb, t, SB_BQ)
            ohg, s_all = _hgrn(hg, hg_gain[l], s0_p, 0, l, nb, t, s_all)
            mk3, mv3, base = mk_p3, mv_p3, l * nb
        else:
            osb = sb_sample(l, qb, kb, vb, nb, t)
            ohg, s_all = _hgrn(hg, hg_gain[l], s0_s, l * nb, l, nb, t, s_all)
            mk3, mv3, base = mk_s3, mv_s3, l * nb
        x1, qm = _mix(x2d, osb, ohg, sb_gain[l], w_out_b[l], norm_mem_q[l], w_mq_b[l], TM)
        om = _memattn(qm, mk3, mv3, base, nb, t, min(TM, t))
        xo = _ffn(x1, om, w_mo_b[l], norm_ffn[l], w1_b[l], w2_b[l], norm_final, l == DEPTH - 1,
                  min(TM_FFN, nb * t))
        return xo, ((k_all, v_all), s_all)

    xp = x_prompt.reshape(nbp * tp, D_MODEL)
    xs = x_sample.reshape(nbs * ts, D_MODEL)
    out_p = out_s = (None, None)
    for l in range(DEPTH):
        xp, out_p = layer(l, xp, nbp, tp, True, out_p)
        xs, out_s = layer(l, xs, nbs, ts, False, out_s)
    (kp, vp), sp = out_p
    (kn, vn), sn = out_s

    def heads_last(a):
        return a.reshape(DEPTH, nbp, SB_HEADS, SB_HEAD_DIM, tp).transpose(0, 1, 4, 2, 3)

    sbs = (DEPTH, nbs, ts, SB_HEADS, SB_HEAD_DIM)
    memp = (DEPTH, nbp, N_MEM, MEM_HEADS, MEM_HEAD_DIM)
    return (xp.reshape(nbp, tp, D_MODEL), xs.reshape(nbs, ts, D_MODEL),
            heads_last(kp), heads_last(vp), sp, mk_p3.reshape(memp), mv_p3.reshape(memp),
            kn.reshape(sbs), vn.reshape(sbs), sn)
```

```python
import functools

import jax
import jax.numpy as jnp
import numpy as np
from jax import lax
from jax.experimental import pallas as pl
from jax.experimental.pallas import tpu as pltpu

D_MODEL = 1024
DEPTH = 2
SB_HEADS = 8
SB_HEAD_DIM = 64
SB_WIDTH = SB_HEADS * SB_HEAD_DIM
HG_HEADS = 4
HG_HEAD_DIM = 128
HG_WIDTH = HG_HEADS * HG_HEAD_DIM
IN_COLS = 3 * SB_WIDTH + 4 * HG_WIDTH
HG_COLS = 6 * HG_WIDTH
N_MEM = 256
MEM_HEADS = 4
MEM_HEAD_DIM = 128
MEM_WIDTH = MEM_HEADS * MEM_HEAD_DIM
D_FF = 4 * D_MODEL
EPS = 1e-6

LANES = 128
VMEM_LIMIT = 56 * 1024 * 1024
F32 = jnp.float32
BF16 = jnp.bfloat16

_NT = (((1,), (1,)), ((), ()))
_TN = (((0,), (0,)), ((), ()))


def _params(*sem):
    return pltpu.CompilerParams(dimension_semantics=sem, vmem_limit_bytes=VMEM_LIMIT)


def _const_spec(shape):
    nd = len(shape)
    return pl.BlockSpec(shape, lambda *_: (0,) * nd, pipeline_mode=pl.Buffered(1))


def _log2(n):
    assert n > 0 and n & (n - 1) == 0, n
    return n.bit_length() - 1


def _rms(x, gain):
    ms = jnp.mean(x * x, axis=-1, keepdims=True)
    return x * lax.rsqrt(ms + EPS) * gain


def _log_sigmoid_parts(z):
    l = jnp.log(1.0 + jnp.exp(-jnp.abs(z)))
    ls = jnp.minimum(z, 0.0) - l
    return ls, ls - z


LOG2E = 1.4426950408889634


def _exp2_neg_abs(x):
    return jnp.exp2(-jnp.abs(x))


def _split_bf16(x):
    hi = x.astype(BF16)
    lo = (x - hi.astype(F32)).astype(BF16)
    return hi, lo


def _memkv_kernel(mem_ref, g_ref, wk_ref, wv_ref, mk_ref, mv_ref):
    h = _rms(mem_ref[...], g_ref[0]).astype(BF16)
    mk_ref[0] = jnp.dot(h, wk_ref[0], preferred_element_type=F32)
    mv_ref[0] = jnp.dot(h, wv_ref[0], preferred_element_type=F32)


def _memkv(mem2d, gains, wk, wv):
    n = mem2d.shape[0]
    out = jax.ShapeDtypeStruct((DEPTH, n, MEM_WIDTH), F32)
    return pl.pallas_call(
        _memkv_kernel,
        grid=(DEPTH,),
        in_specs=[
            pl.BlockSpec((n, D_MODEL), lambda l: (0, 0)),
            pl.BlockSpec((1, 1, D_MODEL), lambda l: (l, 0, 0)),
            pl.BlockSpec((1, D_MODEL, MEM_WIDTH), lambda l: (l, 0, 0)),
            pl.BlockSpec((1, D_MODEL, MEM_WIDTH), lambda l: (l, 0, 0)),
        ],
        out_specs=[pl.BlockSpec((1, n, MEM_WIDTH), lambda l: (l, 0, 0))] * 2,
        out_shape=[out, out],
        compiler_params=_params("arbitrary"),
        name="memkv",
    )(mem2d, gains.reshape(DEPTH, 1, D_MODEL), wk, wv)


def _proj_kernel(x_ref, g_ref, w_ref, lbl_ref, *refs, layer, slab, time_minor):
    qsb_ref, ksb_ref, vsb_ref, kbf_ref, vbf_ref, hg_ref = refs[-6:]
    h = _rms(x_ref[...], g_ref[...]).astype(BF16)

    def col(c):
        return jnp.dot(h, w_ref[:, c * SB_WIDTH:(c + 1) * SB_WIDTH], preferred_element_type=F32)

    def put(ref, val):
        out = val.T if time_minor else val.reshape(val.shape[0], SB_HEADS, SB_HEAD_DIM)
        for s in range(ref.shape[0]):
            ref[s, 0] = out if s == slab else jnp.zeros_like(out)

    def put_hg(c, val):
        hg_ref[:, c * HG_WIDTH:(c + 1) * HG_WIDTH] = val.astype(BF16)

    lg = lbl_ref[...]
    ex = jnp.exp(lg - jnp.max(lg, axis=0, keepdims=True))
    sm = ex / jnp.sum(ex, axis=0, keepdims=True)
    csum = sm[0:1]
    for l in range(1, layer + 1):
        csum = csum + sm[l:l + 1]
    lb = csum - sm[0:1]

    zf = col(4)
    ez = jnp.exp(-jnp.abs(zf))
    lsig = jnp.minimum(zf, 0.0) - jnp.log(1.0 + ez)
    bterm = jnp.log(1.0 - lb) + lsig
    la = jnp.log(jnp.maximum(lb, 1e-30))
    logf_mix = jnp.maximum(la, bterm) + jnp.log(1.0 + jnp.exp(-jnp.abs(la - bterm)))
    logf = jnp.where(lb > 0.0, logf_mix, bterm)
    put_hg(1, (1.0 - lb) * (jnp.where(zf >= 0.0, ez, 1.0) / (1.0 + ez)))
    hi, lo = _split_bf16(logf * LOG2E)
    put_hg(3, hi)
    put_hg(4, lo)
    g = col(6)
    eg = jnp.exp(-jnp.abs(g))
    put_hg(5, g * (jnp.where(g >= 0.0, 1.0, eg) / (1.0 + eg)))
    put_hg(0, col(3) * (HG_HEAD_DIM ** -0.5))
    put_hg(2, col(5))

    qsb_ref[...] = (col(0) * (SB_HEAD_DIM ** -0.5)).astype(BF16)
    k = col(1)
    put(ksb_ref, k)
    kbf_ref[...] = k.astype(BF16)
    v = col(2)
    put(vsb_ref, v)
    vbf_ref[...] = v.astype(BF16)


def _proj(x2d, gain, w_in, lb_logits, tm, layer, kv_all, nb, t):
    n = x2d.shape[0]
    row = lambda w: pl.BlockSpec((tm, w), lambda i: (i, 0))
    in_specs = [row(D_MODEL), _const_spec((1, D_MODEL)), _const_spec((D_MODEL, IN_COLS)),
                _const_spec((DEPTH, HG_WIDTH))]
    args = (x2d, gain.reshape(1, D_MODEL), w_in, lb_logits)
    time_minor = t % tm == 0
    if time_minor:
        nq = t // tm
        kv_shape, blk = (DEPTH, nb, SB_WIDTH, t), (1, SB_WIDTH, tm)
        where = lambda i: (i // nq, 0, i % nq)
    else:
        kv_shape, blk = (DEPTH, n // tm, tm, SB_HEADS, SB_HEAD_DIM), (1, tm, SB_HEADS, SB_HEAD_DIM)
        where = lambda i: (i, 0, 0, 0)
    if kv_all is None:
        kv_spec, slab, aliases = pl.BlockSpec((DEPTH,) + blk, lambda i: (0,) + where(i)), layer, {}
    else:
        kv_spec, slab, aliases = pl.BlockSpec((1,) + blk, lambda i: (layer,) + where(i)), 0, {4: 1, 5: 2}
        in_specs += [pl.BlockSpec(memory_space=pl.ANY)] * 2
        args += tuple(kv_all)
    return pl.pallas_call(
        functools.partial(_proj_kernel, layer=layer, slab=slab, time_minor=time_minor),
        grid=(n // tm,),
        in_specs=in_specs,
        out_specs=[row(SB_WIDTH), kv_spec, kv_spec, row(SB_WIDTH), row(SB_WIDTH), row(HG_COLS)],
        out_shape=[
            jax.ShapeDtypeStruct((n, SB_WIDTH), BF16),
            jax.ShapeDtypeStruct(kv_shape, F32),
            jax.ShapeDtypeStruct(kv_shape, F32),
            jax.ShapeDtypeStruct((n, SB_WIDTH), BF16),
            jax.ShapeDtypeStruct((n, SB_WIDTH), BF16),
            jax.ShapeDtypeStruct((n, HG_COLS), BF16),
        ],
        input_output_aliases=aliases,
        compiler_params=_params("arbitrary"),
        name="proj",
    )(*args)


SB_BK = LANES
SB_DEAD = -104.0
SB_NPAIR = 4


def _suffix_matrix():
    j = np.arange(SB_BK)[:, None]
    s = np.arange(SB_BK)[None, :]
    u = np.concatenate([(j > s).astype(np.float32), np.ones((SB_BK, SB_BK), np.float32)], axis=1)
    return jnp.asarray(np.concatenate([u, u], axis=0), dtype=BF16)


def _sbp_kernel(q_ref, k_ref, v_ref, u_ref, o_ref, acc_ref, run_ref, *, bq, npair):
    bk = SB_BK
    qi = pl.program_id(2)
    lane = lax.broadcasted_iota(jnp.int32, (1, LANES), 1)
    m_a = (lane < SB_HEAD_DIM).astype(BF16)
    m_b = (lane >= SB_HEAD_DIM).astype(BF16)
    acc_ref[...] = jnp.zeros_like(acc_ref)
    run_ref[...] = jnp.zeros_like(run_ref)
    u = u_ref[...]
    nd = bq // bk
    assert nd % 2 == 0, "earlier key blocks are taken two per trip"

    def sweep(blocks):
        units = [(j, rel, p) for j, rel in blocks for p in range(npair)]
        ps = lambda p: slice(p * LANES, (p + 1) * LANES)
        keys = lambda j: pl.ds(pl.multiple_of(j * bk, bk), bk)
        qrows = lambda rel: slice(0 if rel is None else rel, bq)
        masks = {rel: (lax.broadcasted_iota(jnp.int32, (bq - rel, bk), 1)
                       < lax.broadcasted_iota(jnp.int32, (bq - rel, bk), 0))
                 for _, rel in blocks if rel is not None}
        zs = []
        for j, rel, p in units:
            ks = k_ref[keys(j), ps(p)]
            kcat = jnp.concatenate([ks * m_a, ks * m_b], axis=0)
            zs.append(lax.dot_general(q_ref[qrows(rel), ps(p)], kcat, _NT, preferred_element_type=F32))
        lss, css = [], []
        for (j, rel, p), z in zip(units, zs):
            for hh in range(2):
                ls, lk = _log_sigmoid_parts(z[:, hh * bk:(hh + 1) * bk])
                if rel is not None:
                    lk = jnp.where(masks[rel], lk, 0.0)
                hi, lo = _split_bf16(lk)
                lss.append(ls)
                css.append(jnp.dot(jnp.concatenate([hi, lo], axis=1), u, preferred_element_type=F32))
        for i, (j, rel, p) in enumerate(units):
            a_parts = []
            for hh in range(2):
                ls, cs = lss[2 * i + hh], css[2 * i + hh]
                run = run_ref[2 * p + hh, qrows(rel)]
                a = jnp.exp(ls + cs[:, :bk] + run)
                if rel is not None:
                    a = jnp.where(masks[rel], a, 0.0)
                run_ref[2 * p + hh, qrows(rel)] = run + cs[:, bk:]
                a_parts.append(a.astype(BF16))
            vs = v_ref[keys(j), ps(p)]
            vcat = jnp.concatenate([vs * m_a, vs * m_b], axis=0)
            acc_ref[qrows(rel), ps(p)] += jnp.dot(jnp.concatenate(a_parts, axis=1), vcat,
                                                  preferred_element_type=F32)

    sweep([(qi * nd + d, d * bk) for d in range(nd - 1, -1, -1)])

    def live():
        return jnp.max(run_ref[...]) > SB_DEAD

    def cond(state):
        j, alive = state
        return jnp.logical_and(j >= 0, alive)

    def body(state):
        j, _ = state
        sweep([(j, None), (j - 1, None)])
        return j - 2, live()

    lax.while_loop(cond, body, (qi * nd - 1, live()))
    o_ref[...] = acc_ref[...]


def _sb_prompt(qb, kb, vb, nb, t, bq):
    nq = t // bq
    w = SB_NPAIR * LANES
    return pl.pallas_call(
        functools.partial(_sbp_kernel, bq=bq, npair=SB_NPAIR),
        grid=(nb, SB_WIDTH // w, nq),
        in_specs=[
            pl.BlockSpec((bq, w), lambda b, h, i: (b * nq + i, h)),
            pl.BlockSpec((t, w), lambda b, h, i: (b, h)),
            pl.BlockSpec((t, w), lambda b, h, i: (b, h)),
            _const_spec((2 * SB_BK, 2 * SB_BK)),
        ],
        out_specs=pl.BlockSpec((bq, w), lambda b, h, i: (b * nq + i, h)),
        out_shape=jax.ShapeDtypeStruct((nb * t, SB_WIDTH), F32),
        scratch_shapes=[pltpu.VMEM((bq, w), F32), pltpu.VMEM((2 * SB_NPAIR, bq, LANES), F32)],
        compiler_params=_params("arbitrary", "arbitrary", "arbitrary"),
        name="sb_prompt",
    )(qb, kb, vb, _suffix_matrix())


def _sbs_kernel(q_ref, kn_ref, vn_ref, kt_ref, vt_ref, u_ref, o_ref, alive_ref, acc_ref, run_ref,
                *, tq, span, spb):
    bk = SB_BK
    nr = SB_HEADS * tq
    streams = range(spb)
    srows = lambda s: slice(s * tq, (s + 1) * tq)
    row = lax.broadcasted_iota(jnp.int32, (nr, SB_WIDTH), 0)
    colw = lax.broadcasted_iota(jnp.int32, (nr, SB_WIDTH), 1)
    same_head = (row >> _log2(tq)) == (colw >> _log2(SB_HEAD_DIM))
    qx = [jnp.where(same_head, jnp.concatenate([q_ref[srows(s)]] * SB_HEADS, axis=0), jnp.zeros((), BF16))
          for s in streams]
    u = u_ref[...]
    acc_ref[...] = jnp.zeros_like(acc_ref)
    run_ref[...] = jnp.zeros_like(run_ref)

    def weights(zs, mask):
        parts = [_log_sigmoid_parts(z) for z in zs]
        css = []
        for ls, lk in parts:
            if mask is not None:
                lk = jnp.where(mask, lk, 0.0)
            hi, lo = _split_bf16(lk)
            css.append(jnp.dot(jnp.concatenate([hi, lo], axis=1), u, preferred_element_type=F32))
        out = []
        for s, ((ls, _), cs) in enumerate(zip(parts, css)):
            run = run_ref[s]
            a = jnp.exp(ls + cs[:, :bk] + run)
            if mask is not None:
                a = jnp.where(mask, a, 0.0)
            run_ref[s] = run + cs[:, bk:]
            out.append(a.astype(BF16))
        return out

    pad = jnp.zeros((bk - tq, SB_WIDTH), BF16)
    skey = lax.broadcasted_iota(jnp.int32, (nr, bk), 1)
    tqry = lax.broadcasted_iota(jnp.int32, (nr, bk), 0) & (tq - 1)
    a = weights([lax.dot_general(qx[s], jnp.concatenate([kn_ref[srows(s)], pad], axis=0), _NT,
                                 preferred_element_type=F32) for s in streams], skey < tqry)
    for s in streams:
        acc_ref[s] += jnp.dot(a[s], jnp.concatenate([vn_ref[srows(s)], pad], axis=0),
                              preferred_element_type=F32)

    def live():
        return jnp.max(run_ref[...]) > SB_DEAD

    for j in range(span // bk - 1, -1, -1):
        @pl.when(live())
        def _():
            cols = slice(j * bk, (j + 1) * bk)
            a = weights([jnp.dot(qx[s], kt_ref[s, :, cols].astype(BF16), preferred_element_type=F32)
                         for s in streams], None)
            for s in streams:
                acc_ref[s] += lax.dot_general(a[s], vt_ref[s, :, cols].astype(BF16), _NT,
                                              preferred_element_type=F32)

    alive_ref[...] = jnp.where(live(), jnp.ones(alive_ref.shape, F32), jnp.zeros(alive_ref.shape, F32))
    for s in streams:
        accm = jnp.where(same_head, acc_ref[s], 0.0)
        out = accm[0:tq]
        for h in range(1, SB_HEADS):
            out = out + accm[h * tq:(h + 1) * tq]
        o_ref[srows(s)] = out


def _sb_sample(qb, kb, vb, kcache, vcache, layer, nb, tq, span):
    past = kcache.shape[2]
    assert past % span == 0 and span % SB_BK == 0
    spb = SBS_SPB if nb % SBS_SPB == 0 and span <= SBS_NEAR else 1
    rowspec = pl.BlockSpec((spb * tq, SB_WIDTH), lambda s: (s, 0))
    cspec = pl.BlockSpec((spb, SB_WIDTH, span), lambda s: (layer * nb // spb + s, 0, past // span - 1))
    nr = SB_HEADS * tq
    return pl.pallas_call(
        functools.partial(_sbs_kernel, tq=tq, span=span, spb=spb),
        grid=(nb // spb,),
        in_specs=[rowspec, rowspec, rowspec, cspec, cspec, _const_spec((2 * SB_BK, 2 * SB_BK))],
        out_specs=[rowspec, pl.BlockSpec((spb, 1, LANES), lambda s: (s, 0, 0))],
        out_shape=[jax.ShapeDtypeStruct((nb * tq, SB_WIDTH), F32),
                   jax.ShapeDtypeStruct((nb, 1, LANES), F32)],
        scratch_shapes=[pltpu.VMEM((spb, nr, SB_WIDTH), F32), pltpu.VMEM((spb, nr, SB_BK), F32)],
        compiler_params=_params("arbitrary"),
        name="sb_sample",
    )(qb, kb, vb, kcache, vcache, _suffix_matrix())


def _hgrn_consts(c):
    levels = []
    m = 1
    while m < c:
        levels.append(m)
        m *= 2
    t = np.arange(c)[:, None]
    u = np.arange(c)[None, :]
    mats = [(u <= t).astype(np.float32)]
    masks = [np.eye(c, dtype=np.float32)]
    for m in levels:
        end_a = (t // (2 * m)) * 2 * m + m - 1
        in_b = (t % (2 * m)) >= m
        w = np.where(in_b, (u > end_a) & (u <= t), False).astype(np.float32)
        w = w - np.where(~in_b, (u > t) & (u <= end_a), False).astype(np.float32)
        mats.append(w)
        masks.append((((t // (2 * m)) == (u // (2 * m))) & in_b & ((u % (2 * m)) < m)).astype(np.float32))
    w = np.concatenate(mats, axis=0)
    return tuple(levels), jnp.asarray(np.concatenate([w, w], axis=1), dtype=BF16), jnp.asarray(np.stack(masks))


def _hgrn_kernel(hg_ref, gain_ref, s0_ref, w_ref, mk_ref, *refs, c, levels, slab, nsteps, spb, cps):
    o_ref, sfin_ref, st_ref = refs[-3:]
    step = pl.program_id(1)
    hd = HG_HEAD_DIM
    assert spb == 1 or cps == 1
    groups = range(spb * cps)
    units = [(g, h) for g in groups for h in range(HG_HEADS)]
    nu = range(len(units))
    slot = lambda g, h: (g if cps == 1 else 0) * HG_HEADS + h

    @pl.when(step == 0)
    def _():
        for s in range(spb):
            for h in range(HG_HEADS):
                st_ref[slot(s, h)] = s0_ref[s, h].T

    rows = lambda g: slice(g * c, (g + 1) * c)
    col = lambda base, g, h: hg_ref[rows(g), base * HG_WIDTH + h * hd:base * HG_WIDTH + (h + 1) * hd]
    w = w_ref[...]
    dd = [jnp.dot(w, jnp.concatenate([hg_ref[rows(g), 3 * HG_WIDTH:4 * HG_WIDTH],
                                      hg_ref[rows(g), 4 * HG_WIDTH:5 * HG_WIDTH]], axis=0),
                  preferred_element_type=F32) for g in groups]
    hsl = lambda h: slice(h * hd, (h + 1) * hd)
    q, k, v = ([col(b, g, h) for g, h in units] for b in range(3))
    cum = [dd[g][0:c, hsl(h)] for g, h in units]

    sc = [lax.dot_general(q[i], k[i], _NT, preferred_element_type=F32) * mk_ref[0] for i in nu]
    for li in range(len(levels)):
        for i, (g, h) in enumerate(units):
            e = _exp2_neg_abs(dd[g][(li + 1) * c:(li + 2) * c, hsl(h)]).astype(BF16)
            sc[i] = sc[i] + lax.dot_general(q[i] * e, k[i] * e, _NT,
                                            preferred_element_type=F32) * mk_ref[li + 1]
    st = {}
    o = [None] * len(units)
    for i, (g, h) in enumerate(units):
        sl_ = slot(g, h)
        cur = st.get(sl_)
        if cur is None:
            cur = st_ref[sl_]
        o[i] = lax.dot_general(q[i] * jnp.exp2(cum[i]).astype(BF16), cur.astype(BF16), _NT,
                               preferred_element_type=F32)
        last = cum[i][c - 1:c, :]
        kdec = k[i] * jnp.exp2(last - cum[i]).astype(BF16)
        st[sl_] = cur * jnp.exp2(last) + lax.dot_general(v[i], kdec, _TN, preferred_element_type=F32)
    for sl_, val in st.items():
        st_ref[sl_] = val
    for i in nu:
        o[i] = o[i] + jnp.dot(sc[i].astype(BF16), v[i], preferred_element_type=F32)
    for i, (g, h) in enumerate(units):
        o_ref[rows(g), hsl(h)] = (_rms(o[i], gain_ref[:, hsl(h)]) * col(5, g, h).astype(F32)).astype(BF16)

    @pl.when(step == nsteps - 1)
    def _():
        for d in range(sfin_ref.shape[0]):
            for s in range(spb):
                for h in range(HG_HEADS):
                    sfin_ref[d, s, h] = st_ref[slot(s, h)].T if d == slab else jnp.zeros((hd, hd), F32)


def _hgrn(hg, hg_gain, s0, s0_base, layer, nb, t, s_all):
    c = min(LANES, t)
    nchunks = t // c
    levels, wcat, masks = _hgrn_consts(c)
    nl = len(levels) + 1
    sdims = (HG_HEADS, HG_HEAD_DIM, HG_HEAD_DIM)
    spb = HG_SPB if nchunks == 1 and nb % HG_SPB == 0 and s0_base % HG_SPB == 0 else 1
    cps = HG_CPS if nchunks % HG_CPS == 0 else 1
    nsteps = nchunks // cps
    rb = spb * cps * c
    in_specs = [
        pl.BlockSpec((rb, HG_COLS), lambda b, s: (b * nsteps + s, 0)),
        _const_spec((1, HG_WIDTH)),
        pl.BlockSpec((spb,) + sdims, lambda b, s: (s0_base // spb + b, 0, 0, 0)),
        _const_spec((nl * c, 2 * c)),
        _const_spec((nl, c, c)),
    ]
    args = (hg, hg_gain.reshape(1, HG_WIDTH), s0, wcat, masks)
    if s_all is None:
        s_spec, slab, aliases = pl.BlockSpec((DEPTH, spb) + sdims, lambda b, s: (0, b, 0, 0, 0)), layer, {}
    else:
        s_spec, slab, aliases = pl.BlockSpec((1, spb) + sdims, lambda b, s: (layer, b, 0, 0, 0)), 0, {5: 1}
        in_specs.append(pl.BlockSpec(memory_space=pl.ANY))
        args += (s_all,)
    return pl.pallas_call(
        functools.partial(_hgrn_kernel, c=c, levels=levels, slab=slab, nsteps=nsteps, spb=spb, cps=cps),
        grid=(nb // spb, nsteps),
        in_specs=in_specs,
        out_specs=[pl.BlockSpec((rb, HG_WIDTH), lambda b, s: (b * nsteps + s, 0)), s_spec],
        out_shape=[
            jax.ShapeDtypeStruct((nb * t, HG_WIDTH), BF16),
            jax.ShapeDtypeStruct((DEPTH, nb) + sdims, F32),
        ],
        input_output_aliases=aliases,
        scratch_shapes=[pltpu.VMEM((spb * HG_HEADS, HG_HEAD_DIM, HG_HEAD_DIM), F32)],
        compiler_params=_params("arbitrary", "arbitrary"),
        name="hgrn",
    )(*args)


def _mix_kernel(x_ref, osb_ref, ohg_ref, sbg_ref, wout_ref, nmq_ref, wmq_ref, *refs):
    x1_ref, out_ref = refs[-2:]
    a = _rms(osb_ref[...], sbg_ref[...]).astype(BF16)
    b = ohg_ref[...]
    x1 = (x_ref[...]
          + jnp.dot(a, wout_ref[0:SB_WIDTH, :], preferred_element_type=F32)
          + jnp.dot(b, wout_ref[SB_WIDTH:, :], preferred_element_type=F32))
    x1_ref[...] = x1
    h = _rms(x1, nmq_ref[...]).astype(BF16)
    qm = jnp.dot(h, wmq_ref[...], preferred_element_type=F32).astype(BF16)
    if len(refs) == 2:
        out_ref[...] = qm
    else:
        _mem_attention(lambda s, sl: qm[:, sl], refs[0], refs[1], out_ref, spb=1, tm=qm.shape[0])


def _mix(x2d, osb, ohg, sb_gain, w_out, norm_mem_q, w_mq, tm, mem=None):
    n = x2d.shape[0]
    row = lambda w: pl.BlockSpec((tm, w), lambda i: (i, 0))
    in_specs = [row(D_MODEL), row(SB_WIDTH), row(HG_WIDTH), _const_spec((1, SB_WIDTH)),
                _const_spec((D_MODEL, D_MODEL)), _const_spec((1, D_MODEL)),
                _const_spec((D_MODEL, MEM_WIDTH))]
    args = (x2d, osb, ohg, sb_gain.reshape(1, SB_WIDTH), w_out, norm_mem_q.reshape(1, D_MODEL), w_mq)
    if mem is not None:
        mk3, mv3, base, t = mem
        assert t % tm == 0
        mspec = pl.BlockSpec((1, N_MEM, MEM_HEADS, MEM_HEAD_DIM), lambda i: (base + i // (t // tm), 0, 0, 0))
        in_specs += [mspec, mspec]
        args += (mk3, mv3)
    return pl.pallas_call(
        _mix_kernel,
        grid=(n // tm,),
        in_specs=in_specs,
        out_specs=[row(D_MODEL), row(MEM_WIDTH)],
        out_shape=[jax.ShapeDtypeStruct((n, D_MODEL), F32), jax.ShapeDtypeStruct((n, MEM_WIDTH), BF16)],
        compiler_params=_params("arbitrary"),
        name="mix",
    )(*args)


def _mem_attention(q_of, mk_ref, mv_ref, o_ref, *, spb, tm):
    hd = MEM_HEAD_DIM
    mk = [mk_ref[s].reshape(N_MEM, MEM_WIDTH).astype(BF16) for s in range(spb)]
    mv = [mv_ref[s].reshape(N_MEM, MEM_WIDTH).astype(BF16) for s in range(spb)]
    units = [(s, h) for s in range(spb) for h in range(MEM_HEADS)]
    rows = lambda s: slice(s * tm, (s + 1) * tm)
    sl = lambda h: slice(h * hd, (h + 1) * hd)
    sc = [lax.dot_general(q_of(s, sl(h)), mk[s][:, sl(h)], _NT,
                          preferred_element_type=F32) * (hd ** -0.5) for s, h in units]
    p = [jnp.exp(x - jnp.max(x, axis=-1, keepdims=True)) for x in sc]
    den = [jnp.sum(x, axis=-1, keepdims=True) for x in p]
    o = [jnp.dot(x.astype(BF16), mv[s][:, sl(h)], preferred_element_type=F32) for x, (s, h) in zip(p, units)]
    for x, d, (s, h) in zip(o, den, units):
        o_ref[rows(s), sl(h)] = (x / d).astype(BF16)


def _memattn_kernel(q_ref, mk_ref, mv_ref, o_ref, *, spb, tm):
    _mem_attention(lambda s, sl: q_ref[s * tm:(s + 1) * tm, sl], mk_ref, mv_ref, o_ref, spb=spb, tm=tm)


def _memattn(qm, mk3, mv3, base, nb, t, tm):
    nt = t // tm
    spb = MEM_SPB if nt == 1 and nb % MEM_SPB == 0 and base % MEM_SPB == 0 else 1
    mspec = pl.BlockSpec((spb, N_MEM, MEM_HEADS, MEM_HEAD_DIM), lambda b, i: (base // spb + b, 0, 0, 0))
    qspec = pl.BlockSpec((spb * tm, MEM_WIDTH), lambda b, i: (b * nt + i, 0))
    return pl.pallas_call(
        functools.partial(_memattn_kernel, spb=spb, tm=tm),
        grid=(nb // spb, nt),
        in_specs=[qspec, mspec, mspec],
        out_specs=qspec,
        out_shape=jax.ShapeDtypeStruct((nb * t, MEM_WIDTH), BF16),
        compiler_params=_params("arbitrary", "arbitrary"),
        name="memattn",
    )(qm, mk3, mv3)


FF_CHUNK = 1024


def _ffn_kernel(x1_ref, om_ref, wmo_ref, nf_ref, w1_ref, w2_ref, nfin_ref, o_ref, *, final):
    x2 = x1_ref[...] + jnp.dot(om_ref[...], wmo_ref[...], preferred_element_type=F32)
    h = _rms(x2, nf_ref[...]).astype(BF16)
    x3 = x2
    for c in range(D_FF // FF_CHUNK):
        cs = slice(c * FF_CHUNK, (c + 1) * FF_CHUNK)
        r = jnp.maximum(jnp.dot(h, w1_ref[:, cs], preferred_element_type=F32), 0.0)
        x3 = x3 + jnp.dot((r * r).astype(BF16), w2_ref[cs, :], preferred_element_type=F32)
    o_ref[...] = _rms(x3, nfin_ref[...]) if final else x3


def _ffn(x1, om, w_mo, norm_ffn, w1, w2, norm_final, final, tm):
    n = x1.shape[0]
    row = lambda w: pl.BlockSpec((tm, w), lambda i: (i, 0))
    return pl.pallas_call(
        functools.partial(_ffn_kernel, final=final),
        grid=(n // tm,),
        in_specs=[row(D_MODEL), row(MEM_WIDTH), _const_spec((MEM_WIDTH, D_MODEL)),
                  _const_spec((1, D_MODEL)), _const_spec((D_MODEL, D_FF)), _const_spec((D_FF, D_MODEL)),
                  _const_spec((1, D_MODEL))],
        out_specs=row(D_MODEL),
        out_shape=jax.ShapeDtypeStruct((n, D_MODEL), F32),
        compiler_params=_params("arbitrary"),
        name="ffn",
    )(x1, om, w_mo, norm_ffn.reshape(1, D_MODEL), w1, w2, norm_final.reshape(1, D_MODEL))


TM = 512
TM_FFN = 1024
MEM_SPB = 4
SBS_SPB = 4
HG_SPB = 4
HG_CPS = 2
SB_BQ = 256
SBS_NEAR = 256


def kernel(x_prompt, x_sample, mem_prompt, cache_sb_k, cache_sb_v, state_hgrn, cache_mem_k, cache_mem_v,
           lb_logits, norm_mix, w_in, sb_gain, hg_gain, w_out, norm_mem_q, norm_mem_kv, w_mq, w_mk, w_mv,
           w_mo, norm_ffn, w_ffn1, w_ffn2, norm_final):
    nbp, tp, _ = x_prompt.shape
    nbs, ts, _ = x_sample.shape
    past = cache_sb_k.shape[2]
    w_in_b, w_out_b, w_mq_b, w_mk_b, w_mv_b, w_mo_b, w1_b, w2_b = (
        w.astype(BF16) for w in (w_in, w_out, w_mq, w_mk, w_mv, w_mo, w_ffn1, w_ffn2))
    lbl = lb_logits.astype(F32)

    mk_p, mv_p = _memkv(mem_prompt.reshape(nbp * N_MEM, D_MODEL), norm_mem_kv, w_mk_b, w_mv_b)
    mdims = (N_MEM, MEM_HEADS, MEM_HEAD_DIM)
    mk_p3 = mk_p.reshape((DEPTH * nbp,) + mdims)
    mv_p3 = mv_p.reshape((DEPTH * nbp,) + mdims)
    mk_s3 = cache_mem_k.reshape((DEPTH * nbs,) + mdims)
    mv_s3 = cache_mem_v.reshape((DEPTH * nbs,) + mdims)
    s0_s = state_hgrn.reshape(DEPTH * nbs, HG_HEADS, HG_HEAD_DIM, HG_HEAD_DIM)
    s0_p = jnp.zeros((nbp, HG_HEADS, HG_HEAD_DIM, HG_HEAD_DIM), F32)
    near = min(SBS_NEAR, past)
    kc4 = cache_sb_k.transpose(0, 1, 3, 4, 2).reshape(DEPTH * nbs, SB_WIDTH, past)
    vc4 = cache_sb_v.transpose(0, 1, 3, 4, 2).reshape(DEPTH * nbs, SB_WIDTH, past)

    def sb_sample(l, qb, kb, vb, nb, t):
        o_near, alive = _sb_sample(qb, kb, vb, kc4, vc4, l, nb, t, near)
        if near == past:
            return o_near
        return lax.cond(jnp.max(alive) > 0.0,
                        lambda: _sb_sample(qb, kb, vb, kc4, vc4, l, nb, t, past)[0],
                        lambda: o_near)

    def layer(l, x2d, nb, t, prompt, carried):
        kv_all, s_all = carried
        qb, k_all, v_all, kb, vb, hg = _proj(x2d, norm_mix[l], w_in_b[l], lbl, TM, l, kv_all, nb, t)
        if prompt:
            osb = _sb_prompt(qb, kb, vb, nb, t, SB_BQ)
            ohg, s_all = _hgrn(hg, hg_gain[l], s0_p, 0, l, nb, t, s_all)
            mk3, mv3, base = mk_p3, mv_p3, l * nb
        else:
            osb = sb_sample(l, qb, kb, vb, nb, t)
            ohg, s_all = _hgrn(hg, hg_gain[l], s0_s, l * nb, l, nb, t, s_all)
            mk3, mv3, base = mk_s3, mv_s3, l * nb
        if t % TM == 0:
            x1, om = _mix(x2d, osb, ohg, sb_gain[l], w_out_b[l], norm_mem_q[l], w_mq_b[l], TM,
                          mem=(mk3, mv3, base, t))
        else:
            x1, qm = _mix(x2d, osb, ohg, sb_gain[l], w_out_b[l], norm_mem_q[l], w_mq_b[l], TM)
            om = _memattn(qm, mk3, mv3, base, nb, t, min(TM, t))
        xo = _ffn(x1, om, w_mo_b[l], norm_ffn[l], w1_b[l], w2_b[l], norm_final, l == DEPTH - 1,
                  min(TM_FFN, nb * t))
        return xo, ((k_all, v_all), s_all)

    xp = x_prompt.reshape(nbp * tp, D_MODEL)
    xs = x_sample.reshape(nbs * ts, D_MODEL)
    out_p = out_s = (None, None)
    for l in range(DEPTH):
        xp, out_p = layer(l, xp, nbp, tp, True, out_p)
        xs, out_s = layer(l, xs, nbs, ts, False, out_s)
    (kp, vp), sp = out_p
    (kn, vn), sn = out_s

    def heads_last(a):
        return a.reshape(DEPTH, nbp, SB_HEADS, SB_HEAD_DIM, tp).transpose(0, 1, 4, 2, 3)

    sbs = (DEPTH, nbs, ts, SB_HEADS, SB_HEAD_DIM)
    memp = (DEPTH, nbp, N_MEM, MEM_HEADS, MEM_HEAD_DIM)
    return (xp.reshape(nbp, tp, D_MODEL), xs.reshape(nbs, ts, D_MODEL),
            heads_last(kp), heads_last(vp), sp, mk_p3.reshape(memp), mv_p3.reshape(memp),
            kn.reshape(sbs), vn.reshape(sbs), sn)
```

```python
import functools

import jax
import jax.numpy as jnp
import numpy as np
from jax import lax
from jax.experimental import pallas as pl
from jax.experimental.pallas import tpu as pltpu

D_MODEL = 1024
DEPTH = 2
SB_HEADS = 8
SB_HEAD_DIM = 64
SB_WIDTH = SB_HEADS * SB_HEAD_DIM
HG_HEADS = 4
HG_HEAD_DIM = 128
HG_WIDTH = HG_HEADS * HG_HEAD_DIM
IN_COLS = 3 * SB_WIDTH + 4 * HG_WIDTH
HG_COLS = 6 * HG_WIDTH
N_MEM = 256
MEM_HEADS = 4
MEM_HEAD_DIM = 128
MEM_WIDTH = MEM_HEADS * MEM_HEAD_DIM
D_FF = 4 * D_MODEL
EPS = 1e-6

LANES = 128
VMEM_LIMIT = 56 * 1024 * 1024
F32 = jnp.float32
BF16 = jnp.bfloat16

_NT = (((1,), (1,)), ((), ()))
_TN = (((0,), (0,)), ((), ()))


def _params(*sem):
    return pltpu.CompilerParams(dimension_semantics=sem, vmem_limit_bytes=VMEM_LIMIT)


def _const_spec(shape):
    nd = len(shape)
    return pl.BlockSpec(shape, lambda *_: (0,) * nd, pipeline_mode=pl.Buffered(1))


def _log2(n):
    assert n > 0 and n & (n - 1) == 0, n
    return n.bit_length() - 1


def _rms(x, gain):
    ms = jnp.mean(x * x, axis=-1, keepdims=True)
    return x * lax.rsqrt(ms + EPS) * gain


def _log_sigmoid_parts(z):
    l = jnp.log(1.0 + jnp.exp(-jnp.abs(z)))
    ls = jnp.minimum(z, 0.0) - l
    return ls, ls - z


LOG2E = 1.4426950408889634


def _exp2_neg_abs(x):
    return jnp.exp2(-jnp.abs(x))


def _split_bf16(x):
    hi = x.astype(BF16)
    lo = (x - hi.astype(F32)).astype(BF16)
    return hi, lo


def _memkv_kernel(mem_ref, g_ref, wk_ref, wv_ref, mk_ref, mv_ref):
    h = _rms(mem_ref[...], g_ref[0]).astype(BF16)
    mk_ref[0] = jnp.dot(h, wk_ref[0], preferred_element_type=F32)
    mv_ref[0] = jnp.dot(h, wv_ref[0], preferred_element_type=F32)


def _memkv(mem2d, gains, wk, wv):
    n = mem2d.shape[0]
    out = jax.ShapeDtypeStruct((DEPTH, n, MEM_WIDTH), F32)
    return pl.pallas_call(
        _memkv_kernel,
        grid=(DEPTH,),
        in_specs=[
            pl.BlockSpec((n, D_MODEL), lambda l: (0, 0)),
            pl.BlockSpec((1, 1, D_MODEL), lambda l: (l, 0, 0)),
            pl.BlockSpec((1, D_MODEL, MEM_WIDTH), lambda l: (l, 0, 0)),
            pl.BlockSpec((1, D_MODEL, MEM_WIDTH), lambda l: (l, 0, 0)),
        ],
        out_specs=[pl.BlockSpec((1, n, MEM_WIDTH), lambda l: (l, 0, 0))] * 2,
        out_shape=[out, out],
        compiler_params=_params("arbitrary"),
        name="memkv",
    )(mem2d, gains.reshape(DEPTH, 1, D_MODEL), wk, wv)


def _proj_kernel(x_ref, g_ref, w_ref, lbl_ref, *refs, layer, slab, time_minor):
    qsb_ref, ksb_ref, vsb_ref, kbf_ref, vbf_ref, hg_ref = refs[-6:]
    h = _rms(x_ref[...], g_ref[...]).astype(BF16)

    def col(c):
        return jnp.dot(h, w_ref[:, c * SB_WIDTH:(c + 1) * SB_WIDTH], preferred_element_type=F32)

    def put(ref, val):
        out = val.T if time_minor else val.reshape(val.shape[0], SB_HEADS, SB_HEAD_DIM)
        for s in range(ref.shape[0]):
            ref[s, 0] = out if s == slab else jnp.zeros_like(out)

    def put_hg(c, val):
        hg_ref[:, c * HG_WIDTH:(c + 1) * HG_WIDTH] = val.astype(BF16)

    lg = lbl_ref[...]
    ex = jnp.exp(lg - jnp.max(lg, axis=0, keepdims=True))
    sm = ex / jnp.sum(ex, axis=0, keepdims=True)
    csum = sm[0:1]
    for l in range(1, layer + 1):
        csum = csum + sm[l:l + 1]
    lb = csum - sm[0:1]

    zf = col(4)
    ez = jnp.exp(-jnp.abs(zf))
    lsig = jnp.minimum(zf, 0.0) - jnp.log(1.0 + ez)
    bterm = jnp.log(1.0 - lb) + lsig
    la = jnp.log(jnp.maximum(lb, 1e-30))
    logf_mix = jnp.maximum(la, bterm) + jnp.log(1.0 + jnp.exp(-jnp.abs(la - bterm)))
    logf = jnp.where(lb > 0.0, logf_mix, bterm)
    put_hg(1, (1.0 - lb) * (jnp.where(zf >= 0.0, ez, 1.0) / (1.0 + ez)))
    hi, lo = _split_bf16(logf * LOG2E)
    put_hg(3, hi)
    put_hg(4, lo)
    g = col(6)
    eg = jnp.exp(-jnp.abs(g))
    put_hg(5, g * (jnp.where(g >= 0.0, 1.0, eg) / (1.0 + eg)))
    put_hg(0, col(3) * (HG_HEAD_DIM ** -0.5))
    put_hg(2, col(5))

    qsb_ref[...] = (col(0) * (SB_HEAD_DIM ** -0.5)).astype(BF16)
    k = col(1)
    put(ksb_ref, k)
    kbf_ref[...] = k.astype(BF16)
    v = col(2)
    put(vsb_ref, v)
    vbf_ref[...] = v.astype(BF16)


def _proj(x2d, gain, w_in, lb_logits, tm, layer, kv_all, nb, t):
    n = x2d.shape[0]
    row = lambda w: pl.BlockSpec((tm, w), lambda i: (i, 0))
    in_specs = [row(D_MODEL), _const_spec((1, D_MODEL)), _const_spec((D_MODEL, IN_COLS)),
                _const_spec((DEPTH, HG_WIDTH))]
    args = (x2d, gain.reshape(1, D_MODEL), w_in, lb_logits)
    time_minor = t % tm == 0
    if time_minor:
        nq = t // tm
        kv_shape, blk = (DEPTH, nb, SB_WIDTH, t), (1, SB_WIDTH, tm)
        where = lambda i: (i // nq, 0, i % nq)
    else:
        kv_shape, blk = (DEPTH, n // tm, tm, SB_HEADS, SB_HEAD_DIM), (1, tm, SB_HEADS, SB_HEAD_DIM)
        where = lambda i: (i, 0, 0, 0)
    if kv_all is None:
        kv_spec, slab, aliases = pl.BlockSpec((DEPTH,) + blk, lambda i: (0,) + where(i)), layer, {}
    else:
        kv_spec, slab, aliases = pl.BlockSpec((1,) + blk, lambda i: (layer,) + where(i)), 0, {4: 1, 5: 2}
        in_specs += [pl.BlockSpec(memory_space=pl.ANY)] * 2
        args += tuple(kv_all)
    return pl.pallas_call(
        functools.partial(_proj_kernel, layer=layer, slab=slab, time_minor=time_minor),
        grid=(n // tm,),
        in_specs=in_specs,
        out_specs=[row(SB_WIDTH), kv_spec, kv_spec, row(SB_WIDTH), row(SB_WIDTH), row(HG_COLS)],
        out_shape=[
            jax.ShapeDtypeStruct((n, SB_WIDTH), BF16),
            jax.ShapeDtypeStruct(kv_shape, F32),
            jax.ShapeDtypeStruct(kv_shape, F32),
            jax.ShapeDtypeStruct((n, SB_WIDTH), BF16),
            jax.ShapeDtypeStruct((n, SB_WIDTH), BF16),
            jax.ShapeDtypeStruct((n, HG_COLS), BF16),
        ],
        input_output_aliases=aliases,
        compiler_params=_params("arbitrary"),
        name="proj",
    )(*args)


SB_BK = LANES
SB_DEAD = -104.0
SB_NPAIR = 4


def _suffix_matrix():
    j = np.arange(SB_BK)[:, None]
    s = np.arange(SB_BK)[None, :]
    u = np.concatenate([(j > s).astype(np.float32), np.ones((SB_BK, SB_BK), np.float32)], axis=1)
    return jnp.asarray(np.concatenate([u, u], axis=0), dtype=BF16)


def _sbp_kernel(q_ref, k_ref, v_ref, u_ref, o_ref, acc_ref, run_ref, *, bq, npair):
    bk = SB_BK
    qi = pl.program_id(2)
    lane = lax.broadcasted_iota(jnp.int32, (1, LANES), 1)
    m_a = (lane < SB_HEAD_DIM).astype(BF16)
    m_b = (lane >= SB_HEAD_DIM).astype(BF16)
    acc_ref[...] = jnp.zeros_like(acc_ref)
    run_ref[...] = jnp.zeros_like(run_ref)
    u = u_ref[...]
    nd = bq // bk
    assert nd % 2 == 0, "earlier key blocks are taken two per trip"

    def sweep(blocks):
        units = [(j, rel, p) for j, rel in blocks for p in range(npair)]
        ps = lambda p: slice(p * LANES, (p + 1) * LANES)
        keys = lambda j: pl.ds(pl.multiple_of(j * bk, bk), bk)
        qrows = lambda rel: slice(0 if rel is None else rel, bq)
        masks = {rel: (lax.broadcasted_iota(jnp.int32, (bq - rel, bk), 1)
                       < lax.broadcasted_iota(jnp.int32, (bq - rel, bk), 0))
                 for _, rel in blocks if rel is not None}
        zs = []
        for j, rel, p in units:
            ks = k_ref[keys(j), ps(p)]
            kcat = jnp.concatenate([ks * m_a, ks * m_b], axis=0)
            zs.append(lax.dot_general(q_ref[qrows(rel), ps(p)], kcat, _NT, preferred_element_type=F32))
        lss, css = [], []
        for (j, rel, p), z in zip(units, zs):
            for hh in range(2):
                ls, lk = _log_sigmoid_parts(z[:, hh * bk:(hh + 1) * bk])
                if rel is not None:
                    lk = jnp.where(masks[rel], lk, 0.0)
                hi, lo = _split_bf16(lk)
                lss.append(ls)
                css.append(jnp.dot(jnp.concatenate([hi, lo], axis=1), u, preferred_element_type=F32))
        for i, (j, rel, p) in enumerate(units):
            a_parts = []
            for hh in range(2):
                ls, cs = lss[2 * i + hh], css[2 * i + hh]
                run = run_ref[2 * p + hh, qrows(rel)]
                a = jnp.exp(ls + cs[:, :bk] + run)
                if rel is not None:
                    a = jnp.where(masks[rel], a, 0.0)
                run_ref[2 * p + hh, qrows(rel)] = run + cs[:, bk:]
                a_parts.append(a.astype(BF16))
            vs = v_ref[keys(j), ps(p)]
            vcat = jnp.concatenate([vs * m_a, vs * m_b], axis=0)
            acc_ref[qrows(rel), ps(p)] += jnp.dot(jnp.concatenate(a_parts, axis=1), vcat,
                                                  preferred_element_type=F32)

    sweep([(qi * nd + d, d * bk) for d in range(nd - 1, -1, -1)])

    def live():
        return jnp.max(run_ref[...]) > SB_DEAD

    def cond(state):
        j, alive = state
        return jnp.logical_and(j >= 0, alive)

    def body(state):
        j, _ = state
        sweep([(j, None), (j - 1, None)])
        return j - 2, live()

    lax.while_loop(cond, body, (qi * nd - 1, live()))
    o_ref[...] = acc_ref[...]


def _sb_prompt(qb, kb, vb, nb, t, bq):
    nq = t // bq
    w = SB_NPAIR * LANES
    return pl.pallas_call(
        functools.partial(_sbp_kernel, bq=bq, npair=SB_NPAIR),
        grid=(nb, SB_WIDTH // w, nq),
        in_specs=[
            pl.BlockSpec((bq, w), lambda b, h, i: (b * nq + i, h)),
            pl.BlockSpec((t, w), lambda b, h, i: (b, h)),
            pl.BlockSpec((t, w), lambda b, h, i: (b, h)),
            _const_spec((2 * SB_BK, 2 * SB_BK)),
        ],
        out_specs=pl.BlockSpec((bq, w), lambda b, h, i: (b * nq + i, h)),
        out_shape=jax.ShapeDtypeStruct((nb * t, SB_WIDTH), F32),
        scratch_shapes=[pltpu.VMEM((bq, w), F32), pltpu.VMEM((2 * SB_NPAIR, bq, LANES), F32)],
        compiler_params=_params("arbitrary", "arbitrary", "arbitrary"),
        name="sb_prompt",
    )(qb, kb, vb, _suffix_matrix())


def _sbs_kernel(q_ref, kn_ref, vn_ref, kt_ref, vt_ref, u_ref, o_ref, alive_ref, acc_ref, run_ref,
                *, tq, span, spb):
    bk = SB_BK
    nr = SB_HEADS * tq
    streams = range(spb)
    srows = lambda s: slice(s * tq, (s + 1) * tq)
    row = lax.broadcasted_iota(jnp.int32, (nr, SB_WIDTH), 0)
    colw = lax.broadcasted_iota(jnp.int32, (nr, SB_WIDTH), 1)
    same_head = (row >> _log2(tq)) == (colw >> _log2(SB_HEAD_DIM))
    qx = [jnp.where(same_head, jnp.concatenate([q_ref[srows(s)]] * SB_HEADS, axis=0), jnp.zeros((), BF16))
          for s in streams]
    u = u_ref[...]
    acc_ref[...] = jnp.zeros_like(acc_ref)
    run_ref[...] = jnp.zeros_like(run_ref)

    def weights(zs, mask):
        parts = [_log_sigmoid_parts(z) for z in zs]
        css = []
        for ls, lk in parts:
            if mask is not None:
                lk = jnp.where(mask, lk, 0.0)
            hi, lo = _split_bf16(lk)
            css.append(jnp.dot(jnp.concatenate([hi, lo], axis=1), u, preferred_element_type=F32))
        out = []
        for s, ((ls, _), cs) in enumerate(zip(parts, css)):
            run = run_ref[s]
            a = jnp.exp(ls + cs[:, :bk] + run)
            if mask is not None:
                a = jnp.where(mask, a, 0.0)
            run_ref[s] = run + cs[:, bk:]
            out.append(a.astype(BF16))
        return out

    pad = jnp.zeros((bk - tq, SB_WIDTH), BF16)
    skey = lax.broadcasted_iota(jnp.int32, (nr, bk), 1)
    tqry = lax.broadcasted_iota(jnp.int32, (nr, bk), 0) & (tq - 1)
    a = weights([lax.dot_general(qx[s], jnp.concatenate([kn_ref[srows(s)], pad], axis=0), _NT,
                                 preferred_element_type=F32) for s in streams], skey < tqry)
    for s in streams:
        acc_ref[s] += jnp.dot(a[s], jnp.concatenate([vn_ref[srows(s)], pad], axis=0),
                              preferred_element_type=F32)

    def live():
        return jnp.max(run_ref[...]) > SB_DEAD

    for j in range(span // bk - 1, -1, -1):
        @pl.when(live())
        def _():
            cols = slice(j * bk, (j + 1) * bk)
            a = weights([jnp.dot(qx[s], kt_ref[s, :, cols].astype(BF16), preferred_element_type=F32)
                         for s in streams], None)
            for s in streams:
                acc_ref[s] += lax.dot_general(a[s], vt_ref[s, :, cols].astype(BF16), _NT,
                                              preferred_element_type=F32)

    alive_ref[...] = jnp.where(live(), jnp.ones(alive_ref.shape, F32), jnp.zeros(alive_ref.shape, F32))
    for s in streams:
        accm = jnp.where(same_head, acc_ref[s], 0.0)
        out = accm[0:tq]
        for h in range(1, SB_HEADS):
            out = out + accm[h * tq:(h + 1) * tq]
        o_ref[srows(s)] = out


def _sb_sample(qb, kb, vb, kcache, vcache, layer, nb, tq, span):
    past = kcache.shape[2]
    assert past % span == 0 and span % SB_BK == 0
    spb = SBS_SPB if nb % SBS_SPB == 0 and span <= SBS_NEAR else 1
    rowspec = pl.BlockSpec((spb * tq, SB_WIDTH), lambda s: (s, 0))
    cspec = pl.BlockSpec((spb, SB_WIDTH, span), lambda s: (layer * nb // spb + s, 0, past // span - 1))
    nr = SB_HEADS * tq
    return pl.pallas_call(
        functools.partial(_sbs_kernel, tq=tq, span=span, spb=spb),
        grid=(nb // spb,),
        in_specs=[rowspec, rowspec, rowspec, cspec, cspec, _const_spec((2 * SB_BK, 2 * SB_BK))],
        out_specs=[rowspec, pl.BlockSpec((spb, 1, LANES), lambda s: (s, 0, 0))],
        out_shape=[jax.ShapeDtypeStruct((nb * tq, SB_WIDTH), F32),
                   jax.ShapeDtypeStruct((nb, 1, LANES), F32)],
        scratch_shapes=[pltpu.VMEM((spb, nr, SB_WIDTH), F32), pltpu.VMEM((spb, nr, SB_BK), F32)],
        compiler_params=_params("arbitrary"),
        name="sb_sample",
    )(qb, kb, vb, kcache, vcache, _suffix_matrix())


def _hgrn_consts(c):
    levels = []
    m = 1
    while m < c:
        levels.append(m)
        m *= 2
    t = np.arange(c)[:, None]
    u = np.arange(c)[None, :]
    mats = [(u <= t).astype(np.float32)]
    masks = [np.eye(c, dtype=np.float32)]
    for m in levels:
        end_a = (t // (2 * m)) * 2 * m + m - 1
        in_b = (t % (2 * m)) >= m
        w = np.where(in_b, (u > end_a) & (u <= t), False).astype(np.float32)
        w = w - np.where(~in_b, (u > t) & (u <= end_a), False).astype(np.float32)
        mats.append(w)
        masks.append((((t // (2 * m)) == (u // (2 * m))) & in_b & ((u % (2 * m)) < m)).astype(np.float32))
    w = np.concatenate(mats, axis=0)
    return tuple(levels), jnp.asarray(np.concatenate([w, w], axis=1), dtype=BF16), jnp.asarray(np.stack(masks))


def _hgrn_kernel(hg_ref, gain_ref, s0_ref, w_ref, mk_ref, *refs, c, levels, slab, nsteps, spb, cps):
    o_ref, sfin_ref, st_ref = refs[-3:]
    step = pl.program_id(1)
    hd = HG_HEAD_DIM
    assert spb == 1 or cps == 1
    groups = range(spb * cps)
    units = [(g, h) for g in groups for h in range(HG_HEADS)]
    nu = range(len(units))
    slot = lambda g, h: (g if cps == 1 else 0) * HG_HEADS + h

    @pl.when(step == 0)
    def _():
        for s in range(spb):
            for h in range(HG_HEADS):
                st_ref[slot(s, h)] = s0_ref[s, h].T

    rows = lambda g: slice(g * c, (g + 1) * c)
    col = lambda base, g, h: hg_ref[rows(g), base * HG_WIDTH + h * hd:base * HG_WIDTH + (h + 1) * hd]
    w = w_ref[...]
    dd = [jnp.dot(w, jnp.concatenate([hg_ref[rows(g), 3 * HG_WIDTH:4 * HG_WIDTH],
                                      hg_ref[rows(g), 4 * HG_WIDTH:5 * HG_WIDTH]], axis=0),
                  preferred_element_type=F32) for g in groups]
    hsl = lambda h: slice(h * hd, (h + 1) * hd)
    q, k, v = ([col(b, g, h) for g, h in units] for b in range(3))
    cum = [dd[g][0:c, hsl(h)] for g, h in units]

    sc = [lax.dot_general(q[i], k[i], _NT, preferred_element_type=F32) * mk_ref[0] for i in nu]
    for li in range(len(levels)):
        for i, (g, h) in enumerate(units):
            e = _exp2_neg_abs(dd[g][(li + 1) * c:(li + 2) * c, hsl(h)]).astype(BF16)
            sc[i] = sc[i] + lax.dot_general(q[i] * e, k[i] * e, _NT,
                                            preferred_element_type=F32) * mk_ref[li + 1]
    st = {}
    o = [None] * len(units)
    for i, (g, h) in enumerate(units):
        sl_ = slot(g, h)
        cur = st.get(sl_)
        if cur is None:
            cur = st_ref[sl_]
        o[i] = lax.dot_general(q[i] * jnp.exp2(cum[i]).astype(BF16), cur.astype(BF16), _NT,
                               preferred_element_type=F32)
        last = cum[i][c - 1:c, :]
        kdec = k[i] * jnp.exp2(last - cum[i]).astype(BF16)
        st[sl_] = cur * jnp.exp2(last) + lax.dot_general(v[i], kdec, _TN, preferred_element_type=F32)
    for sl_, val in st.items():
        st_ref[sl_] = val
    for i in nu:
        o[i] = o[i] + jnp.dot(sc[i].astype(BF16), v[i], preferred_element_type=F32)
    for i, (g, h) in enumerate(units):
        o_ref[rows(g), hsl(h)] = (_rms(o[i], gain_ref[:, hsl(h)]) * col(5, g, h).astype(F32)).astype(BF16)

    @pl.when(step == nsteps - 1)
    def _():
        for d in range(sfin_ref.shape[0]):
            for s in range(spb):
                for h in range(HG_HEADS):
                    sfin_ref[d, s, h] = st_ref[slot(s, h)].T if d == slab else jnp.zeros((hd, hd), F32)


def _hgrn(hg, hg_gain, s0, s0_base, layer, nb, t, s_all):
    c = min(LANES, t)
    nchunks = t // c
    levels, wcat, masks = _hgrn_consts(c)
    nl = len(levels) + 1
    sdims = (HG_HEADS, HG_HEAD_DIM, HG_HEAD_DIM)
    spb = HG_SPB if nchunks == 1 and nb % HG_SPB == 0 and s0_base % HG_SPB == 0 else 1
    cps = HG_CPS if nchunks % HG_CPS == 0 else 1
    nsteps = nchunks // cps
    rb = spb * cps * c
    in_specs = [
        pl.BlockSpec((rb, HG_COLS), lambda b, s: (b * nsteps + s, 0)),
        _const_spec((1, HG_WIDTH)),
        pl.BlockSpec((spb,) + sdims, lambda b, s: (s0_base // spb + b, 0, 0, 0)),
        _const_spec((nl * c, 2 * c)),
        _const_spec((nl, c, c)),
    ]
    args = (hg, hg_gain.reshape(1, HG_WIDTH), s0, wcat, masks)
    if s_all is None:
        s_spec, slab, aliases = pl.BlockSpec((DEPTH, spb) + sdims, lambda b, s: (0, b, 0, 0, 0)), layer, {}
    else:
        s_spec, slab, aliases = pl.BlockSpec((1, spb) + sdims, lambda b, s: (layer, b, 0, 0, 0)), 0, {5: 1}
        in_specs.append(pl.BlockSpec(memory_space=pl.ANY))
        args += (s_all,)
    return pl.pallas_call(
        functools.partial(_hgrn_kernel, c=c, levels=levels, slab=slab, nsteps=nsteps, spb=spb, cps=cps),
        grid=(nb // spb, nsteps),
        in_specs=in_specs,
        out_specs=[pl.BlockSpec((rb, HG_WIDTH), lambda b, s: (b * nsteps + s, 0)), s_spec],
        out_shape=[
            jax.ShapeDtypeStruct((nb * t, HG_WIDTH), BF16),
            jax.ShapeDtypeStruct((DEPTH, nb) + sdims, F32),
        ],
        input_output_aliases=aliases,
        scratch_shapes=[pltpu.VMEM((spb * HG_HEADS, HG_HEAD_DIM, HG_HEAD_DIM), F32)],
        compiler_params=_params("arbitrary", "arbitrary"),
        name="hgrn",
    )(*args)


def _mix_kernel(x_ref, osb_ref, ohg_ref, sbg_ref, wout_ref, nmq_ref, wmq_ref, *refs):
    x1_ref, out_ref = refs[-2:]
    a = _rms(osb_ref[...], sbg_ref[...]).astype(BF16)
    b = ohg_ref[...]
    x1 = (x_ref[...]
          + jnp.dot(a, wout_ref[0:SB_WIDTH, :], preferred_element_type=F32)
          + jnp.dot(b, wout_ref[SB_WIDTH:, :], preferred_element_type=F32))
    x1_ref[...] = x1
    h = _rms(x1, nmq_ref[...]).astype(BF16)
    qm = jnp.dot(h, wmq_ref[...], preferred_element_type=F32).astype(BF16)
    if len(refs) == 2:
        out_ref[...] = qm
    else:
        _mem_attention(lambda s, sl: qm[:, sl], refs[0], refs[1], out_ref, spb=1, tm=qm.shape[0])


def _mix(x2d, osb, ohg, sb_gain, w_out, norm_mem_q, w_mq, tm, mem=None):
    n = x2d.shape[0]
    row = lambda w: pl.BlockSpec((tm, w), lambda i: (i, 0))
    in_specs = [row(D_MODEL), row(SB_WIDTH), row(HG_WIDTH), _const_spec((1, SB_WIDTH)),
                _const_spec((D_MODEL, D_MODEL)), _const_spec((1, D_MODEL)),
                _const_spec((D_MODEL, MEM_WIDTH))]
    args = (x2d, osb, ohg, sb_gain.reshape(1, SB_WIDTH), w_out, norm_mem_q.reshape(1, D_MODEL), w_mq)
    if mem is not None:
        mk3, mv3, base, t = mem
        assert t % tm == 0
        mspec = pl.BlockSpec((1, N_MEM, MEM_HEADS, MEM_HEAD_DIM), lambda i: (base + i // (t // tm), 0, 0, 0))
        in_specs += [mspec, mspec]
        args += (mk3, mv3)
    return pl.pallas_call(
        _mix_kernel,
        grid=(n // tm,),
        in_specs=in_specs,
        out_specs=[row(D_MODEL), row(MEM_WIDTH)],
        out_shape=[jax.ShapeDtypeStruct((n, D_MODEL), F32), jax.ShapeDtypeStruct((n, MEM_WIDTH), BF16)],
        compiler_params=_params("arbitrary"),
        name="mix",
    )(*args)


def _mem_attention(q_of, mk_ref, mv_ref, o_ref, *, spb, tm):
    hd = MEM_HEAD_DIM
    mk = [mk_ref[s].reshape(N_MEM, MEM_WIDTH).astype(BF16) for s in range(spb)]
    mv = [mv_ref[s].reshape(N_MEM, MEM_WIDTH).astype(BF16) for s in range(spb)]
    units = [(s, h) for s in range(spb) for h in range(MEM_HEADS)]
    rows = lambda s: slice(s * tm, (s + 1) * tm)
    sl = lambda h: slice(h * hd, (h + 1) * hd)
    sc = [lax.dot_general(q_of(s, sl(h)), mk[s][:, sl(h)], _NT,
                          preferred_element_type=F32) * (hd ** -0.5) for s, h in units]
    p = [jnp.exp(x - jnp.max(x, axis=-1, keepdims=True)) for x in sc]
    den = [jnp.sum(x, axis=-1, keepdims=True) for x in p]
    o = [jnp.dot(x.astype(BF16), mv[s][:, sl(h)], preferred_element_type=F32) for x, (s, h) in zip(p, units)]
    for x, d, (s, h) in zip(o, den, units):
        o_ref[rows(s), sl(h)] = (x / d).astype(BF16)


def _memattn_kernel(q_ref, mk_ref, mv_ref, o_ref, *, spb, tm):
    _mem_attention(lambda s, sl: q_ref[s * tm:(s + 1) * tm, sl], mk_ref, mv_ref, o_ref, spb=spb, tm=tm)


def _memattn(qm, mk3, mv3, base, nb, t, tm):
    nt = t // tm
    spb = MEM_SPB if nt == 1 and nb % MEM_SPB == 0 and base % MEM_SPB == 0 else 1
    mspec = pl.BlockSpec((spb, N_MEM, MEM_HEADS, MEM_HEAD_DIM), lambda b, i: (base // spb + b, 0, 0, 0))
    qspec = pl.BlockSpec((spb * tm, MEM_WIDTH), lambda b, i: (b * nt + i, 0))
    return pl.pallas_call(
        functools.partial(_memattn_kernel, spb=spb, tm=tm),
        grid=(nb // spb, nt),
        in_specs=[qspec, mspec, mspec],
        out_specs=qspec,
        out_shape=jax.ShapeDtypeStruct((nb * t, MEM_WIDTH), BF16),
        compiler_params=_params("arbitrary", "arbitrary"),
        name="memattn",
    )(qm, mk3, mv3)


FF_CHUNK = 1024


def _ffn_kernel(x1_ref, om_ref, wmo_ref, nf_ref, w1_ref, w2_ref, nfin_ref, o_ref, *, final):
    x2 = x1_ref[...] + jnp.dot(om_ref[...], wmo_ref[...], preferred_element_type=F32)
    h = _rms(x2, nf_ref[...]).astype(BF16)
    x3 = x2
    for c in range(D_FF // FF_CHUNK):
        cs = slice(c * FF_CHUNK, (c + 1) * FF_CHUNK)
        r = jnp.maximum(jnp.dot(h, w1_ref[:, cs], preferred_element_type=F32), 0.0)
        x3 = x3 + jnp.dot((r * r).astype(BF16), w2_ref[cs, :], preferred_element_type=F32)
    o_ref[...] = _rms(x3, nfin_ref[...]) if final else x3


def _ffn(x1, om, w_mo, norm_ffn, w1, w2, norm_final, final, tm):
    n = x1.shape[0]
    row = lambda w: pl.BlockSpec((tm, w), lambda i: (i, 0))
    return pl.pallas_call(
        functools.partial(_ffn_kernel, final=final),
        grid=(n // tm,),
        in_specs=[row(D_MODEL), row(MEM_WIDTH), _const_spec((MEM_WIDTH, D_MODEL)),
                  _const_spec((1, D_MODEL)), _const_spec((D_MODEL, D_FF)), _const_spec((D_FF, D_MODEL)),
                  _const_spec((1, D_MODEL))],
        out_specs=row(D_MODEL),
        out_shape=jax.ShapeDtypeStruct((n, D_MODEL), F32),
        compiler_params=_params("arbitrary"),
        name="ffn",
    )(x1, om, w_mo, norm_ffn.reshape(1, D_MODEL), w1, w2, norm_final.reshape(1, D_MODEL))


TM = 512
TM_FFN = 1024
MEM_SPB = 8
SBS_SPB = 8
HG_SPB = 8
HG_CPS = 4
SB_BQ = 256
SBS_NEAR = 256


def kernel(x_prompt, x_sample, mem_prompt, cache_sb_k, cache_sb_v, state_hgrn, cache_mem_k, cache_mem_v,
           lb_logits, norm_mix, w_in, sb_gain, hg_gain, w_out, norm_mem_q, norm_mem_kv, w_mq, w_mk, w_mv,
           w_mo, norm_ffn, w_ffn1, w_ffn2, norm_final):
    nbp, tp, _ = x_prompt.shape
    nbs, ts, _ = x_sample.shape
    past = cache_sb_k.shape[2]
    w_in_b, w_out_b, w_mq_b, w_mk_b, w_mv_b, w_mo_b, w1_b, w2_b = (
        w.astype(BF16) for w in (w_in, w_out, w_mq, w_mk, w_mv, w_mo, w_ffn1, w_ffn2))
    lbl = lb_logits.astype(F32)

    mk_p, mv_p = _memkv(mem_prompt.reshape(nbp * N_MEM, D_MODEL), norm_mem_kv, w_mk_b, w_mv_b)
    mdims = (N_MEM, MEM_HEADS, MEM_HEAD_DIM)
    mk_p3 = mk_p.reshape((DEPTH * nbp,) + mdims)
    mv_p3 = mv_p.reshape((DEPTH * nbp,) + mdims)
    mk_s3 = cache_mem_k.reshape((DEPTH * nbs,) + mdims)
    mv_s3 = cache_mem_v.reshape((DEPTH * nbs,) + mdims)
    s0_s = state_hgrn.reshape(DEPTH * nbs, HG_HEADS, HG_HEAD_DIM, HG_HEAD_DIM)
    s0_p = jnp.zeros((nbp, HG_HEADS, HG_HEAD_DIM, HG_HEAD_DIM), F32)
    near = min(SBS_NEAR, past)
    kc4 = cache_sb_k.transpose(0, 1, 3, 4, 2).reshape(DEPTH * nbs, SB_WIDTH, past)
    vc4 = cache_sb_v.transpose(0, 1, 3, 4, 2).reshape(DEPTH * nbs, SB_WIDTH, past)

    def sb_sample(l, qb, kb, vb, nb, t):
        o_near, alive = _sb_sample(qb, kb, vb, kc4, vc4, l, nb, t, near)
        if near == past:
            return o_near
        return lax.cond(jnp.max(alive) > 0.0,
                        lambda: _sb_sample(qb, kb, vb, kc4, vc4, l, nb, t, past)[0],
                        lambda: o_near)

    def layer(l, x2d, nb, t, prompt, carried):
        kv_all, s_all = carried
        qb, k_all, v_all, kb, vb, hg = _proj(x2d, norm_mix[l], w_in_b[l], lbl, TM, l, kv_all, nb, t)
        if prompt:
            osb = _sb_prompt(qb, kb, vb, nb, t, SB_BQ)
            ohg, s_all = _hgrn(hg, hg_gain[l], s0_p, 0, l, nb, t, s_all)
            mk3, mv3, base = mk_p3, mv_p3, l * nb
        else:
            osb = sb_sample(l, qb, kb, vb, nb, t)
            ohg, s_all = _hgrn(hg, hg_gain[l], s0_s, l * nb, l, nb, t, s_all)
            mk3, mv3, base = mk_s3, mv_s3, l * nb
        if t % TM == 0:
            x1, om = _mix(x2d, osb, ohg, sb_gain[l], w_out_b[l], norm_mem_q[l], w_mq_b[l], TM,
                          mem=(mk3, mv3, base, t))
        else:
            x1, qm = _mix(x2d, osb, ohg, sb_gain[l], w_out_b[l], norm_mem_q[l], w_mq_b[l], TM)
            om = _memattn(qm, mk3, mv3, base, nb, t, min(TM, t))
        xo = _ffn(x1, om, w_mo_b[l], norm_ffn[l], w1_b[l], w2_b[l], norm_final, l == DEPTH - 1,
                  min(TM_FFN, nb * t))
        return xo, ((k_all, v_all), s_all)

    xp = x_prompt.reshape(nbp * tp, D_MODEL)
    xs = x_sample.reshape(nbs * ts, D_MODEL)
    out_p = out_s = (None, None)
    for l in range(DEPTH):
        xp, out_p = layer(l, xp, nbp, tp, True, out_p)
        xs, out_s = layer(l, xs, nbs, ts, False, out_s)
    (kp, vp), sp = out_p
    (kn, vn), sn = out_s

    def heads_last(a):
        return a.reshape(DEPTH, nbp, SB_HEADS, SB_HEAD_DIM, tp).transpose(0, 1, 4, 2, 3)

    sbs = (DEPTH, nbs, ts, SB_HEADS, SB_HEAD_DIM)
    memp = (DEPTH, nbp, N_MEM, MEM_HEADS, MEM_HEAD_DIM)
    return (xp.reshape(nbp, tp, D_MODEL), xs.reshape(nbs, ts, D_MODEL),
            heads_last(kp), heads_last(vp), sp, mk_p3.reshape(memp), mv_p3.reshape(memp),
            kn.reshape(sbs), vn.reshape(sbs), sn)
```

```python
import functools

import jax
import jax.numpy as jnp
import numpy as np
from jax import lax
from jax.experimental import pallas as pl
from jax.experimental.pallas import tpu as pltpu

D_MODEL = 1024
DEPTH = 2
SB_HEADS = 8
SB_HEAD_DIM = 64
SB_WIDTH = SB_HEADS * SB_HEAD_DIM
HG_HEADS = 4
HG_HEAD_DIM = 128
HG_WIDTH = HG_HEADS * HG_HEAD_DIM
IN_COLS = 3 * SB_WIDTH + 4 * HG_WIDTH
HG_COLS = 6 * HG_WIDTH
N_MEM = 256
MEM_HEADS = 4
MEM_HEAD_DIM = 128
MEM_WIDTH = MEM_HEADS * MEM_HEAD_DIM
D_FF = 4 * D_MODEL
EPS = 1e-6

LANES = 128
VMEM_LIMIT = 56 * 1024 * 1024
F32 = jnp.float32
BF16 = jnp.bfloat16

_NT = (((1,), (1,)), ((), ()))
_TN = (((0,), (0,)), ((), ()))


def _params(*sem):
    return pltpu.CompilerParams(dimension_semantics=sem, vmem_limit_bytes=VMEM_LIMIT)


def _const_spec(shape):
    nd = len(shape)
    return pl.BlockSpec(shape, lambda *_: (0,) * nd, pipeline_mode=pl.Buffered(1))


def _log2(n):
    assert n > 0 and n & (n - 1) == 0, n
    return n.bit_length() - 1


def _layer_spec(shape, layer):
    nd = len(shape)
    return pl.BlockSpec((None,) + tuple(shape), lambda *_: (layer,) + (0,) * nd,
                        pipeline_mode=pl.Buffered(1))


def _rms(x, gain):
    ms = jnp.mean(x * x, axis=-1, keepdims=True)
    return x * lax.rsqrt(ms + EPS) * gain


def _log_sigmoid_parts(z):
    l = jnp.log(1.0 + jnp.exp(-jnp.abs(z)))
    ls = jnp.minimum(z, 0.0) - l
    return ls, ls - z


LOG2E = 1.4426950408889634


def _exp2_neg_abs(x):
    return jnp.exp2(-jnp.abs(x))


def _split_bf16(x):
    hi = x.astype(BF16)
    lo = (x - hi.astype(F32)).astype(BF16)
    return hi, lo


def _memkv_kernel(mem_ref, g_ref, wk_ref, wv_ref, mk_ref, mv_ref):
    h = _rms(mem_ref[...], g_ref[0]).astype(BF16)
    mk_ref[0] = jnp.dot(h, wk_ref[0], preferred_element_type=F32)
    mv_ref[0] = jnp.dot(h, wv_ref[0], preferred_element_type=F32)


def _memkv(mem2d, gains, wk, wv):
    n = mem2d.shape[0]
    out = jax.ShapeDtypeStruct((DEPTH, n, MEM_WIDTH), F32)
    return pl.pallas_call(
        _memkv_kernel,
        grid=(DEPTH,),
        in_specs=[
            pl.BlockSpec((n, D_MODEL), lambda l: (0, 0)),
            pl.BlockSpec((1, 1, D_MODEL), lambda l: (l, 0, 0)),
            pl.BlockSpec((1, D_MODEL, MEM_WIDTH), lambda l: (l, 0, 0)),
            pl.BlockSpec((1, D_MODEL, MEM_WIDTH), lambda l: (l, 0, 0)),
        ],
        out_specs=[pl.BlockSpec((1, n, MEM_WIDTH), lambda l: (l, 0, 0))] * 2,
        out_shape=[out, out],
        compiler_params=_params("arbitrary"),
        name="memkv",
    )(mem2d, gains.reshape(DEPTH, 1, D_MODEL), wk, wv)


def _proj_kernel(x_ref, g_ref, w_ref, lbl_ref, *refs, layer, slab, time_minor):
    qsb_ref, ksb_ref, vsb_ref, kbf_ref, vbf_ref, hg_ref = refs[-6:]
    h = _rms(x_ref[...], g_ref[...]).astype(BF16)

    def col(c):
        return jnp.dot(h, w_ref[:, c * SB_WIDTH:(c + 1) * SB_WIDTH], preferred_element_type=F32)

    def put(ref, val):
        out = val.T if time_minor else val.reshape(val.shape[0], SB_HEADS, SB_HEAD_DIM)
        for s in range(ref.shape[0]):
            ref[s, 0] = out if s == slab else jnp.zeros_like(out)

    def put_hg(c, val):
        hg_ref[:, c * HG_WIDTH:(c + 1) * HG_WIDTH] = val.astype(BF16)

    lg = lbl_ref[...]
    ex = jnp.exp(lg - jnp.max(lg, axis=0, keepdims=True))
    sm = ex / jnp.sum(ex, axis=0, keepdims=True)
    csum = sm[0:1]
    for l in range(1, layer + 1):
        csum = csum + sm[l:l + 1]
    lb = csum - sm[0:1]

    zf = col(4)
    ez = jnp.exp(-jnp.abs(zf))
    lsig = jnp.minimum(zf, 0.0) - jnp.log(1.0 + ez)
    bterm = jnp.log(1.0 - lb) + lsig
    la = jnp.log(jnp.maximum(lb, 1e-30))
    logf_mix = jnp.maximum(la, bterm) + jnp.log(1.0 + jnp.exp(-jnp.abs(la - bterm)))
    logf = jnp.where(lb > 0.0, logf_mix, bterm)
    put_hg(1, (1.0 - lb) * (jnp.where(zf >= 0.0, ez, 1.0) / (1.0 + ez)))
    hi, lo = _split_bf16(logf * LOG2E)
    put_hg(3, hi)
    put_hg(4, lo)
    g = col(6)
    eg = jnp.exp(-jnp.abs(g))
    put_hg(5, g * (jnp.where(g >= 0.0, 1.0, eg) / (1.0 + eg)))
    put_hg(0, col(3) * (HG_HEAD_DIM ** -0.5))
    put_hg(2, col(5))

    qsb_ref[...] = (col(0) * (SB_HEAD_DIM ** -0.5)).astype(BF16)
    k = col(1)
    put(ksb_ref, k)
    kbf_ref[...] = k.astype(BF16)
    v = col(2)
    put(vsb_ref, v)
    vbf_ref[...] = v.astype(BF16)


def _proj(x2d, gain, w_in, lb_logits, tm, layer, kv_all, nb, t):
    n = x2d.shape[0]
    row = lambda w: pl.BlockSpec((tm, w), lambda i: (i, 0))
    in_specs = [row(D_MODEL), _const_spec((1, D_MODEL)), _layer_spec((D_MODEL, IN_COLS), layer),
                _const_spec((DEPTH, HG_WIDTH))]
    args = (x2d, gain.reshape(1, D_MODEL), w_in, lb_logits)
    time_minor = t % tm == 0
    if time_minor:
        nq = t // tm
        kv_shape, blk = (DEPTH, nb, SB_WIDTH, t), (1, SB_WIDTH, tm)
        where = lambda i: (i // nq, 0, i % nq)
    else:
        kv_shape, blk = (DEPTH, n // tm, tm, SB_HEADS, SB_HEAD_DIM), (1, tm, SB_HEADS, SB_HEAD_DIM)
        where = lambda i: (i, 0, 0, 0)
    if kv_all is None:
        kv_spec, slab, aliases = pl.BlockSpec((DEPTH,) + blk, lambda i: (0,) + where(i)), layer, {}
    else:
        kv_spec, slab, aliases = pl.BlockSpec((1,) + blk, lambda i: (layer,) + where(i)), 0, {4: 1, 5: 2}
        in_specs += [pl.BlockSpec(memory_space=pl.ANY)] * 2
        args += tuple(kv_all)
    return pl.pallas_call(
        functools.partial(_proj_kernel, layer=layer, slab=slab, time_minor=time_minor),
        grid=(n // tm,),
        in_specs=in_specs,
        out_specs=[row(SB_WIDTH), kv_spec, kv_spec, row(SB_WIDTH), row(SB_WIDTH), row(HG_COLS)],
        out_shape=[
            jax.ShapeDtypeStruct((n, SB_WIDTH), BF16),
            jax.ShapeDtypeStruct(kv_shape, F32),
            jax.ShapeDtypeStruct(kv_shape, F32),
            jax.ShapeDtypeStruct((n, SB_WIDTH), BF16),
            jax.ShapeDtypeStruct((n, SB_WIDTH), BF16),
            jax.ShapeDtypeStruct((n, HG_COLS), BF16),
        ],
        input_output_aliases=aliases,
        compiler_params=_params("arbitrary"),
        name="proj",
    )(*args)


SB_BK = LANES
SB_DEAD = -104.0
SB_NPAIR = 4


def _suffix_matrix():
    j = np.arange(SB_BK)[:, None]
    s = np.arange(SB_BK)[None, :]
    u = np.concatenate([(j > s).astype(np.float32), np.ones((SB_BK, SB_BK), np.float32)], axis=1)
    return jnp.asarray(np.concatenate([u, u], axis=0), dtype=BF16)


def _sbp_kernel(q_ref, k_ref, v_ref, u_ref, o_ref, acc_ref, run_ref, *, bq, npair):
    bk = SB_BK
    qi = pl.program_id(2)
    lane = lax.broadcasted_iota(jnp.int32, (1, LANES), 1)
    m_a = (lane < SB_HEAD_DIM).astype(BF16)
    m_b = (lane >= SB_HEAD_DIM).astype(BF16)
    acc_ref[...] = jnp.zeros_like(acc_ref)
    run_ref[...] = jnp.zeros_like(run_ref)
    u = u_ref[...]
    nd = bq // bk
    assert nd % 2 == 0, "earlier key blocks are taken two per trip"

    def sweep(blocks):
        units = [(j, rel, p) for j, rel in blocks for p in range(npair)]
        ps = lambda p: slice(p * LANES, (p + 1) * LANES)
        keys = lambda j: pl.ds(pl.multiple_of(j * bk, bk), bk)
        qrows = lambda rel: slice(0 if rel is None else rel, bq)
        masks = {rel: (lax.broadcasted_iota(jnp.int32, (bq - rel, bk), 1)
                       < lax.broadcasted_iota(jnp.int32, (bq - rel, bk), 0))
                 for _, rel in blocks if rel is not None}
        zs = []
        for j, rel, p in units:
            ks = k_ref[keys(j), ps(p)]
            kcat = jnp.concatenate([ks * m_a, ks * m_b], axis=0)
            zs.append(lax.dot_general(q_ref[qrows(rel), ps(p)], kcat, _NT, preferred_element_type=F32))
        lss, css = [], []
        for (j, rel, p), z in zip(units, zs):
            for hh in range(2):
                ls, lk = _log_sigmoid_parts(z[:, hh * bk:(hh + 1) * bk])
                if rel is not None:
                    lk = jnp.where(masks[rel], lk, 0.0)
                hi, lo = _split_bf16(lk)
                lss.append(ls)
                css.append(jnp.dot(jnp.concatenate([hi, lo], axis=1), u, preferred_element_type=F32))
        for i, (j, rel, p) in enumerate(units):
            a_parts = []
            for hh in range(2):
                ls, cs = lss[2 * i + hh], css[2 * i + hh]
                run = run_ref[2 * p + hh, qrows(rel)]
                a = jnp.exp(ls + cs[:, :bk] + run)
                if rel is not None:
                    a = jnp.where(masks[rel], a, 0.0)
                run_ref[2 * p + hh, qrows(rel)] = run + cs[:, bk:]
                a_parts.append(a.astype(BF16))
            vs = v_ref[keys(j), ps(p)]
            vcat = jnp.concatenate([vs * m_a, vs * m_b], axis=0)
            acc_ref[qrows(rel), ps(p)] += jnp.dot(jnp.concatenate(a_parts, axis=1), vcat,
                                                  preferred_element_type=F32)

    sweep([(qi * nd + d, d * bk) for d in range(nd - 1, -1, -1)])

    def live():
        return jnp.max(run_ref[...]) > SB_DEAD

    def cond(state):
        j, alive = state
        return jnp.logical_and(j >= 0, alive)

    def body(state):
        j, _ = state
        sweep([(j, None), (j - 1, None)])
        return j - 2, live()

    lax.while_loop(cond, body, (qi * nd - 1, live()))
    o_ref[...] = acc_ref[...]


def _sb_prompt(qb, kb, vb, nb, t, bq):
    nq = t // bq
    w = SB_NPAIR * LANES
    return pl.pallas_call(
        functools.partial(_sbp_kernel, bq=bq, npair=SB_NPAIR),
        grid=(nb, SB_WIDTH // w, nq),
        in_specs=[
            pl.BlockSpec((bq, w), lambda b, h, i: (b * nq + i, h)),
            pl.BlockSpec((t, w), lambda b, h, i: (b, h)),
            pl.BlockSpec((t, w), lambda b, h, i: (b, h)),
            _const_spec((2 * SB_BK, 2 * SB_BK)),
        ],
        out_specs=pl.BlockSpec((bq, w), lambda b, h, i: (b * nq + i, h)),
        out_shape=jax.ShapeDtypeStruct((nb * t, SB_WIDTH), F32),
        scratch_shapes=[pltpu.VMEM((bq, w), F32), pltpu.VMEM((2 * SB_NPAIR, bq, LANES), F32)],
        compiler_params=_params("arbitrary", "arbitrary", "arbitrary"),
        name="sb_prompt",
    )(qb, kb, vb, _suffix_matrix())


def _sbs_kernel(q_ref, kn_ref, vn_ref, kt_ref, vt_ref, u_ref, o_ref, alive_ref, acc_ref, run_ref,
                *, tq, span, spb):
    bk = SB_BK
    nr = SB_HEADS * tq
    streams = range(spb)
    srows = lambda s: slice(s * tq, (s + 1) * tq)
    row = lax.broadcasted_iota(jnp.int32, (nr, SB_WIDTH), 0)
    colw = lax.broadcasted_iota(jnp.int32, (nr, SB_WIDTH), 1)
    same_head = (row >> _log2(tq)) == (colw >> _log2(SB_HEAD_DIM))
    qx = [jnp.where(same_head, jnp.concatenate([q_ref[srows(s)]] * SB_HEADS, axis=0), jnp.zeros((), BF16))
          for s in streams]
    u = u_ref[...]
    acc_ref[...] = jnp.zeros_like(acc_ref)
    run_ref[...] = jnp.zeros_like(run_ref)

    def weights(zs, mask):
        parts = [_log_sigmoid_parts(z) for z in zs]
        css = []
        for ls, lk in parts:
            if mask is not None:
                lk = jnp.where(mask, lk, 0.0)
            hi, lo = _split_bf16(lk)
            css.append(jnp.dot(jnp.concatenate([hi, lo], axis=1), u, preferred_element_type=F32))
        out = []
        for s, ((ls, _), cs) in enumerate(zip(parts, css)):
            run = run_ref[s]
            a = jnp.exp(ls + cs[:, :bk] + run)
            if mask is not None:
                a = jnp.where(mask, a, 0.0)
            run_ref[s] = run + cs[:, bk:]
            out.append(a.astype(BF16))
        return out

    pad = jnp.zeros((bk - tq, SB_WIDTH), BF16)
    skey = lax.broadcasted_iota(jnp.int32, (nr, bk), 1)
    tqry = lax.broadcasted_iota(jnp.int32, (nr, bk), 0) & (tq - 1)
    a = weights([lax.dot_general(qx[s], jnp.concatenate([kn_ref[srows(s)], pad], axis=0), _NT,
                                 preferred_element_type=F32) for s in streams], skey < tqry)
    for s in streams:
        acc_ref[s] += jnp.dot(a[s], jnp.concatenate([vn_ref[srows(s)], pad], axis=0),
                              preferred_element_type=F32)

    def live():
        return jnp.max(run_ref[...]) > SB_DEAD

    for j in range(span // bk - 1, -1, -1):
        @pl.when(live())
        def _():
            cols = slice(j * bk, (j + 1) * bk)
            a = weights([jnp.dot(qx[s], kt_ref[s, :, cols].astype(BF16), preferred_element_type=F32)
                         for s in streams], None)
            for s in streams:
                acc_ref[s] += lax.dot_general(a[s], vt_ref[s, :, cols].astype(BF16), _NT,
                                              preferred_element_type=F32)

    alive_ref[...] = jnp.where(live(), jnp.ones(alive_ref.shape, F32), jnp.zeros(alive_ref.shape, F32))
    for s in streams:
        accm = jnp.where(same_head, acc_ref[s], 0.0)
        out = accm[0:tq]
        for h in range(1, SB_HEADS):
            out = out + accm[h * tq:(h + 1) * tq]
        o_ref[srows(s)] = out


def _sb_sample(qb, kb, vb, kcache, vcache, layer, nb, tq, span):
    past = kcache.shape[2]
    assert past % span == 0 and span % SB_BK == 0
    spb = SBS_SPB if nb % SBS_SPB == 0 and span <= SBS_NEAR else 1
    rowspec = pl.BlockSpec((spb * tq, SB_WIDTH), lambda s: (s, 0))
    cspec = pl.BlockSpec((spb, SB_WIDTH, span), lambda s: (layer * nb // spb + s, 0, past // span - 1))
    nr = SB_HEADS * tq
    return pl.pallas_call(
        functools.partial(_sbs_kernel, tq=tq, span=span, spb=spb),
        grid=(nb // spb,),
        in_specs=[rowspec, rowspec, rowspec, cspec, cspec, _const_spec((2 * SB_BK, 2 * SB_BK))],
        out_specs=[rowspec, pl.BlockSpec((spb, 1, LANES), lambda s: (s, 0, 0))],
        out_shape=[jax.ShapeDtypeStruct((nb * tq, SB_WIDTH), F32),
                   jax.ShapeDtypeStruct((nb, 1, LANES), F32)],
        scratch_shapes=[pltpu.VMEM((spb, nr, SB_WIDTH), F32), pltpu.VMEM((spb, nr, SB_BK), F32)],
        compiler_params=_params("arbitrary"),
        name="sb_sample",
    )(qb, kb, vb, kcache, vcache, _suffix_matrix())


def _hgrn_consts(c):
    levels = []
    m = 1
    while m < c:
        levels.append(m)
        m *= 2
    t = np.arange(c)[:, None]
    u = np.arange(c)[None, :]
    mats = [(u <= t).astype(np.float32)]
    masks = [np.eye(c, dtype=np.float32)]
    for m in levels:
        end_a = (t // (2 * m)) * 2 * m + m - 1
        in_b = (t % (2 * m)) >= m
        w = np.where(in_b, (u > end_a) & (u <= t), False).astype(np.float32)
        w = w - np.where(~in_b, (u > t) & (u <= end_a), False).astype(np.float32)
        mats.append(w)
        masks.append((((t // (2 * m)) == (u // (2 * m))) & in_b & ((u % (2 * m)) < m)).astype(np.float32))
    w = np.concatenate(mats, axis=0)
    return tuple(levels), jnp.asarray(np.concatenate([w, w], axis=1), dtype=BF16), jnp.asarray(np.stack(masks))


def _hgrn_kernel(hg_ref, gain_ref, s0_ref, w_ref, mk_ref, *refs, c, levels, slab, nsteps, spb, cps):
    o_ref, sfin_ref, st_ref = refs[-3:]
    step = pl.program_id(1)
    hd = HG_HEAD_DIM
    assert spb == 1 or cps == 1
    groups = range(spb * cps)
    units = [(g, h) for g in groups for h in range(HG_HEADS)]
    nu = range(len(units))
    slot = lambda g, h: (g if cps == 1 else 0) * HG_HEADS + h

    @pl.when(step == 0)
    def _():
        for s in range(spb):
            for h in range(HG_HEADS):
                st_ref[slot(s, h)] = s0_ref[s, h].T

    rows = lambda g: slice(g * c, (g + 1) * c)
    col = lambda base, g, h: hg_ref[rows(g), base * HG_WIDTH + h * hd:base * HG_WIDTH + (h + 1) * hd]
    w = w_ref[...]
    dd = [jnp.dot(w, jnp.concatenate([hg_ref[rows(g), 3 * HG_WIDTH:4 * HG_WIDTH],
                                      hg_ref[rows(g), 4 * HG_WIDTH:5 * HG_WIDTH]], axis=0),
                  preferred_element_type=F32) for g in groups]
    hsl = lambda h: slice(h * hd, (h + 1) * hd)
    q, k, v = ([col(b, g, h) for g, h in units] for b in range(3))
    cum = [dd[g][0:c, hsl(h)] for g, h in units]

    sc = [lax.dot_general(q[i], k[i], _NT, preferred_element_type=F32) * mk_ref[0] for i in nu]
    for li in range(len(levels)):
        for i, (g, h) in enumerate(units):
            e = _exp2_neg_abs(dd[g][(li + 1) * c:(li + 2) * c, hsl(h)]).astype(BF16)
            sc[i] = sc[i] + lax.dot_general(q[i] * e, k[i] * e, _NT,
                                            preferred_element_type=F32) * mk_ref[li + 1]
    st = {}
    o = [None] * len(units)
    for i, (g, h) in enumerate(units):
        sl_ = slot(g, h)
        cur = st.get(sl_)
        if cur is None:
            cur = st_ref[sl_]
        o[i] = lax.dot_general(q[i] * jnp.exp2(cum[i]).astype(BF16), cur.astype(BF16), _NT,
                               preferred_element_type=F32)
        last = cum[i][c - 1:c, :]
        kdec = k[i] * jnp.exp2(last - cum[i]).astype(BF16)
        st[sl_] = cur * jnp.exp2(last) + lax.dot_general(v[i], kdec, _TN, preferred_element_type=F32)
    for sl_, val in st.items():
        st_ref[sl_] = val
    for i in nu:
        o[i] = o[i] + jnp.dot(sc[i].astype(BF16), v[i], preferred_element_type=F32)
    for i, (g, h) in enumerate(units):
        o_ref[rows(g), hsl(h)] = (_rms(o[i], gain_ref[:, hsl(h)]) * col(5, g, h).astype(F32)).astype(BF16)

    @pl.when(step == nsteps - 1)
    def _():
        for d in range(sfin_ref.shape[0]):
            for s in range(spb):
                for h in range(HG_HEADS):
                    sfin_ref[d, s, h] = st_ref[slot(s, h)].T if d == slab else jnp.zeros((hd, hd), F32)


def _hgrn(hg, hg_gain, s0, s0_base, layer, nb, t, s_all):
    c = min(LANES, t)
    nchunks = t // c
    levels, wcat, masks = _hgrn_consts(c)
    nl = len(levels) + 1
    sdims = (HG_HEADS, HG_HEAD_DIM, HG_HEAD_DIM)
    spb = HG_SPB if nchunks == 1 and nb % HG_SPB == 0 and s0_base % HG_SPB == 0 else 1
    cps = HG_CPS if nchunks % HG_CPS == 0 else 1
    nsteps = nchunks // cps
    rb = spb * cps * c
    in_specs = [
        pl.BlockSpec((rb, HG_COLS), lambda b, s: (b * nsteps + s, 0)),
        _const_spec((1, HG_WIDTH)),
        pl.BlockSpec((spb,) + sdims, lambda b, s: (s0_base // spb + b, 0, 0, 0)),
        _const_spec((nl * c, 2 * c)),
        _const_spec((nl, c, c)),
    ]
    args = (hg, hg_gain.reshape(1, HG_WIDTH), s0, wcat, masks)
    if s_all is None:
        s_spec, slab, aliases = pl.BlockSpec((DEPTH, spb) + sdims, lambda b, s: (0, b, 0, 0, 0)), layer, {}
    else:
        s_spec, slab, aliases = pl.BlockSpec((1, spb) + sdims, lambda b, s: (layer, b, 0, 0, 0)), 0, {5: 1}
        in_specs.append(pl.BlockSpec(memory_space=pl.ANY))
        args += (s_all,)
    return pl.pallas_call(
        functools.partial(_hgrn_kernel, c=c, levels=levels, slab=slab, nsteps=nsteps, spb=spb, cps=cps),
        grid=(nb // spb, nsteps),
        in_specs=in_specs,
        out_specs=[pl.BlockSpec((rb, HG_WIDTH), lambda b, s: (b * nsteps + s, 0)), s_spec],
        out_shape=[
            jax.ShapeDtypeStruct((nb * t, HG_WIDTH), BF16),
            jax.ShapeDtypeStruct((DEPTH, nb) + sdims, F32),
        ],
        input_output_aliases=aliases,
        scratch_shapes=[pltpu.VMEM((spb * HG_HEADS, HG_HEAD_DIM, HG_HEAD_DIM), F32)],
        compiler_params=_params("arbitrary", "arbitrary"),
        name="hgrn",
    )(*args)


def _mix_kernel(x_ref, osb_ref, ohg_ref, sbg_ref, wout_ref, nmq_ref, wmq_ref, *refs):
    x1_ref, out_ref = refs[-2:]
    a = _rms(osb_ref[...], sbg_ref[...]).astype(BF16)
    b = ohg_ref[...]
    x1 = (x_ref[...]
          + jnp.dot(a, wout_ref[0:SB_WIDTH, :], preferred_element_type=F32)
          + jnp.dot(b, wout_ref[SB_WIDTH:, :], preferred_element_type=F32))
    x1_ref[...] = x1
    h = _rms(x1, nmq_ref[...]).astype(BF16)
    qm = jnp.dot(h, wmq_ref[...], preferred_element_type=F32).astype(BF16)
    if len(refs) == 2:
        out_ref[...] = qm
    else:
        _mem_attention(lambda s, sl: qm[:, sl], refs[0], refs[1], out_ref, spb=1, tm=qm.shape[0])


def _mix(x2d, osb, ohg, sb_gain, w_out, norm_mem_q, w_mq, tm, layer, mem=None):
    n = x2d.shape[0]
    row = lambda w: pl.BlockSpec((tm, w), lambda i: (i, 0))
    in_specs = [row(D_MODEL), row(SB_WIDTH), row(HG_WIDTH), _const_spec((1, SB_WIDTH)),
                _layer_spec((D_MODEL, D_MODEL), layer), _const_spec((1, D_MODEL)),
                _layer_spec((D_MODEL, MEM_WIDTH), layer)]
    args = (x2d, osb, ohg, sb_gain.reshape(1, SB_WIDTH), w_out, norm_mem_q.reshape(1, D_MODEL), w_mq)
    if mem is not None:
        mk3, mv3, base, t = mem
        assert t % tm == 0
        mspec = pl.BlockSpec((1, N_MEM, MEM_HEADS, MEM_HEAD_DIM), lambda i: (base + i // (t // tm), 0, 0, 0))
        in_specs += [mspec, mspec]
        args += (mk3, mv3)
    return pl.pallas_call(
        _mix_kernel,
        grid=(n // tm,),
        in_specs=in_specs,
        out_specs=[row(D_MODEL), row(MEM_WIDTH)],
        out_shape=[jax.ShapeDtypeStruct((n, D_MODEL), F32), jax.ShapeDtypeStruct((n, MEM_WIDTH), BF16)],
        compiler_params=_params("arbitrary"),
        name="mix",
    )(*args)


def _mem_attention(q_of, mk_ref, mv_ref, o_ref, *, spb, tm):
    hd = MEM_HEAD_DIM
    mk = [mk_ref[s].reshape(N_MEM, MEM_WIDTH).astype(BF16) for s in range(spb)]
    mv = [mv_ref[s].reshape(N_MEM, MEM_WIDTH).astype(BF16) for s in range(spb)]
    units = [(s, h) for s in range(spb) for h in range(MEM_HEADS)]
    rows = lambda s: slice(s * tm, (s + 1) * tm)
    sl = lambda h: slice(h * hd, (h + 1) * hd)
    sc = [lax.dot_general(q_of(s, sl(h)), mk[s][:, sl(h)], _NT,
                          preferred_element_type=F32) * (hd ** -0.5) for s, h in units]
    p = [jnp.exp(x - jnp.max(x, axis=-1, keepdims=True)) for x in sc]
    den = [jnp.sum(x, axis=-1, keepdims=True) for x in p]
    o = [jnp.dot(x.astype(BF16), mv[s][:, sl(h)], preferred_element_type=F32) for x, (s, h) in zip(p, units)]
    for x, d, (s, h) in zip(o, den, units):
        o_ref[rows(s), sl(h)] = (x / d).astype(BF16)


def _memattn_kernel(q_ref, mk_ref, mv_ref, o_ref, *, spb, tm):
    _mem_attention(lambda s, sl: q_ref[s * tm:(s + 1) * tm, sl], mk_ref, mv_ref, o_ref, spb=spb, tm=tm)


def _memattn(qm, mk3, mv3, base, nb, t, tm):
    nt = t // tm
    spb = MEM_SPB if nt == 1 and nb % MEM_SPB == 0 and base % MEM_SPB == 0 else 1
    mspec = pl.BlockSpec((spb, N_MEM, MEM_HEADS, MEM_HEAD_DIM), lambda b, i: (base // spb + b, 0, 0, 0))
    qspec = pl.BlockSpec((spb * tm, MEM_WIDTH), lambda b, i: (b * nt + i, 0))
    return pl.pallas_call(
        functools.partial(_memattn_kernel, spb=spb, tm=tm),
        grid=(nb // spb, nt),
        in_specs=[qspec, mspec, mspec],
        out_specs=qspec,
        out_shape=jax.ShapeDtypeStruct((nb * t, MEM_WIDTH), BF16),
        compiler_params=_params("arbitrary", "arbitrary"),
        name="memattn",
    )(qm, mk3, mv3)


FF_CHUNK = 1024


def _ffn_kernel(x1_ref, om_ref, wmo_ref, nf_ref, w1_ref, w2_ref, nfin_ref, o_ref, *, final):
    x2 = x1_ref[...] + jnp.dot(om_ref[...], wmo_ref[...], preferred_element_type=F32)
    h = _rms(x2, nf_ref[...]).astype(BF16)
    x3 = x2
    for c in range(D_FF // FF_CHUNK):
        cs = slice(c * FF_CHUNK, (c + 1) * FF_CHUNK)
        r = jnp.maximum(jnp.dot(h, w1_ref[:, cs], preferred_element_type=F32), 0.0)
        x3 = x3 + jnp.dot((r * r).astype(BF16), w2_ref[cs, :], preferred_element_type=F32)
    o_ref[...] = _rms(x3, nfin_ref[...]) if final else x3


def _ffn(x1, om, w_mo, norm_ffn, w1, w2, norm_final, layer, final, tm):
    n = x1.shape[0]
    row = lambda w: pl.BlockSpec((tm, w), lambda i: (i, 0))
    return pl.pallas_call(
        functools.partial(_ffn_kernel, final=final),
        grid=(n // tm,),
        in_specs=[row(D_MODEL), row(MEM_WIDTH), _layer_spec((MEM_WIDTH, D_MODEL), layer),
                  _const_spec((1, D_MODEL)), _layer_spec((D_MODEL, D_FF), layer),
                  _layer_spec((D_FF, D_MODEL), layer),
                  _const_spec((1, D_MODEL))],
        out_specs=row(D_MODEL),
        out_shape=jax.ShapeDtypeStruct((n, D_MODEL), F32),
        compiler_params=_params("arbitrary"),
        name="ffn",
    )(x1, om, w_mo, norm_ffn.reshape(1, D_MODEL), w1, w2, norm_final.reshape(1, D_MODEL))


TM = 512
TM_FFN = 1024
MEM_SPB = 8
SBS_SPB = 8
HG_SPB = 8
HG_CPS = 4
SB_BQ = 256
SBS_NEAR = 256


def kernel(x_prompt, x_sample, mem_prompt, cache_sb_k, cache_sb_v, state_hgrn, cache_mem_k, cache_mem_v,
           lb_logits, norm_mix, w_in, sb_gain, hg_gain, w_out, norm_mem_q, norm_mem_kv, w_mq, w_mk, w_mv,
           w_mo, norm_ffn, w_ffn1, w_ffn2, norm_final):
    nbp, tp, _ = x_prompt.shape
    nbs, ts, _ = x_sample.shape
    past = cache_sb_k.shape[2]
    w_in_b, w_out_b, w_mq_b, w_mk_b, w_mv_b, w_mo_b, w1_b, w2_b = (
        w.astype(BF16) for w in (w_in, w_out, w_mq, w_mk, w_mv, w_mo, w_ffn1, w_ffn2))
    lbl = lb_logits.astype(F32)

    mk_p, mv_p = _memkv(mem_prompt.reshape(nbp * N_MEM, D_MODEL), norm_mem_kv, w_mk_b, w_mv_b)
    mdims = (N_MEM, MEM_HEADS, MEM_HEAD_DIM)
    mk_p3 = mk_p.reshape((DEPTH * nbp,) + mdims)
    mv_p3 = mv_p.reshape((DEPTH * nbp,) + mdims)
    mk_s3 = cache_mem_k.reshape((DEPTH * nbs,) + mdims)
    mv_s3 = cache_mem_v.reshape((DEPTH * nbs,) + mdims)
    s0_s = state_hgrn.reshape(DEPTH * nbs, HG_HEADS, HG_HEAD_DIM, HG_HEAD_DIM)
    s0_p = jnp.zeros((nbp, HG_HEADS, HG_HEAD_DIM, HG_HEAD_DIM), F32)
    near = min(SBS_NEAR, past)
    kc4 = cache_sb_k.transpose(0, 1, 3, 4, 2).reshape(DEPTH * nbs, SB_WIDTH, past)
    vc4 = cache_sb_v.transpose(0, 1, 3, 4, 2).reshape(DEPTH * nbs, SB_WIDTH, past)

    def sb_sample(l, qb, kb, vb, nb, t):
        o_near, alive = _sb_sample(qb, kb, vb, kc4, vc4, l, nb, t, near)
        if near == past:
            return o_near
        return lax.cond(jnp.max(alive) > 0.0,
                        lambda: _sb_sample(qb, kb, vb, kc4, vc4, l, nb, t, past)[0],
                        lambda: o_near)

    def layer(l, x2d, nb, t, prompt, carried):
        kv_all, s_all = carried
        qb, k_all, v_all, kb, vb, hg = _proj(x2d, norm_mix[l], w_in_b, lbl, TM, l, kv_all, nb, t)
        if prompt:
            osb = _sb_prompt(qb, kb, vb, nb, t, SB_BQ)
            ohg, s_all = _hgrn(hg, hg_gain[l], s0_p, 0, l, nb, t, s_all)
            mk3, mv3, base = mk_p3, mv_p3, l * nb
        else:
            osb = sb_sample(l, qb, kb, vb, nb, t)
            ohg, s_all = _hgrn(hg, hg_gain[l], s0_s, l * nb, l, nb, t, s_all)
            mk3, mv3, base = mk_s3, mv_s3, l * nb
        if t % TM == 0:
            x1, om = _mix(x2d, osb, ohg, sb_gain[l], w_out_b, norm_mem_q[l], w_mq_b, TM, l,
                          mem=(mk3, mv3, base, t))
        else:
            x1, qm = _mix(x2d, osb, ohg, sb_gain[l], w_out_b, norm_mem_q[l], w_mq_b, TM, l)
            om = _memattn(qm, mk3, mv3, base, nb, t, min(TM, t))
        xo = _ffn(x1, om, w_mo_b, norm_ffn[l], w1_b, w2_b, norm_final, l, l == DEPTH - 1,
                  min(TM_FFN, nb * t))
        return xo, ((k_all, v_all), s_all)

    xp = x_prompt.reshape(nbp * tp, D_MODEL)
    xs = x_sample.reshape(nbs * ts, D_MODEL)
    out_p = out_s = (None, None)
    for l in range(DEPTH):
        xp, out_p = layer(l, xp, nbp, tp, True, out_p)
        xs, out_s = layer(l, xs, nbs, ts, False, out_s)
    (kp, vp), sp = out_p
    (kn, vn), sn = out_s

    def heads_last(a):
        return a.reshape(DEPTH, nbp, SB_HEADS, SB_HEAD_DIM, tp).transpose(0, 1, 4, 2, 3)

    sbs = (DEPTH, nbs, ts, SB_HEADS, SB_HEAD_DIM)
    memp = (DEPTH, nbp, N_MEM, MEM_HEADS, MEM_HEAD_DIM)
    return (xp.reshape(nbp, tp, D_MODEL), xs.reshape(nbs, ts, D_MODEL),
            heads_last(kp), heads_last(vp), sp, mk_p3.reshape(memp), mv_p3.reshape(memp),
            kn.reshape(sbs), vn.reshape(sbs), sn)
```

```python
import functools

import jax
import jax.numpy as jnp
import numpy as np
from jax import lax
from jax.experimental import pallas as pl
from jax.experimental.pallas import tpu as pltpu

D_MODEL = 1024
DEPTH = 2
SB_HEADS = 8
SB_HEAD_DIM = 64
SB_WIDTH = SB_HEADS * SB_HEAD_DIM
HG_HEADS = 4
HG_HEAD_DIM = 128
HG_WIDTH = HG_HEADS * HG_HEAD_DIM
IN_COLS = 3 * SB_WIDTH + 4 * HG_WIDTH
HG_COLS = 6 * HG_WIDTH
N_MEM = 256
MEM_HEADS = 4
MEM_HEAD_DIM = 128
MEM_WIDTH = MEM_HEADS * MEM_HEAD_DIM
D_FF = 4 * D_MODEL
EPS = 1e-6

LANES = 128
VMEM_LIMIT = 56 * 1024 * 1024
F32 = jnp.float32
BF16 = jnp.bfloat16

_NT = (((1,), (1,)), ((), ()))
_TN = (((0,), (0,)), ((), ()))


def _params(*sem):
    return pltpu.CompilerParams(dimension_semantics=sem, vmem_limit_bytes=VMEM_LIMIT)


def _const_spec(shape):
    nd = len(shape)
    return pl.BlockSpec(shape, lambda *_: (0,) * nd, pipeline_mode=pl.Buffered(1))


def _log2(n):
    assert n > 0 and n & (n - 1) == 0, n
    return n.bit_length() - 1


def _layer_spec(shape, layer):
    nd = len(shape)
    return pl.BlockSpec((None,) + tuple(shape), lambda *_: (layer,) + (0,) * nd,
                        pipeline_mode=pl.Buffered(1))


def _rms(x, gain):
    ms = jnp.mean(x * x, axis=-1, keepdims=True)
    return x * lax.rsqrt(ms + EPS) * gain


def _log_sigmoid_parts(z):
    l = jnp.log(1.0 + jnp.exp(-jnp.abs(z)))
    ls = jnp.minimum(z, 0.0) - l
    return ls, ls - z


LOG2E = 1.4426950408889634


def _exp2_neg_abs(x):
    return jnp.exp2(-jnp.abs(x))


def _split_bf16(x):
    hi = x.astype(BF16)
    lo = (x - hi.astype(F32)).astype(BF16)
    return hi, lo


def _memkv_kernel(mem_ref, g_ref, wk_ref, wv_ref, mk_ref, mv_ref):
    h = _rms(mem_ref[...], g_ref[0]).astype(BF16)
    mk_ref[0] = jnp.dot(h, wk_ref[0], preferred_element_type=F32)
    mv_ref[0] = jnp.dot(h, wv_ref[0], preferred_element_type=F32)


def _memkv(mem2d, gains, wk, wv):
    n = mem2d.shape[0]
    out = jax.ShapeDtypeStruct((DEPTH, n, MEM_WIDTH), F32)
    return pl.pallas_call(
        _memkv_kernel,
        grid=(DEPTH,),
        in_specs=[
            pl.BlockSpec((n, D_MODEL), lambda l: (0, 0)),
            pl.BlockSpec((1, 1, D_MODEL), lambda l: (l, 0, 0)),
            pl.BlockSpec((1, D_MODEL, MEM_WIDTH), lambda l: (l, 0, 0)),
            pl.BlockSpec((1, D_MODEL, MEM_WIDTH), lambda l: (l, 0, 0)),
        ],
        out_specs=[pl.BlockSpec((1, n, MEM_WIDTH), lambda l: (l, 0, 0))] * 2,
        out_shape=[out, out],
        compiler_params=_params("arbitrary"),
        name="memkv",
    )(mem2d, gains.reshape(DEPTH, 1, D_MODEL), wk, wv)


def _proj_kernel(x_ref, g_ref, w_ref, lbl_ref, *refs, layer, slab, time_minor):
    qsb_ref, ksb_ref, vsb_ref, kbf_ref, vbf_ref, hg_ref = refs[-6:]
    h = _rms(x_ref[...], g_ref[...]).astype(BF16)

    def col(c):
        return jnp.dot(h, w_ref[:, c * SB_WIDTH:(c + 1) * SB_WIDTH], preferred_element_type=F32)

    def put(ref, val):
        out = val.T if time_minor else val.reshape(val.shape[0], SB_HEADS, SB_HEAD_DIM)
        for s in range(ref.shape[0]):
            ref[s, 0] = out if s == slab else jnp.zeros_like(out)

    def put_hg(c, val):
        hg_ref[:, c * HG_WIDTH:(c + 1) * HG_WIDTH] = val.astype(BF16)

    lg = lbl_ref[...]
    ex = jnp.exp(lg - jnp.max(lg, axis=0, keepdims=True))
    sm = ex / jnp.sum(ex, axis=0, keepdims=True)
    csum = sm[0:1]
    for l in range(1, layer + 1):
        csum = csum + sm[l:l + 1]
    lb = csum - sm[0:1]

    zf = col(4)
    ez = jnp.exp(-jnp.abs(zf))
    lsig = jnp.minimum(zf, 0.0) - jnp.log(1.0 + ez)
    bterm = jnp.log(1.0 - lb) + lsig
    la = jnp.log(jnp.maximum(lb, 1e-30))
    logf_mix = jnp.maximum(la, bterm) + jnp.log(1.0 + jnp.exp(-jnp.abs(la - bterm)))
    logf = jnp.where(lb > 0.0, logf_mix, bterm)
    put_hg(1, (1.0 - lb) * (jnp.where(zf >= 0.0, ez, 1.0) / (1.0 + ez)))
    hi, lo = _split_bf16(logf * LOG2E)
    put_hg(3, hi)
    put_hg(4, lo)
    g = col(6)
    eg = jnp.exp(-jnp.abs(g))
    put_hg(5, g * (jnp.where(g >= 0.0, 1.0, eg) / (1.0 + eg)))
    put_hg(0, col(3) * (HG_HEAD_DIM ** -0.5))
    put_hg(2, col(5))

    qsb_ref[...] = (col(0) * (SB_HEAD_DIM ** -0.5)).astype(BF16)
    k = col(1)
    put(ksb_ref, k)
    kbf_ref[...] = k.astype(BF16)
    v = col(2)
    put(vsb_ref, v)
    vbf_ref[...] = v.astype(BF16)


def _proj(x2d, gain, w_in, lb_logits, tm, layer, kv_all, nb, t):
    n = x2d.shape[0]
    row = lambda w: pl.BlockSpec((tm, w), lambda i: (i, 0))
    in_specs = [row(D_MODEL), _const_spec((1, D_MODEL)), _layer_spec((D_MODEL, IN_COLS), layer),
                _const_spec((DEPTH, HG_WIDTH))]
    args = (x2d, gain.reshape(1, D_MODEL), w_in, lb_logits)
    time_minor = t % tm == 0
    if time_minor:
        nq = t // tm
        kv_shape, blk = (DEPTH, nb, SB_WIDTH, t), (1, SB_WIDTH, tm)
        where = lambda i: (i // nq, 0, i % nq)
    else:
        kv_shape, blk = (DEPTH, n // tm, tm, SB_HEADS, SB_HEAD_DIM), (1, tm, SB_HEADS, SB_HEAD_DIM)
        where = lambda i: (i, 0, 0, 0)
    if kv_all is None:
        kv_spec, slab, aliases = pl.BlockSpec((DEPTH,) + blk, lambda i: (0,) + where(i)), layer, {}
    else:
        kv_spec, slab, aliases = pl.BlockSpec((1,) + blk, lambda i: (layer,) + where(i)), 0, {4: 1, 5: 2}
        in_specs += [pl.BlockSpec(memory_space=pl.ANY)] * 2
        args += tuple(kv_all)
    return pl.pallas_call(
        functools.partial(_proj_kernel, layer=layer, slab=slab, time_minor=time_minor),
        grid=(n // tm,),
        in_specs=in_specs,
        out_specs=[row(SB_WIDTH), kv_spec, kv_spec, row(SB_WIDTH), row(SB_WIDTH), row(HG_COLS)],
        out_shape=[
            jax.ShapeDtypeStruct((n, SB_WIDTH), BF16),
            jax.ShapeDtypeStruct(kv_shape, F32),
            jax.ShapeDtypeStruct(kv_shape, F32),
            jax.ShapeDtypeStruct((n, SB_WIDTH), BF16),
            jax.ShapeDtypeStruct((n, SB_WIDTH), BF16),
            jax.ShapeDtypeStruct((n, HG_COLS), BF16),
        ],
        input_output_aliases=aliases,
        compiler_params=_params("arbitrary"),
        name="proj",
    )(*args)


SB_BK = LANES
SB_DEAD = -104.0
SB_NPAIR = 4


def _suffix_matrix():
    j = np.arange(SB_BK)[:, None]
    s = np.arange(SB_BK)[None, :]
    u = np.concatenate([(j > s).astype(np.float32), np.ones((SB_BK, SB_BK), np.float32)], axis=1)
    return jnp.asarray(np.concatenate([u, u], axis=0), dtype=BF16)


def _sbp_kernel(q_ref, k_ref, v_ref, u_ref, o_ref, acc_ref, run_ref, *, bq, npair):
    bk = SB_BK
    qi = pl.program_id(2)
    lane = lax.broadcasted_iota(jnp.int32, (1, LANES), 1)
    m_a = (lane < SB_HEAD_DIM).astype(BF16)
    m_b = (lane >= SB_HEAD_DIM).astype(BF16)
    acc_ref[...] = jnp.zeros_like(acc_ref)
    run_ref[...] = jnp.zeros_like(run_ref)
    u = u_ref[...]
    nd = bq // bk
    assert nd % 2 == 0, "earlier key blocks are taken two per trip"

    def sweep(blocks):
        units = [(j, rel, p) for j, rel in blocks for p in range(npair)]
        ps = lambda p: slice(p * LANES, (p + 1) * LANES)
        keys = lambda j: pl.ds(pl.multiple_of(j * bk, bk), bk)
        qrows = lambda rel: slice(0 if rel is None else rel, bq)
        masks = {rel: (lax.broadcasted_iota(jnp.int32, (bq - rel, bk), 1)
                       < lax.broadcasted_iota(jnp.int32, (bq - rel, bk), 0))
                 for _, rel in blocks if rel is not None}
        zs = []
        for j, rel, p in units:
            ks = k_ref[keys(j), ps(p)]
            kcat = jnp.concatenate([ks * m_a, ks * m_b], axis=0)
            zs.append(lax.dot_general(q_ref[qrows(rel), ps(p)], kcat, _NT, preferred_element_type=F32))
        lss, css = [], []
        for (j, rel, p), z in zip(units, zs):
            for hh in range(2):
                ls, lk = _log_sigmoid_parts(z[:, hh * bk:(hh + 1) * bk])
                if rel is not None:
                    lk = jnp.where(masks[rel], lk, 0.0)
                hi, lo = _split_bf16(lk)
                lss.append(ls)
                css.append(jnp.dot(jnp.concatenate([hi, lo], axis=1), u, preferred_element_type=F32))
        for i, (j, rel, p) in enumerate(units):
            a_parts = []
            for hh in range(2):
                ls, cs = lss[2 * i + hh], css[2 * i + hh]
                run = run_ref[2 * p + hh, qrows(rel)]
                a = jnp.exp(ls + cs[:, :bk] + run)
                if rel is not None:
                    a = jnp.where(masks[rel], a, 0.0)
                run_ref[2 * p + hh, qrows(rel)] = run + cs[:, bk:]
                a_parts.append(a.astype(BF16))
            vs = v_ref[keys(j), ps(p)]
            vcat = jnp.concatenate([vs * m_a, vs * m_b], axis=0)
            acc_ref[qrows(rel), ps(p)] += jnp.dot(jnp.concatenate(a_parts, axis=1), vcat,
                                                  preferred_element_type=F32)

    sweep([(qi * nd + d, d * bk) for d in range(nd - 1, -1, -1)])

    def live():
        return jnp.max(run_ref[...]) > SB_DEAD

    def cond(state):
        j, alive = state
        return jnp.logical_and(j >= 0, alive)

    def body(state):
        j, _ = state
        sweep([(j, None), (j - 1, None)])
        return j - 2, live()

    lax.while_loop(cond, body, (qi * nd - 1, live()))
    o_ref[...] = acc_ref[...]


def _sb_prompt(qb, kb, vb, nb, t, bq):
    nq = t // bq
    w = SB_NPAIR * LANES
    return pl.pallas_call(
        functools.partial(_sbp_kernel, bq=bq, npair=SB_NPAIR),
        grid=(nb, SB_WIDTH // w, nq),
        in_specs=[
            pl.BlockSpec((bq, w), lambda b, h, i: (b * nq + i, h)),
            pl.BlockSpec((t, w), lambda b, h, i: (b, h)),
            pl.BlockSpec((t, w), lambda b, h, i: (b, h)),
            _const_spec((2 * SB_BK, 2 * SB_BK)),
        ],
        out_specs=pl.BlockSpec((bq, w), lambda b, h, i: (b * nq + i, h)),
        out_shape=jax.ShapeDtypeStruct((nb * t, SB_WIDTH), F32),
        scratch_shapes=[pltpu.VMEM((bq, w), F32), pltpu.VMEM((2 * SB_NPAIR, bq, LANES), F32)],
        compiler_params=_params("arbitrary", "arbitrary", "arbitrary"),
        name="sb_prompt",
    )(qb, kb, vb, _suffix_matrix())


def _sbs_kernel(q_ref, kn_ref, vn_ref, kt_ref, vt_ref, u_ref, o_ref, alive_ref, acc_ref, run_ref,
                *, tq, span, spb):
    bk = SB_BK
    nr = SB_HEADS * tq
    streams = range(spb)
    srows = lambda s: slice(s * tq, (s + 1) * tq)
    row = lax.broadcasted_iota(jnp.int32, (nr, SB_WIDTH), 0)
    colw = lax.broadcasted_iota(jnp.int32, (nr, SB_WIDTH), 1)
    same_head = (row >> _log2(tq)) == (colw >> _log2(SB_HEAD_DIM))
    qx = [jnp.where(same_head, jnp.concatenate([q_ref[srows(s)]] * SB_HEADS, axis=0), jnp.zeros((), BF16))
          for s in streams]
    u = u_ref[...]
    acc_ref[...] = jnp.zeros_like(acc_ref)
    run_ref[...] = jnp.zeros_like(run_ref)

    def weights(zs, mask):
        parts = [_log_sigmoid_parts(z) for z in zs]
        css = []
        for ls, lk in parts:
            if mask is not None:
                lk = jnp.where(mask, lk, 0.0)
            hi, lo = _split_bf16(lk)
            css.append(jnp.dot(jnp.concatenate([hi, lo], axis=1), u, preferred_element_type=F32))
        out = []
        for s, ((ls, _), cs) in enumerate(zip(parts, css)):
            run = run_ref[s]
            a = jnp.exp(ls + cs[:, :bk] + run)
            if mask is not None:
                a = jnp.where(mask, a, 0.0)
            run_ref[s] = run + cs[:, bk:]
            out.append(a.astype(BF16))
        return out

    pad = jnp.zeros((bk - tq, SB_WIDTH), BF16)
    skey = lax.broadcasted_iota(jnp.int32, (nr, bk), 1)
    tqry = lax.broadcasted_iota(jnp.int32, (nr, bk), 0) & (tq - 1)
    a = weights([lax.dot_general(qx[s], jnp.concatenate([kn_ref[srows(s)], pad], axis=0), _NT,
                                 preferred_element_type=F32) for s in streams], skey < tqry)
    for s in streams:
        acc_ref[s] += jnp.dot(a[s], jnp.concatenate([vn_ref[srows(s)], pad], axis=0),
                              preferred_element_type=F32)

    def live():
        return jnp.max(run_ref[...]) > SB_DEAD

    for j in range(span // bk - 1, -1, -1):
        @pl.when(live())
        def _():
            cols = slice(j * bk, (j + 1) * bk)
            a = weights([jnp.dot(qx[s], kt_ref[s, :, cols].astype(BF16), preferred_element_type=F32)
                         for s in streams], None)
            for s in streams:
                acc_ref[s] += lax.dot_general(a[s], vt_ref[s, :, cols].astype(BF16), _NT,
                                              preferred_element_type=F32)

    alive_ref[...] = jnp.where(live(), jnp.ones(alive_ref.shape, F32), jnp.zeros(alive_ref.shape, F32))
    for s in streams:
        accm = jnp.where(same_head, acc_ref[s], 0.0)
        out = accm[0:tq]
        for h in range(1, SB_HEADS):
            out = out + accm[h * tq:(h + 1) * tq]
        o_ref[srows(s)] = out


def _sb_sample(qb, kb, vb, kcache, vcache, layer, nb, tq, span):
    past = kcache.shape[2]
    assert past % span == 0 and span % SB_BK == 0
    spb = SBS_SPB if nb % SBS_SPB == 0 and span <= SBS_NEAR else 1
    rowspec = pl.BlockSpec((spb * tq, SB_WIDTH), lambda s: (s, 0))
    cspec = pl.BlockSpec((spb, SB_WIDTH, span), lambda s: (layer * nb // spb + s, 0, past // span - 1))
    nr = SB_HEADS * tq
    return pl.pallas_call(
        functools.partial(_sbs_kernel, tq=tq, span=span, spb=spb),
        grid=(nb // spb,),
        in_specs=[rowspec, rowspec, rowspec, cspec, cspec, _const_spec((2 * SB_BK, 2 * SB_BK))],
        out_specs=[rowspec, pl.BlockSpec((spb, 1, LANES), lambda s: (s, 0, 0))],
        out_shape=[jax.ShapeDtypeStruct((nb * tq, SB_WIDTH), F32),
                   jax.ShapeDtypeStruct((nb, 1, LANES), F32)],
        scratch_shapes=[pltpu.VMEM((spb, nr, SB_WIDTH), F32), pltpu.VMEM((spb, nr, SB_BK), F32)],
        compiler_params=_params("arbitrary"),
        name="sb_sample",
    )(qb, kb, vb, kcache, vcache, _suffix_matrix())


def _hgrn_consts(c):
    levels = []
    m = 1
    while m < c:
        levels.append(m)
        m *= 2
    t = np.arange(c)[:, None]
    u = np.arange(c)[None, :]
    mats = [(u <= t).astype(np.float32)]
    masks = [np.eye(c, dtype=np.float32)]
    for m in levels:
        end_a = (t // (2 * m)) * 2 * m + m - 1
        in_b = (t % (2 * m)) >= m
        w = np.where(in_b, (u > end_a) & (u <= t), False).astype(np.float32)
        w = w - np.where(~in_b, (u > t) & (u <= end_a), False).astype(np.float32)
        mats.append(w)
        masks.append((((t // (2 * m)) == (u // (2 * m))) & in_b & ((u % (2 * m)) < m)).astype(np.float32))
    w = np.concatenate(mats, axis=0)
    return tuple(levels), jnp.asarray(np.concatenate([w, w], axis=1), dtype=BF16), jnp.asarray(np.stack(masks))


def _hgrn_kernel(hg_ref, gain_ref, s0_ref, w_ref, mk_ref, *refs, c, levels, slab, nsteps, spb, cps):
    o_ref, sfin_ref, st_ref = refs[-3:]
    step = pl.program_id(1)
    hd = HG_HEAD_DIM
    assert spb == 1 or cps == 1
    groups = range(spb * cps)
    units = [(g, h) for g in groups for h in range(HG_HEADS)]
    nu = range(len(units))
    slot = lambda g, h: (g if cps == 1 else 0) * HG_HEADS + h

    @pl.when(step == 0)
    def _():
        for s in range(spb):
            for h in range(HG_HEADS):
                st_ref[slot(s, h)] = s0_ref[s, h].T

    rows = lambda g: slice(g * c, (g + 1) * c)
    col = lambda base, g, h: hg_ref[rows(g), base * HG_WIDTH + h * hd:base * HG_WIDTH + (h + 1) * hd]
    w = w_ref[...]
    dd = [jnp.dot(w, jnp.concatenate([hg_ref[rows(g), 3 * HG_WIDTH:4 * HG_WIDTH],
                                      hg_ref[rows(g), 4 * HG_WIDTH:5 * HG_WIDTH]], axis=0),
                  preferred_element_type=F32) for g in groups]
    hsl = lambda h: slice(h * hd, (h + 1) * hd)
    q, k, v = ([col(b, g, h) for g, h in units] for b in range(3))
    cum = [dd[g][0:c, hsl(h)] for g, h in units]

    sc = [lax.dot_general(q[i], k[i], _NT, preferred_element_type=F32) * mk_ref[0] for i in nu]
    for li in range(len(levels)):
        for i, (g, h) in enumerate(units):
            e = _exp2_neg_abs(dd[g][(li + 1) * c:(li + 2) * c, hsl(h)]).astype(BF16)
            sc[i] = sc[i] + lax.dot_general(q[i] * e, k[i] * e, _NT,
                                            preferred_element_type=F32) * mk_ref[li + 1]
    st = {}
    o = [None] * len(units)
    for i, (g, h) in enumerate(units):
        sl_ = slot(g, h)
        cur = st.get(sl_)
        if cur is None:
            cur = st_ref[sl_]
        o[i] = lax.dot_general(q[i] * jnp.exp2(cum[i]).astype(BF16), cur.astype(BF16), _NT,
                               preferred_element_type=F32)
        last = cum[i][c - 1:c, :]
        kdec = k[i] * jnp.exp2(last - cum[i]).astype(BF16)
        st[sl_] = cur * jnp.exp2(last) + lax.dot_general(v[i], kdec, _TN, preferred_element_type=F32)
    for sl_, val in st.items():
        st_ref[sl_] = val
    for i in nu:
        o[i] = o[i] + jnp.dot(sc[i].astype(BF16), v[i], preferred_element_type=F32)
    for i, (g, h) in enumerate(units):
        o_ref[rows(g), hsl(h)] = (_rms(o[i], gain_ref[:, hsl(h)]) * col(5, g, h).astype(F32)).astype(BF16)

    @pl.when(step == nsteps - 1)
    def _():
        for d in range(sfin_ref.shape[0]):
            for s in range(spb):
                for h in range(HG_HEADS):
                    sfin_ref[d, s, h] = st_ref[slot(s, h)].T if d == slab else jnp.zeros((hd, hd), F32)


def _hgrn(hg, hg_gain, s0, s0_base, layer, nb, t, s_all):
    c = min(LANES, t)
    nchunks = t // c
    levels, wcat, masks = _hgrn_consts(c)
    nl = len(levels) + 1
    sdims = (HG_HEADS, HG_HEAD_DIM, HG_HEAD_DIM)
    spb = HG_SPB if nchunks == 1 and nb % HG_SPB == 0 and s0_base % HG_SPB == 0 else 1
    cps = HG_CPS if nchunks % HG_CPS == 0 else 1
    nsteps = nchunks // cps
    rb = spb * cps * c
    in_specs = [
        pl.BlockSpec((rb, HG_COLS), lambda b, s: (b * nsteps + s, 0)),
        _const_spec((1, HG_WIDTH)),
        pl.BlockSpec((spb,) + sdims, lambda b, s: (s0_base // spb + b, 0, 0, 0)),
        _const_spec((nl * c, 2 * c)),
        _const_spec((nl, c, c)),
    ]
    args = (hg, hg_gain.reshape(1, HG_WIDTH), s0, wcat, masks)
    if s_all is None:
        s_spec, slab, aliases = pl.BlockSpec((DEPTH, spb) + sdims, lambda b, s: (0, b, 0, 0, 0)), layer, {}
    else:
        s_spec, slab, aliases = pl.BlockSpec((1, spb) + sdims, lambda b, s: (layer, b, 0, 0, 0)), 0, {5: 1}
        in_specs.append(pl.BlockSpec(memory_space=pl.ANY))
        args += (s_all,)
    return pl.pallas_call(
        functools.partial(_hgrn_kernel, c=c, levels=levels, slab=slab, nsteps=nsteps, spb=spb, cps=cps),
        grid=(nb // spb, nsteps),
        in_specs=in_specs,
        out_specs=[pl.BlockSpec((rb, HG_WIDTH), lambda b, s: (b * nsteps + s, 0)), s_spec],
        out_shape=[
            jax.ShapeDtypeStruct((nb * t, HG_WIDTH), BF16),
            jax.ShapeDtypeStruct((DEPTH, nb) + sdims, F32),
        ],
        input_output_aliases=aliases,
        scratch_shapes=[pltpu.VMEM((spb * HG_HEADS, HG_HEAD_DIM, HG_HEAD_DIM), F32)],
        compiler_params=_params("arbitrary", "arbitrary"),
        name="hgrn",
    )(*args)


def _mix_kernel(x_ref, osb_ref, ohg_ref, sbg_ref, wout_ref, nmq_ref, wmq_ref, *refs):
    x1_ref, out_ref = refs[-2:]
    a = _rms(osb_ref[...], sbg_ref[...]).astype(BF16)
    b = ohg_ref[...]
    x1 = (x_ref[...]
          + jnp.dot(a, wout_ref[0:SB_WIDTH, :], preferred_element_type=F32)
          + jnp.dot(b, wout_ref[SB_WIDTH:, :], preferred_element_type=F32))
    x1_ref[...] = x1
    h = _rms(x1, nmq_ref[...]).astype(BF16)
    qm = jnp.dot(h, wmq_ref[...], preferred_element_type=F32).astype(BF16)
    if len(refs) == 2:
        out_ref[...] = qm
    else:
        _mem_attention(lambda s, sl: qm[:, sl], refs[0], refs[1], out_ref, spb=1, tm=qm.shape[0])


def _mix(x2d, osb, ohg, sb_gain, w_out, norm_mem_q, w_mq, tm, layer, mem=None):
    n = x2d.shape[0]
    row = lambda w: pl.BlockSpec((tm, w), lambda i: (i, 0))
    in_specs = [row(D_MODEL), row(SB_WIDTH), row(HG_WIDTH), _const_spec((1, SB_WIDTH)),
                _layer_spec((D_MODEL, D_MODEL), layer), _const_spec((1, D_MODEL)),
                _layer_spec((D_MODEL, MEM_WIDTH), layer)]
    args = (x2d, osb, ohg, sb_gain.reshape(1, SB_WIDTH), w_out, norm_mem_q.reshape(1, D_MODEL), w_mq)
    if mem is not None:
        mk3, mv3, base, t = mem
        assert t % tm == 0
        mspec = pl.BlockSpec((1, N_MEM, MEM_HEADS, MEM_HEAD_DIM), lambda i: (base + i // (t // tm), 0, 0, 0))
        in_specs += [mspec, mspec]
        args += (mk3, mv3)
    return pl.pallas_call(
        _mix_kernel,
        grid=(n // tm,),
        in_specs=in_specs,
        out_specs=[row(D_MODEL), row(MEM_WIDTH)],
        out_shape=[jax.ShapeDtypeStruct((n, D_MODEL), F32), jax.ShapeDtypeStruct((n, MEM_WIDTH), BF16)],
        compiler_params=_params("arbitrary"),
        name="mix",
    )(*args)


def _mem_attention(q_of, mk_ref, mv_ref, o_ref, *, spb, tm):
    hd = MEM_HEAD_DIM
    mk = [mk_ref[s].reshape(N_MEM, MEM_WIDTH).astype(BF16) for s in range(spb)]
    mv = [mv_ref[s].reshape(N_MEM, MEM_WIDTH).astype(BF16) for s in range(spb)]
    units = [(s, h) for s in range(spb) for h in range(MEM_HEADS)]
    rows = lambda s: slice(s * tm, (s + 1) * tm)
    sl = lambda h: slice(h * hd, (h + 1) * hd)
    sc = [lax.dot_general(q_of(s, sl(h)), mk[s][:, sl(h)], _NT,
                          preferred_element_type=F32) * (hd ** -0.5) for s, h in units]
    p = [jnp.exp(x - jnp.max(x, axis=-1, keepdims=True)) for x in sc]
    den = [jnp.sum(x, axis=-1, keepdims=True) for x in p]
    o = [jnp.dot(x.astype(BF16), mv[s][:, sl(h)], preferred_element_type=F32) for x, (s, h) in zip(p, units)]
    for x, d, (s, h) in zip(o, den, units):
        o_ref[rows(s), sl(h)] = (x / d).astype(BF16)


def _memattn_kernel(q_ref, mk_ref, mv_ref, o_ref, *, spb, tm):
    _mem_attention(lambda s, sl: q_ref[s * tm:(s + 1) * tm, sl], mk_ref, mv_ref, o_ref, spb=spb, tm=tm)


def _memattn(qm, mk3, mv3, base, nb, t, tm):
    nt = t // tm
    spb = MEM_SPB if nt == 1 and nb % MEM_SPB == 0 and base % MEM_SPB == 0 else 1
    mspec = pl.BlockSpec((spb, N_MEM, MEM_HEADS, MEM_HEAD_DIM), lambda b, i: (base // spb + b, 0, 0, 0))
    qspec = pl.BlockSpec((spb * tm, MEM_WIDTH), lambda b, i: (b * nt + i, 0))
    return pl.pallas_call(
        functools.partial(_memattn_kernel, spb=spb, tm=tm),
        grid=(nb // spb, nt),
        in_specs=[qspec, mspec, mspec],
        out_specs=qspec,
        out_shape=jax.ShapeDtypeStruct((nb * t, MEM_WIDTH), BF16),
        compiler_params=_params("arbitrary", "arbitrary"),
        name="memattn",
    )(qm, mk3, mv3)


FF_CHUNK = 1024


def _ffn_kernel(x1_ref, om_ref, wmo_ref, nf_ref, w1_ref, w2_ref, nfin_ref, o_ref, *, final):
    x2 = x1_ref[...] + jnp.dot(om_ref[...], wmo_ref[...], preferred_element_type=F32)
    h = _rms(x2, nf_ref[...]).astype(BF16)
    x3 = x2
    for c in range(D_FF // FF_CHUNK):
        cs = slice(c * FF_CHUNK, (c + 1) * FF_CHUNK)
        r = jnp.maximum(jnp.dot(h, w1_ref[:, cs], preferred_element_type=F32), 0.0)
        x3 = x3 + jnp.dot((r * r).astype(BF16), w2_ref[cs, :], preferred_element_type=F32)
    o_ref[...] = _rms(x3, nfin_ref[...]) if final else x3


def _ffn(x1, om, w_mo, norm_ffn, w1, w2, norm_final, layer, final, tm):
    n = x1.shape[0]
    row = lambda w: pl.BlockSpec((tm, w), lambda i: (i, 0))
    return pl.pallas_call(
        functools.partial(_ffn_kernel, final=final),
        grid=(n // tm,),
        in_specs=[row(D_MODEL), row(MEM_WIDTH), _layer_spec((MEM_WIDTH, D_MODEL), layer),
                  _const_spec((1, D_MODEL)), _layer_spec((D_MODEL, D_FF), layer),
                  _layer_spec((D_FF, D_MODEL), layer),
                  _const_spec((1, D_MODEL))],
        out_specs=row(D_MODEL),
        out_shape=jax.ShapeDtypeStruct((n, D_MODEL), F32),
        compiler_params=_params("arbitrary"),
        name="ffn",
    )(x1, om, w_mo, norm_ffn.reshape(1, D_MODEL), w1, w2, norm_final.reshape(1, D_MODEL))


TM = 512
TM_FFN = 1024
MEM_SPB = 8
SBS_SPB = 8
HG_SPB = 8
HG_CPS = 4
SB_BQ = 256
SBS_NEAR = 256


def kernel(x_prompt, x_sample, mem_prompt, cache_sb_k, cache_sb_v, state_hgrn, cache_mem_k, cache_mem_v,
           lb_logits, norm_mix, w_in, sb_gain, hg_gain, w_out, norm_mem_q, norm_mem_kv, w_mq, w_mk, w_mv,
           w_mo, norm_ffn, w_ffn1, w_ffn2, norm_final):
    nbp, tp, _ = x_prompt.shape
    nbs, ts, _ = x_sample.shape
    past = cache_sb_k.shape[2]
    w_in_b, w_out_b, w_mq_b, w_mk_b, w_mv_b, w_mo_b, w1_b, w2_b = (
        w.astype(BF16) for w in (w_in, w_out, w_mq, w_mk, w_mv, w_mo, w_ffn1, w_ffn2))
    lbl = lb_logits.astype(F32)

    mk_p, mv_p = _memkv(mem_prompt.reshape(nbp * N_MEM, D_MODEL), norm_mem_kv, w_mk_b, w_mv_b)
    mdims = (N_MEM, MEM_HEADS, MEM_HEAD_DIM)
    mk_p3 = mk_p.reshape((DEPTH * nbp,) + mdims)
    mv_p3 = mv_p.reshape((DEPTH * nbp,) + mdims)
    mk_s3 = cache_mem_k.reshape((DEPTH * nbs,) + mdims)
    mv_s3 = cache_mem_v.reshape((DEPTH * nbs,) + mdims)
    s0_s = state_hgrn.reshape(DEPTH * nbs, HG_HEADS, HG_HEAD_DIM, HG_HEAD_DIM)
    s0_p = jnp.zeros((nbp, HG_HEADS, HG_HEAD_DIM, HG_HEAD_DIM), F32)
    near = min(SBS_NEAR, past)
    kc4 = cache_sb_k.transpose(0, 1, 3, 4, 2).reshape(DEPTH * nbs, SB_WIDTH, past)
    vc4 = cache_sb_v.transpose(0, 1, 3, 4, 2).reshape(DEPTH * nbs, SB_WIDTH, past)

    def sb_sample(l, qb, kb, vb, nb, t):
        o_near, alive = _sb_sample(qb, kb, vb, kc4, vc4, l, nb, t, near)
        if near == past:
            return o_near
        return lax.cond(jnp.max(alive) > 0.0,
                        lambda: _sb_sample(qb, kb, vb, kc4, vc4, l, nb, t, past)[0],
                        lambda: o_near)

    def layer(l, x2d, nb, t, prompt, carried):
        kv_all, s_all = carried
        qb, k_all, v_all, kb, vb, hg = _proj(x2d, norm_mix[l], w_in_b, lbl, TM, l, kv_all, nb, t)
        if prompt:
            osb = _sb_prompt(qb, kb, vb, nb, t, SB_BQ)
            ohg, s_all = _hgrn(hg, hg_gain[l], s0_p, 0, l, nb, t, s_all)
            mk3, mv3, base = mk_p3, mv_p3, l * nb
        else:
            osb = sb_sample(l, qb, kb, vb, nb, t)
            ohg, s_all = _hgrn(hg, hg_gain[l], s0_s, l * nb, l, nb, t, s_all)
            mk3, mv3, base = mk_s3, mv_s3, l * nb
        if t % TM_FFN == 0:
            x1, om = _mix(x2d, osb, ohg, sb_gain[l], w_out_b, norm_mem_q[l], w_mq_b, TM_FFN, l,
                          mem=(mk3, mv3, base, t))
        else:
            x1, qm = _mix(x2d, osb, ohg, sb_gain[l], w_out_b, norm_mem_q[l], w_mq_b, TM, l)
            om = _memattn(qm, mk3, mv3, base, nb, t, min(TM, t))
        xo = _ffn(x1, om, w_mo_b, norm_ffn[l], w1_b, w2_b, norm_final, l, l == DEPTH - 1,
                  min(TM_FFN, nb * t))
        return xo, ((k_all, v_all), s_all)

    xp = x_prompt.reshape(nbp * tp, D_MODEL)
    xs = x_sample.reshape(nbs * ts, D_MODEL)
    out_p = out_s = (None, None)
    for l in range(DEPTH):
        xp, out_p = layer(l, xp, nbp, tp, True, out_p)
        xs, out_s = layer(l, xs, nbs, ts, False, out_s)
    (kp, vp), sp = out_p
    (kn, vn), sn = out_s

    def heads_last(a):
        return a.reshape(DEPTH, nbp, SB_HEADS, SB_HEAD_DIM, tp).transpose(0, 1, 4, 2, 3)

    sbs = (DEPTH, nbs, ts, SB_HEADS, SB_HEAD_DIM)
    memp = (DEPTH, nbp, N_MEM, MEM_HEADS, MEM_HEAD_DIM)
    return (xp.reshape(nbp, tp, D_MODEL), xs.reshape(nbs, ts, D_MODEL),
            heads_last(kp), heads_last(vp), sp, mk_p3.reshape(memp), mv_p3.reshape(memp),
            kn.reshape(sbs), vn.reshape(sbs), sn)
```
